```python
import jax, jax.numpy as jnp
from jax import lax
import numpy as np

D_MODEL = 1024
BATCH = 8
SEQ = 4096
DEPTH = 4

GRID_W = 64
BLOCK = 128
EPS = 1e-6

RET_HEADS = 4
RET_DIM = 64
RET_W = RET_HEADS * RET_DIM
LRU_BLOCKS = 4
LRU_BLOCK_DIM = 64
LRU_W = LRU_BLOCKS * LRU_BLOCK_DIM
LRU_CONV = 4
LRU_C = 8.0
ATT_HEADS = 8
ATT_KV_HEADS = 2
ATT_GROUP = ATT_HEADS // ATT_KV_HEADS
ATT_DIM = 64
ATT_W = ATT_HEADS * ATT_DIM
KV_W = ATT_KV_HEADS * ATT_DIM
ROPE_BASE = 10000.0

MIX_W = RET_W + LRU_W + ATT_W
IN_W = 4 * RET_W + 2 * LRU_W + ATT_W + 2 * KV_W

PEER_HEADS = 8
PEER_KEYS = 128
PEER_EXPERTS = PEER_KEYS * PEER_KEYS
PEER_HALF = 128
PEER_QDIM = 2 * PEER_HALF
PEER_TOPK = 16
PEER_CHUNK = 128

kernel_name = "hybrid_ret_lru_gqa_peer_encoder"


def rms_norm(x, w):
    xf = x.astype(jnp.float32)
    y = xf * lax.rsqrt(jnp.mean(xf * xf, axis=-1, keepdims=True) + EPS)
    return (y * w.astype(jnp.float32)).astype(x.dtype)


def rotate_half(x):
    x1, x2 = jnp.split(x, 2, axis=-1)
    return jnp.concatenate([-x2, x1], axis=-1)


def axial_rotate_half(x):
    xr, xc = jnp.split(x, 2, axis=-1)
    return jnp.concatenate([rotate_half(xr), rotate_half(xc)], axis=-1)


def rope_table(pos, inv_freq):
    ang = pos[:, None] * inv_freq[None, :]
    ang = jnp.concatenate([ang, ang], axis=-1)
    return jnp.cos(ang), jnp.sin(ang)


def split_cols(t, widths):
    return jnp.split(t, [int(o) for o in np.cumsum(widths)[:-1]], axis=-1)


def retention_dir(q, k, v, log_g, include_diag):
    B, H, S, Dk = q.shape
    Dv = v.shape[-1]
    n = S // BLOCK
    qc = q.reshape(B, H, n, BLOCK, Dk)
    kc = k.reshape(B, H, n, BLOCK, Dk)
    vc = v.reshape(B, H, n, BLOCK, Dv)
    idx = jnp.arange(BLOCK, dtype=jnp.float32)
    diff = idx[:, None] - idx[None, :]
    mask = (diff >= 0) if include_diag else (diff > 0)
    lg = log_g.astype(jnp.float32)
    dmat = jnp.where(mask[None], jnp.exp(lg[:, None, None] * jnp.where(mask, diff, 0.0)[None]), 0.0)
    scores = jnp.einsum('bhncd,bhnmd->bhncm', qc, kc) * dmat[None, :, None]
    o_intra = jnp.einsum('bhncm,bhnme->bhnce', scores, vc)
    k_w = jnp.exp(lg[:, None] * (BLOCK - 1 - idx)[None])
    kv = jnp.einsum('bhncd,bhnce->bhnde', kc * k_w[None, :, None, :, None], vc)
    chunk_decay = jnp.exp(lg * BLOCK)[None, :, None, None]

    def step(state, kv_c):
        return state * chunk_decay + kv_c, state

    _, prev = lax.scan(step, jnp.zeros((B, H, Dk, Dv), jnp.float32), jnp.moveaxis(kv, 2, 0))
    prev = jnp.moveaxis(prev, 0, 2)
    q_w = jnp.exp(lg[:, None] * (idx + 1.0)[None])
    o_cross = jnp.einsum('bhncd,bhnde->bhnce', qc * q_w[None, :, None, :, None], prev)
    return (o_intra + o_cross).reshape(B, H, S, Dv)


def retention_group(q, k, v, g, log_decay, gn_w, cos, sin):
    B, S, _ = q.shape

    def heads(t):
        return t.reshape(B, S, RET_HEADS, RET_DIM).transpose(0, 2, 1, 3).astype(jnp.float32)

    qh = heads(q)
    kh = heads(k)
    qh = qh * cos + rotate_half(qh) * sin
    kh = (kh * cos + rotate_half(kh) * sin) * (RET_DIM ** -0.5)
    vh = heads(v)
    fwd = retention_dir(qh, kh, vh, log_decay[0], True)
    bwd = jnp.flip(retention_dir(jnp.flip(qh, 2), jnp.flip(kh, 2), jnp.flip(vh, 2), log_decay[1], False), 2)
    o = fwd + bwd
    o = o * lax.rsqrt(jnp.mean(o * o, axis=-1, keepdims=True) + EPS)
    o = o.transpose(0, 2, 1, 3).reshape(B, S, RET_W) * gn_w.astype(jnp.float32)
    return (jax.nn.silu(g.astype(jnp.float32)) * o).astype(q.dtype)


def _lin_combine(left, right):
    a_l, b_l = left
    a_r, b_r = right
    return a_l * a_r, a_r * b_l + b_r


def rglru_group(xb, gb, conv_w, conv_b, gate_w, gate_b, lam):
    B, S, W = xb.shape
    left = LRU_CONV // 2
    xc = lax.conv_general_dilated(xb, conv_w[:, None, :], window_strides=(1,),
                                  padding=[(left, LRU_CONV - 1 - left)],
                                  dimension_numbers=('NWC', 'WIO', 'NWC'),
                                  feature_group_count=W) + conv_b
    xf = xc.astype(jnp.float32)
    blk = xf.reshape(B, S, LRU_BLOCKS, LRU_BLOCK_DIM)
    gates = jnp.einsum('bsnk,dgnkm->dgbsnm', blk, gate_w.astype(jnp.float32)).reshape(2, 2, B, S, W)
    gates = gates + gate_b.astype(jnp.float32)[:, :, None, None, :]
    r = jax.nn.sigmoid(gates[:, 0])
    i = jax.nn.sigmoid(gates[:, 1])
    log_a = -LRU_C * r * jax.nn.softplus(-lam.astype(jnp.float32))[:, None, None, :]
    a = jnp.exp(log_a)
    b = jnp.sqrt(-jnp.expm1(2.0 * log_a)) * (i * xf[None])
    _, h_f = lax.associative_scan(_lin_combine, (a[0], b[0]), axis=1)
    _, h_b = lax.associative_scan(_lin_combine, (jnp.flip(a[1], 1), jnp.flip(b[1], 1)), axis=1)
    h = h_f + jnp.flip(h_b, 1)
    return (h * jax.nn.gelu(gb.astype(jnp.float32))).astype(xb.dtype)


def attention_group(q, k, v, qn_w, kn_w, cos, sin):
    B, S, _ = q.shape
    qh = rms_norm(q.reshape(B, S, ATT_KV_HEADS, ATT_GROUP, ATT_DIM), qn_w)
    kh = rms_norm(k.reshape(B, S, ATT_KV_HEADS, ATT_DIM), kn_w)
    vh = v.reshape(B, S, ATT_KV_HEADS, ATT_DIM)
    qh = (qh * cos[:, None, None, :] + axial_rotate_half(qh) * sin[:, None, None, :]).astype(q.dtype)
    kh = (kh * cos[:, None, :] + axial_rotate_half(kh) * sin[:, None, :]).astype(k.dtype)
    scale = ATT_DIM ** -0.5
    nb = S // BLOCK
    qb = qh.reshape(B, nb, BLOCK, ATT_KV_HEADS, ATT_GROUP, ATT_DIM).transpose(1, 0, 2, 3, 4, 5)

    def block(qblk):
        s = jnp.einsum('bckgd,bskd->bkgcs', qblk, kh, preferred_element_type=jnp.float32) * scale
        p = jax.nn.softmax(s, axis=-1).astype(vh.dtype)
        return jnp.einsum('bkgcs,bskd->bckgd', p, vh)

    o = lax.map(block, qb)
    return o.transpose(1, 0, 2, 3, 4, 5).reshape(B, S, ATT_W)


def peer(x, wq, keys, u, v):
    B, S, D = x.shape
    T = B * S
    xt = x.reshape(T, D)
    q = (xt @ wq).reshape(T, PEER_HEADS, 2, PEER_HALF)
    s = jnp.einsum('thpk,hpnk->thpn', q, keys, preferred_element_type=jnp.float32)
    sv, si = lax.top_k(s, PEER_TOPK)
    cand = (sv[:, :, 0, :, None] + sv[:, :, 1, None, :]).reshape(T, PEER_HEADS, PEER_TOPK * PEER_TOPK)
    cand_idx = (si[:, :, 0, :, None] * PEER_KEYS + si[:, :, 1, None, :]).reshape(T, PEER_HEADS, PEER_TOPK * PEER_TOPK)
    fv, fi = lax.top_k(cand, PEER_TOPK)
    idx = jnp.take_along_axis(cand_idx, fi, axis=-1)
    g = jax.nn.softmax(fv, axis=-1).astype(x.dtype)
    nc = T // PEER_CHUNK

    def chunk(args):
        xc, ic, gc = args
        ue = jnp.take(u, ic, axis=0)
        hid = jax.nn.gelu(jnp.einsum('cd,chkd->chk', xc, ue))
        ve = jnp.take(v, ic, axis=0)
        return jnp.einsum('chk,chkd->cd', gc * hid, ve)

    y = lax.map(chunk, (xt.reshape(nc, PEER_CHUNK, D),
                        idx.reshape(nc, PEER_CHUNK, PEER_HEADS, PEER_TOPK),
                        g.reshape(nc, PEER_CHUNK, PEER_HEADS, PEER_TOPK)))
    return y.reshape(B, S, D)


def setup_inputs(seed: int = 0) -> dict:
    key = jax.random.key(seed)
    ks = jax.random.split(key, 24)
    L, D = DEPTH, D_MODEL
    f32 = jnp.float32
    nrm = lambda k, shape, s: jax.random.normal(k, shape, f32) * s
    x = nrm(ks[0], (BATCH, SEQ, D), 1.0)
    ln1_w = 1.0 + nrm(ks[1], (L, D), 0.01)
    w_in = nrm(ks[2], (L, D, IN_W), D ** -0.5)
    base = jnp.log(1.0 - 2.0 ** (-5.0 - jnp.arange(RET_HEADS, dtype=f32)))
    ret_log_decay = base[None, None, :] * (1.0 + nrm(ks[3], (L, 2, RET_HEADS), 0.05))
    ret_gn_w = 1.0 + nrm(ks[4], (L, RET_W), 0.01)
    lru_conv_w = nrm(ks[5], (L, LRU_CONV, LRU_W), LRU_CONV ** -0.5)
    lru_conv_b = nrm(ks[6], (L, LRU_W), 0.01)
    lru_gate_w = nrm(ks[7], (L, 2, 2, LRU_BLOCKS, LRU_BLOCK_DIM, LRU_BLOCK_DIM), LRU_BLOCK_DIM ** -0.5)
    lru_gate_b = nrm(ks[8], (L, 2, 2, LRU_W), 0.01)
    a_c = jax.random.uniform(ks[9], (L, 2, LRU_W), f32, 0.9, 0.999)
    a0 = a_c ** (1.0 / LRU_C)
    lru_lambda = jnp.log(a0) - jnp.log1p(-a0)
    attn_q_norm = 1.0 + nrm(ks[10], (L, ATT_DIM), 0.01)
    attn_k_norm = 1.0 + nrm(ks[11], (L, ATT_DIM), 0.01)
    w_out = nrm(ks[12], (L, MIX_W, D), (2.0 * MIX_W) ** -0.5)
    ln2_w = 1.0 + nrm(ks[13], (L, D), 0.01)
    peer_wq = nrm(ks[14], (L, D, PEER_HEADS * PEER_QDIM), D ** -0.5)
    peer_keys = nrm(ks[15], (L, PEER_HEADS, 2, PEER_KEYS, PEER_HALF), PEER_HALF ** -0.5)
    peer_u = nrm(ks[16], (L, PEER_EXPERTS, D), D ** -0.5)
    peer_v = nrm(ks[17], (L, PEER_EXPERTS, D), D ** -0.5)
    lnf_w = 1.0 + nrm(ks[18], (D,), 0.01)
    return {"x": x, "ln1_w": ln1_w, "w_in": w_in, "ret_log_decay": ret_log_decay,
            "ret_gn_w": ret_gn_w, "lru_conv_w": lru_conv_w, "lru_conv_b": lru_conv_b,
            "lru_gate_w": lru_gate_w, "lru_gate_b": lru_gate_b, "lru_lambda": lru_lambda,
            "attn_q_norm": attn_q_norm, "attn_k_norm": attn_k_norm, "w_out": w_out,
            "ln2_w": ln2_w, "peer_wq": peer_wq, "peer_keys": peer_keys, "peer_u": peer_u,
            "peer_v": peer_v, "lnf_w": lnf_w}


def reference(x, ln1_w, w_in, ret_log_decay, ret_gn_w, lru_conv_w, lru_conv_b, lru_gate_w,
              lru_gate_b, lru_lambda, attn_q_norm, attn_k_norm, w_out, ln2_w, peer_wq,
              peer_keys, peer_u, peer_v, lnf_w):
    S = x.shape[1]
    n_rows = S // GRID_W
    f32 = jnp.float32
    t = jnp.arange(S, dtype=f32)
    rows = jnp.repeat(jnp.arange(n_rows, dtype=f32), GRID_W)
    cols = jnp.tile(jnp.arange(GRID_W, dtype=f32), n_rows)
    ret_inv = 1.0 / (10000.0 ** jnp.linspace(0.0, 1.0, RET_DIM // 2, dtype=f32))
    ret_cos, ret_sin = rope_table(t, ret_inv)
    ax_n = ATT_DIM // 4
    ax_inv = ROPE_BASE ** (-jnp.arange(ax_n, dtype=f32) / ax_n)
    cr, sr = rope_table(rows, ax_inv)
    cc, sc = rope_table(cols, ax_inv)
    ax_cos = jnp.concatenate([cr, cc], axis=-1)
    ax_sin = jnp.concatenate([sr, sc], axis=-1)
    widths = [RET_W] * 4 + [LRU_W] * 2 + [ATT_W, KV_W, KV_W]
    for l in range(DEPTH):
        h = rms_norm(x, ln1_w[l])
        proj = h @ w_in[l]
        rq, rk, rv, rg, lx, lgt, aq, ak, av = split_cols(proj, widths)
        o_ret = retention_group(rq, rk, rv, rg, ret_log_decay[l], ret_gn_w[l], ret_cos, ret_sin)
        o_lru = rglru_group(lx, lgt, lru_conv_w[l], lru_conv_b[l], lru_gate_w[l], lru_gate_b[l], lru_lambda[l])
        o_att = attention_group(aq, ak, av, attn_q_norm[l], attn_k_norm[l], ax_cos, ax_sin)
        x = x + jnp.concatenate([o_ret, o_lru, o_att], axis=-1) @ w_out[l]
        x = x + peer(rms_norm(x, ln2_w[l]), peer_wq[l], peer_keys[l], peer_u[l], peer_v[l])
    return rms_norm(x, lnf_w)
```

```python
import functools

import jax
import jax.numpy as jnp
from jax import lax
from jax.experimental import pallas as pl
from jax.experimental.pallas import tpu as pltpu

F32 = jnp.float32
BF16 = jnp.bfloat16
HIGHEST = lax.Precision.HIGHEST
EPS = 1e-6

GRID_W = 64
CHUNK = 128
HEAD_DIM = 64
RET_HEADS = 4
RET_W = RET_HEADS * HEAD_DIM
LRU_W = 256
LRU_CONV = 4
LRU_C = 8.0
ATT_HEADS = 8
ATT_KV_HEADS = 2
ATT_GROUP = ATT_HEADS // ATT_KV_HEADS
ATT_W = ATT_HEADS * HEAD_DIM
KV_W = ATT_KV_HEADS * HEAD_DIM
ROPE_BASE = 10000.0
PEER_TOPK = 16

VMEM_LIMIT_BYTES = 56 * 1024 * 1024
NEG_INF = float("-inf")


def _params(*sem):
    return pltpu.CompilerParams(dimension_semantics=sem, vmem_limit_bytes=VMEM_LIMIT_BYTES)


def _rms(x, w):
    return x * lax.rsqrt(jnp.mean(x * x, axis=-1, keepdims=True) + EPS) * w


def _gelu_tanh(x):
    return 0.5 * x * (1.0 + jnp.tanh(0.7978845608028654 * (x + 0.044715 * (x * x * x))))


def _sigmoid(x):
    return 1.0 / (1.0 + jnp.exp(-x))


def _nt_dot(a, b, precision=None):
    return lax.dot_general(a, b, (((1,), (1,)), ((), ())), precision=precision,
                           preferred_element_type=F32)


def _tn_dot(a, b, precision=None):
    return lax.dot_general(a, b, (((0,), (0,)), ((), ())), precision=precision,
                           preferred_element_type=F32)


def _inproj_body(x_ref, lnw_ref, w_ref, *out_refs, widths):
    hb = _rms(x_ref[...], lnw_ref[...]).astype(BF16)
    off = 0
    for o_ref, wd in zip(out_refs, widths):
        o_ref[...] = jnp.dot(hb, w_ref[:, off:off + wd],
                             preferred_element_type=F32).astype(o_ref.dtype)
        off += wd


def _inproj(x2, lnw, w_bf, widths, tm):
    T, D = x2.shape
    n_in = w_bf.shape[1]
    return pl.pallas_call(
        functools.partial(_inproj_body, widths=widths),
        grid=(T // tm,),
        in_specs=[pl.BlockSpec((tm, D), lambda i: (i, 0)),
                  pl.BlockSpec((1, D), lambda i: (0, 0)),
                  pl.BlockSpec((D, n_in), lambda i: (0, 0))],
        out_specs=[pl.BlockSpec((tm, wd), lambda i: (i, 0)) for wd in widths],
        out_shape=[jax.ShapeDtypeStruct((T, wd), F32) for wd in widths],
        compiler_params=_params("parallel"),
        name="inproj",
    )(x2, lnw, w_bf)


def _ret_body(ld_ref, q_ref, k_ref, v_ref, g_ref, cos_ref, sin_ref, lgl_ref, gnw_ref, o_ref,
              qr_s, kr_s, st_s, d_s, *, n_chunks):
    C, W = CHUNK, RET_W
    lane = lax.broadcasted_iota(jnp.int32, (1, W), 1)
    first_half = (lane % HEAD_DIM) < (HEAD_DIM // 2)
    head_masks = [(lane // HEAD_DIM) == h for h in range(RET_HEADS)]
    same_head = (lax.broadcasted_iota(jnp.int32, (W, W), 0) // HEAD_DIM
                 == lax.broadcasted_iota(jnp.int32, (W, W), 1) // HEAD_DIM)
    bd = same_head.astype(F32)
    lgf = lgl_ref[0:1, :]
    lgb = lgl_ref[1:2, :]
    idx = lax.broadcasted_iota(jnp.int32, (C, 1), 0).astype(F32)
    decf = jnp.exp(lgf * float(C))
    decb = jnp.exp(lgb * float(C))

    diff = (lax.broadcasted_iota(jnp.int32, (C, C), 0)
            - lax.broadcasted_iota(jnp.int32, (C, C), 1)).astype(F32)
    for h in range(RET_HEADS):
        d_s[h] = jnp.where(diff >= 0.0,
                           jnp.exp(ld_ref[0, h] * jnp.maximum(diff, 0.0)),
                           jnp.exp(ld_ref[1, h] * jnp.maximum(-diff, 0.0)))

    def rot_half(x):
        return jnp.where(first_half, -pltpu.roll(x, W - HEAD_DIM // 2, 1),
                         pltpu.roll(x, HEAD_DIM // 2, 1))

    def prep(c, carry):
        r0 = pl.multiple_of(c * C, C)
        cs = cos_ref[pl.ds(r0, C), :]
        sn = sin_ref[pl.ds(r0, C), :]
        q = q_ref[pl.ds(r0, C), :]
        k = k_ref[pl.ds(r0, C), :]
        qr_s[pl.ds(r0, C), :] = q * cs + rot_half(q) * sn
        kr_s[pl.ds(r0, C), :] = (k * cs + rot_half(k) * sn) * (HEAD_DIM ** -0.5)
        return carry

    lax.fori_loop(0, n_chunks, prep, 0)

    st_s[...] = jnp.zeros((W, W), F32)

    def fwd(c, carry):
        r0 = pl.multiple_of(c * C, C)
        q = qr_s[pl.ds(r0, C), :]
        k = kr_s[pl.ds(r0, C), :]
        v = v_ref[pl.ds(r0, C), :]
        st = st_s[...]
        qw = q * jnp.exp(lgf * (idx + 1.0))
        o_ref[pl.ds(r0, C), :] = jnp.dot(qw, st * bd, precision=HIGHEST,
                                         preferred_element_type=F32)
        kw = k * jnp.exp(lgf * (float(C) - 1.0 - idx))
        st_s[...] = st * decf + _tn_dot(kw, v, HIGHEST)
        return carry

    lax.fori_loop(0, n_chunks, fwd, 0)

    st_s[...] = jnp.zeros((W, W), F32)

    def bwd(t, carry):
        c = n_chunks - 1 - t
        r0 = pl.multiple_of(c * C, C)
        q = qr_s[pl.ds(r0, C), :]
        k = kr_s[pl.ds(r0, C), :]
        v = v_ref[pl.ds(r0, C), :]
        st = st_s[...]
        qw = q * jnp.exp(lgb * (float(C) - idx))
        o = o_ref[pl.ds(r0, C), :] + jnp.dot(qw, st * bd, precision=HIGHEST,
                                              preferred_element_type=F32)
        kw = k * jnp.exp(lgb * idx)
        st_s[...] = st * decb + _tn_dot(kw, v, HIGHEST)
        for h in range(RET_HEADS):
            qm = jnp.where(head_masks[h], q, 0.0)
            s = _nt_dot(qm, k, HIGHEST) * d_s[h]
            oh = jnp.dot(s, v, precision=HIGHEST, preferred_element_type=F32)
            o = o + jnp.where(head_masks[h], oh, 0.0)
        ms = jnp.dot(o * o, bd, precision=HIGHEST, preferred_element_type=F32) * (1.0 / HEAD_DIM)
        on = o * lax.rsqrt(ms + EPS) * gnw_ref[...]
        g = g_ref[pl.ds(r0, C), :]
        o_ref[pl.ds(r0, C), :] = (g * _sigmoid(g)) * on
        return carry

    lax.fori_loop(0, n_chunks, bwd, 0)


def _retention(rq, rk, rv, rg, cos, sin, log_decay, gn_w, B, S):
    W = RET_W
    lgl = jnp.repeat(log_decay, HEAD_DIM, axis=1)
    seq = pl.BlockSpec((S, W), lambda b: (b, 0), pipeline_mode=pl.Buffered(1))
    tab = pl.BlockSpec((S, W), lambda b: (0, 0), pipeline_mode=pl.Buffered(1))
    return pl.pallas_call(
        functools.partial(_ret_body, n_chunks=S // CHUNK),
        grid=(B,),
        in_specs=[pl.BlockSpec(memory_space=pltpu.SMEM), seq, seq, seq, seq, tab, tab,
                  pl.BlockSpec((2, W), lambda b: (0, 0)),
                  pl.BlockSpec((1, W), lambda b: (0, 0))],
        out_specs=pl.BlockSpec((S, W), lambda b: (b, 0)),
        out_shape=jax.ShapeDtypeStruct((B * S, W), F32),
        scratch_shapes=[pltpu.VMEM((S, W), F32), pltpu.VMEM((S, W), F32),
                        pltpu.VMEM((W, W), F32), pltpu.VMEM((RET_HEADS, CHUNK, CHUNK), F32)],
        compiler_params=_params("parallel"),
        name="retention",
    )(log_decay, rq, rk, rv, rg, cos, sin, lgl, gn_w)


def _lru_body(x_ref, gt_ref, cw_ref, cb_ref, gw_ref, gb_ref, lam_ref, o_ref, xc_s, h_s,
              *, n_blocks, tb):
    W = LRU_W
    S = n_blocks * tb
    z = -lam_ref[...]
    softplus = jnp.maximum(z, 0.0) + jnp.log(1.0 + jnp.exp(-jnp.abs(z)))
    cl = -LRU_C * softplus
    row = lax.broadcasted_iota(jnp.int32, (tb, 1), 0)
    left = LRU_CONV // 2

    def conv_block(kb):
        r0 = pl.multiple_of(kb * tb, tb)
        prev = x_ref[pl.ds(pl.multiple_of(jnp.maximum(r0 - 8, 0), 8), 8), :]
        nxt = x_ref[pl.ds(pl.multiple_of(jnp.minimum(r0 + tb, S - 8), 8), 8), :]
        ext = jnp.concatenate([prev, x_ref[pl.ds(r0, tb), :], nxt], axis=0)
        t = row + r0
        acc = cb_ref[...] + jnp.zeros((tb, W), F32)
        for j in range(LRU_CONV):
            off = j - left
            if off == 0:
                xs = ext[8:8 + tb]
            else:
                xs = pltpu.roll(ext, (-off) % (tb + 16), 0)[8:8 + tb]
                xs = jnp.where((t + off >= 0) & (t + off < S), xs, 0.0)
            acc = acc + xs * cw_ref[j:j + 1, :]
        return acc

    def gates(xc, d):
        g = jnp.dot(xc, gw_ref[:, d * 2 * W:(d + 1) * 2 * W], precision=HIGHEST,
                    preferred_element_type=F32) + gb_ref[:, d * 2 * W:(d + 1) * 2 * W]
        r = _sigmoid(g[:, :W])
        i = _sigmoid(g[:, W:])
        log_a = cl[d:d + 1, :] * r
        a = jnp.exp(log_a)
        b = jnp.sqrt(-jnp.tanh(log_a) * (a * a + 1.0)) * (i * xc)
        return a, b

    def scan_block(a, b, reverse):
        d = 1
        while d < tb:
            if reverse:
                keep = row < tb - d
                a_sh = jnp.where(keep, pltpu.roll(a, tb - d, 0), 1.0)
                b_sh = jnp.where(keep, pltpu.roll(b, tb - d, 0), 0.0)
            else:
                keep = row >= d
                a_sh = jnp.where(keep, pltpu.roll(a, d, 0), 1.0)
                b_sh = jnp.where(keep, pltpu.roll(b, d, 0), 0.0)
            b = a * b_sh + b
            a = a * a_sh
            d *= 2
        return a, b

    def fwd(kb, carry):
        r0 = pl.multiple_of(kb * tb, tb)
        xc = conv_block(kb)
        xc_s[pl.ds(r0, tb), :] = xc
        a, b = gates(xc, 0)
        a, b = scan_block(a, b, False)
        h = a * carry + b
        h_s[pl.ds(r0, tb), :] = h
        return h[tb - 1:tb, :]

    lax.fori_loop(0, n_blocks, fwd, jnp.zeros((1, W), F32))

    def bwd(t, carry):
        kb = n_blocks - 1 - t
        r0 = pl.multiple_of(kb * tb, tb)
        xc = xc_s[pl.ds(r0, tb), :]
        a, b = gates(xc, 1)
        a, b = scan_block(a, b, True)
        h = a * carry + b
        o_ref[pl.ds(r0, tb), :] = (h_s[pl.ds(r0, tb), :] + h) * _gelu_tanh(gt_ref[pl.ds(r0, tb), :])
        return h[0:1, :]

    lax.fori_loop(0, n_blocks, bwd, jnp.zeros((1, W), F32))


def _lru(lx, lgt, conv_w, conv_b, gate_w, gate_b, lam, B, S, tb):
    W = LRU_W
    nb = W // HEAD_DIM
    eye = jnp.eye(nb, dtype=F32)
    gw = jnp.einsum('dgnkm,nj->nkdgjm', gate_w, eye).reshape(W, 4 * W)
    gb = gate_b.reshape(1, 4 * W)
    seq = pl.BlockSpec((S, W), lambda b: (b, 0))
    const = lambda shape: pl.BlockSpec(shape, lambda b: (0,) * len(shape))
    return pl.pallas_call(
        functools.partial(_lru_body, n_blocks=S // tb, tb=tb),
        grid=(B,),
        in_specs=[seq, seq, const((LRU_CONV, W)), const((1, W)), const((W, 4 * W)),
                  const((1, 4 * W)), const((2, W))],
        out_specs=seq,
        out_shape=jax.ShapeDtypeStruct((B * S, W), F32),
        scratch_shapes=[pltpu.VMEM((S, W), F32), pltpu.VMEM((S, W), F32)],
        compiler_params=_params("parallel"),
        name="rglru",
    )(lx, lgt, conv_w, conv_b.reshape(1, W), gw, gb, lam)


def _attn_prep_body(q_ref, k_ref, v_ref, cos_ref, sin_ref, qnw_ref, knw_ref, rep_ref, rept_ref,
                    qo_ref, kt_ref, vo_ref):
    def head_norm_rot(x, w, cs, sn):
        W = x.shape[-1]
        lane = lax.broadcasted_iota(jnp.int32, (1, W), 1)
        same_head = (lax.broadcasted_iota(jnp.int32, (W, W), 0) // HEAD_DIM
                     == lax.broadcasted_iota(jnp.int32, (W, W), 1) // HEAD_DIM)
        ms = jnp.dot(x * x, same_head.astype(F32), precision=HIGHEST,
                     preferred_element_type=F32) * (1.0 / HEAD_DIM)
        xn = x * lax.rsqrt(ms + EPS) * w
        quarter = HEAD_DIM // 4
        rot = jnp.where((lane % (2 * quarter)) < quarter, -pltpu.roll(xn, W - quarter, 1),
                        pltpu.roll(xn, quarter, 1))
        return xn * cs + rot * sn

    cs = cos_ref[...]
    sn = sin_ref[...]
    q = head_norm_rot(q_ref[...], qnw_ref[...], cs, sn)
    qo_ref[...] = (q * (HEAD_DIM ** -0.5)).astype(BF16)
    k = head_norm_rot(k_ref[...], knw_ref[...], cs[:, :KV_W], sn[:, :KV_W]).astype(BF16)
    kt_ref[...] = _nt_dot(rept_ref[...], k).astype(BF16)
    vo_ref[...] = jnp.dot(v_ref[...].astype(BF16), rep_ref[...],
                          preferred_element_type=F32).astype(BF16)


def _attn_prep(aq, ak, av, cos, sin, qn_w, kn_w, B, S, tm):
    T = B * S
    ns = S // tm
    GW = ATT_GROUP * HEAD_DIM
    src = jnp.arange(ATT_KV_HEADS * GW)
    src = (src // GW) * HEAD_DIM + src % HEAD_DIM
    rep = (jnp.arange(KV_W)[:, None] == src[None, :]).astype(BF16)
    return pl.pallas_call(
        _attn_prep_body,
        grid=(B, ns),
        in_specs=[pl.BlockSpec((tm, ATT_W), lambda b, s: (b * ns + s, 0)),
                  pl.BlockSpec((tm, KV_W), lambda b, s: (b * ns + s, 0)),
                  pl.BlockSpec((tm, KV_W), lambda b, s: (b * ns + s, 0)),
                  pl.BlockSpec((tm, ATT_W), lambda b, s: (s, 0)),
                  pl.BlockSpec((tm, ATT_W), lambda b, s: (s, 0)),
                  pl.BlockSpec((1, ATT_W), lambda b, s: (0, 0)),
                  pl.BlockSpec((1, KV_W), lambda b, s: (0, 0)),
                  pl.BlockSpec((KV_W, ATT_KV_HEADS * GW), lambda b, s: (0, 0)),
                  pl.BlockSpec((ATT_KV_HEADS * GW, KV_W), lambda b, s: (0, 0))],
        out_specs=[pl.BlockSpec((tm, ATT_W), lambda b, s: (b * ns + s, 0)),
                   pl.BlockSpec((None, ATT_KV_HEADS * GW, tm), lambda b, s: (b, 0, s)),
                   pl.BlockSpec((tm, ATT_KV_HEADS * GW), lambda b, s: (b * ns + s, 0))],
        out_shape=[jax.ShapeDtypeStruct((T, ATT_W), BF16),
                   jax.ShapeDtypeStruct((B, ATT_KV_HEADS * GW, S), BF16),
                   jax.ShapeDtypeStruct((T, ATT_KV_HEADS * GW), BF16)],
        compiler_params=_params("parallel", "parallel"),
        name="attn_prep",
    )(aq, ak, av, cos, sin, jnp.tile(qn_w, ATT_HEADS).reshape(1, ATT_W),
      jnp.tile(kn_w, ATT_KV_HEADS).reshape(1, KV_W), rep, rep.T)


def _attn_body(q_ref, kt_ref, v_ref, o_ref, *, tq, tk, nk):
    GW = ATT_GROUP * HEAD_DIM
    lane = lax.broadcasted_iota(jnp.int32, (1, GW), 1)
    masks = [(lane // HEAD_DIM) == g for g in range(ATT_GROUP)]
    q = q_ref[...]
    zero = jnp.zeros_like(q)
    qs = jnp.concatenate([jnp.where(m, q, zero) for m in masks], axis=0)
    rows = ATT_GROUP * tq
    m_run = jnp.full((rows, 1), NEG_INF, F32)
    l_run = jnp.zeros((rows, 1), F32)
    acc = jnp.zeros((rows, GW), F32)
    for c in range(nk):
        s = jnp.dot(qs, kt_ref[:, c * tk:(c + 1) * tk], preferred_element_type=F32)
        m_new = jnp.maximum(m_run, jnp.max(s, axis=-1, keepdims=True))
        alpha = jnp.exp(m_run - m_new)
        p = jnp.exp(s - m_new)
        l_run = alpha * l_run + jnp.sum(p, axis=-1, keepdims=True)
        acc = alpha * acc + jnp.dot(p.astype(BF16), v_ref[c * tk:(c + 1) * tk, :],
                                    preferred_element_type=F32)
        m_run = m_new
    on = acc * (1.0 / l_run)
    o = jnp.zeros((tq, GW), F32)
    for g in range(ATT_GROUP):
        o = o + jnp.where(masks[g], on[g * tq:(g + 1) * tq], 0.0)
    o_ref[...] = o


def _attention(qp, kt, vp, B, S, tq, tk):
    T = B * S
    nq = S // tq
    GW = ATT_GROUP * HEAD_DIM
    return pl.pallas_call(
        functools.partial(_attn_body, tq=tq, tk=tk, nk=S // tk),
        grid=(B, ATT_KV_HEADS, nq),
        in_specs=[pl.BlockSpec((tq, GW), lambda b, h, i: (b * nq + i, h)),
                  pl.BlockSpec((None, GW, S), lambda b, h, i: (b, h, 0)),
                  pl.BlockSpec((S, GW), lambda b, h, i: (b, h))],
        out_specs=pl.BlockSpec((tq, GW), lambda b, h, i: (b * nq + i, h)),
        out_shape=jax.ShapeDtypeStruct((T, ATT_W), F32),
        compiler_params=_params("parallel", "parallel", "parallel"),
        name="attention",
    )(qp, kt, vp)


def _outproj_body(x_ref, oret_ref, olru_ref, oatt_ref, w_ref, ln2_ref, wq_ref,
                  xo_ref, xn_ref, q_ref):
    y = jnp.dot(oret_ref[...].astype(BF16), w_ref[0:RET_W, :], preferred_element_type=F32)
    y = y + jnp.dot(olru_ref[...].astype(BF16), w_ref[RET_W:RET_W + LRU_W, :],
                    preferred_element_type=F32)
    y = y + jnp.dot(oatt_ref[...].astype(BF16), w_ref[RET_W + LRU_W:, :],
                    preferred_element_type=F32)
    x = x_ref[...] + y
    xo_ref[...] = x
    xn = _rms(x, ln2_ref[...]).astype(BF16)
    xn_ref[...] = xn
    q_ref[...] = jnp.dot(xn, wq_ref[...], preferred_element_type=F32)


def _outproj(x2, o_ret, o_lru, o_att, w_out_bf, ln2, wq_bf, tm):
    T, D = x2.shape
    QW = wq_bf.shape[1]
    MW = w_out_bf.shape[0]
    tok = lambda w: pl.BlockSpec((tm, w), lambda i: (i, 0))
    return pl.pallas_call(
        _outproj_body,
        grid=(T // tm,),
        in_specs=[tok(D), tok(RET_W), tok(LRU_W), tok(ATT_W),
                  pl.BlockSpec((MW, D), lambda i: (0, 0)),
                  pl.BlockSpec((1, D), lambda i: (0, 0)),
                  pl.BlockSpec((D, QW), lambda i: (0, 0))],
        out_specs=[tok(D), tok(D), tok(QW)],
        out_shape=[jax.ShapeDtypeStruct((T, D), F32), jax.ShapeDtypeStruct((T, D), BF16),
                   jax.ShapeDtypeStruct((T, QW), F32)],
        compiler_params=_params("parallel"),
        name="outproj",
    )(x2, o_ret, o_lru, o_att, w_out_bf, ln2, wq_bf)


def _pair_counts(n):
    return [n // (i + 1) for i in range(n)]


def _peer_route_body(q_ref, keys_ref, thr_ref, w0_ref, s1_ref, w1_ref, a_s, b_s, f_s,
                     *, half, topk):
    s0 = _nt_dot(keys_ref[0], q_ref[:, :half])
    s1 = _nt_dot(keys_ref[1], q_ref[:, half:])

    def top_sorted(x, n, out_ref):
        for r in range(n):
            m = jnp.max(x, axis=0, keepdims=True)
            out_ref[r:r + 1, :] = m
            x = jnp.where(x == m, NEG_INF, x)

    n = topk + 1
    top_sorted(s0, n, a_s)
    top_sorted(s1, n, b_s)
    a = a_s[0:topk, :]
    b = b_s[0:topk, :]
    counts = _pair_counts(n)
    row8 = lax.broadcasted_iota(jnp.int32, (8, 1), 0)
    parts = [a[0:1, :] + b]
    i = 1
    while counts[i] > 1:
        parts.append(jnp.where(row8 < counts[i], a[i:i + 1, :] + b[0:8, :], NEG_INF))
        i += 1
    parts.append(a[i:, :] + b[0:1, :])
    last = jnp.where(row8 == 0, a[0:1, :] + b_s[topk:n, :],
                     jnp.where(row8 == 1, a_s[topk:n, :] + b[0:1, :], NEG_INF))
    parts.append(last)
    cand = jnp.concatenate(parts, axis=0)
    top_sorted(cand, n, f_s)
    f = f_s[0:topk, :]
    z = jnp.sum(jnp.exp(f - f[0:1, :]), axis=0, keepdims=True)
    tau = 0.5 * (f[topk - 1:topk, :] + f_s[topk:n, :])
    thr_ref[...] = tau - s0
    w0_ref[...] = jnp.exp(s0 - a[0:1, :]) * (1.0 / z)
    s1_ref[...] = s1
    w1_ref[...] = jnp.exp(s1 - b[0:1, :])


def _peer_route(q, keys, tm):
    T = q.shape[0]
    H, _, NK, half = keys.shape
    assert _pair_counts(PEER_TOPK + 1)[7] > 1 >= _pair_counts(PEER_TOPK + 1)[8] and NK > PEER_TOPK
    out = jax.ShapeDtypeStruct((H, NK, T), F32)
    ospec = pl.BlockSpec((None, NK, tm), lambda i, h: (h, 0, i))
    return pl.pallas_call(
        functools.partial(_peer_route_body, half=half, topk=PEER_TOPK),
        grid=(T // tm, H),
        in_specs=[pl.BlockSpec((tm, 2 * half), lambda i, h: (i, h)),
                  pl.BlockSpec((None, 2, NK, half), lambda i, h: (h, 0, 0, 0))],
        out_specs=[ospec, ospec, ospec, ospec],
        out_shape=[out, out, out, out],
        scratch_shapes=[pltpu.VMEM((PEER_TOPK + 8, tm), F32)] * 3,
        compiler_params=_params("parallel", "parallel"),
        name="peer_route",
    )(q, keys)


def _peer_mix_body(xn_ref, xres_ref, u_ref, vt_ref, thr_ref, w0_ref, s1_ref, w1_ref, o_ref,
                   h_s, gh_s, acc_s, *, ti, tl):
    H, NK, tm = thr_ref.shape
    j = pl.program_id(1)

    @pl.when(j == 0)
    def _():
        acc_s[...] = jnp.zeros_like(acc_s)

    h_s[...] = _nt_dot(u_ref[...], xn_ref[...])

    def per_key(ii, carry):
        i = j * ti + ii
        r0 = pl.multiple_of(ii * NK, NK)
        for lc in range(tm // tl):
            ls = slice(lc * tl, (lc + 1) * tl)
            g = jnp.zeros((NK, tl), F32)
            for h in range(H):
                t = thr_ref[h, pl.ds(i, 1), ls]
                a = w0_ref[h, pl.ds(i, 1), ls]
                g = g + a * jnp.where(s1_ref[h, :, ls] >= t, w1_ref[h, :, ls], 0.0)
            gh_s[pl.ds(r0, NK), ls] = (g * _gelu_tanh(h_s[pl.ds(r0, NK), ls])).astype(BF16)
        return carry

    lax.fori_loop(0, ti, per_key, 0)
    acc_s[...] += jnp.dot(vt_ref[...], gh_s[...], preferred_element_type=F32)

    @pl.when(j == pl.num_programs(1) - 1)
    def _():
        o_ref[...] = xres_ref[...] + acc_s[...].T


def _peer_mix(xn, xres, u_bf, vt_bf, thr, w0, s1, w1, tm, ti, tl):
    T, D = xn.shape
    H, NK, _ = thr.shape
    E = u_bf.shape[0]
    te = ti * NK
    route = pl.BlockSpec((H, NK, tm), lambda i, j: (0, 0, i))
    return pl.pallas_call(
        functools.partial(_peer_mix_body, ti=ti, tl=tl),
        grid=(T // tm, E // te),
        in_specs=[pl.BlockSpec((tm, D), lambda i, j: (i, 0)),
                  pl.BlockSpec((tm, D), lambda i, j: (i, 0)),
                  pl.BlockSpec((te, D), lambda i, j: (j, 0)),
                  pl.BlockSpec((D, te), lambda i, j: (0, j)),
                  route, route, route, route],
        out_specs=pl.BlockSpec((tm, D), lambda i, j: (i, 0)),
        out_shape=jax.ShapeDtypeStruct((T, D), F32),
        scratch_shapes=[pltpu.VMEM((te, tm), F32), pltpu.VMEM((te, tm), BF16),
                        pltpu.VMEM((D, tm), F32)],
        compiler_params=_params("parallel", "arbitrary"),
        name="peer_mix",
    )(xn, xres, u_bf, vt_bf, thr, w0, s1, w1)


def _final_norm_body(x_ref, w_ref, o_ref):
    o_ref[...] = _rms(x_ref[...], w_ref[...])


def _final_norm(x2, w, tm):
    T, D = x2.shape
    return pl.pallas_call(
        _final_norm_body,
        grid=(T // tm,),
        in_specs=[pl.BlockSpec((tm, D), lambda i: (i, 0)), pl.BlockSpec((1, D), lambda i: (0, 0))],
        out_specs=pl.BlockSpec((tm, D), lambda i: (i, 0)),
        out_shape=jax.ShapeDtypeStruct((T, D), F32),
        compiler_params=_params("parallel"),
        name="final_norm",
    )(x2, w)


def _rope_table(pos, inv_freq):
    ang = pos[:, None] * inv_freq[None, :]
    ang = jnp.concatenate([ang, ang], axis=-1)
    return jnp.cos(ang), jnp.sin(ang)


def _position_tables(S):
    t = jnp.arange(S, dtype=F32)
    n_rows = S // GRID_W
    rows = jnp.repeat(jnp.arange(n_rows, dtype=F32), GRID_W)
    cols = jnp.tile(jnp.arange(GRID_W, dtype=F32), n_rows)
    ret_inv = 1.0 / (10000.0 ** jnp.linspace(0.0, 1.0, HEAD_DIM // 2, dtype=F32))
    ret_cos, ret_sin = _rope_table(t, ret_inv)
    ax_n = HEAD_DIM // 4
    ax_inv = ROPE_BASE ** (-jnp.arange(ax_n, dtype=F32) / ax_n)
    cr, sr = _rope_table(rows, ax_inv)
    cc, sc = _rope_table(cols, ax_inv)
    ax_cos = jnp.concatenate([cr, cc], axis=-1)
    ax_sin = jnp.concatenate([sr, sc], axis=-1)
    return (jnp.tile(ret_cos, (1, RET_HEADS)), jnp.tile(ret_sin, (1, RET_HEADS)),
            jnp.tile(ax_cos, (1, ATT_HEADS)), jnp.tile(ax_sin, (1, ATT_HEADS)))


def _tiles(B, S, n_keys):
    T = B * S
    return dict(
        tm=min(512, T),
        tb=min(256, S),
        tp=min(512, S),
        tq=min(128, S),
        tk=min(512, S),
        tr=min(256, T),
        tmix=min(512, T),
        ti=min(8, n_keys),
        tl=min(256, T),
    )


def kernel(x, ln1_w, w_in, ret_log_decay, ret_gn_w, lru_conv_w, lru_conv_b, lru_gate_w, lru_gate_b,
           lru_lambda, attn_q_norm, attn_k_norm, w_out, ln2_w, peer_wq, peer_keys, peer_u, peer_v,
           lnf_w):
    B, S, D = x.shape
    T = B * S
    depth = w_in.shape[0]
    n_keys = peer_keys.shape[3]
    tl = _tiles(B, S, n_keys)
    widths = (RET_W,) * 4 + (LRU_W,) * 2 + (ATT_W, KV_W, KV_W)
    ret_cos, ret_sin, ax_cos, ax_sin = _position_tables(S)
    x2 = x.reshape(T, D)
    for l in range(depth):
        rq, rk, rv, rg, lx, lgt, aq, ak, av = _inproj(
            x2, ln1_w[l].reshape(1, D), w_in[l].astype(BF16), widths, tl["tm"])
        o_ret = _retention(rq, rk, rv, rg, ret_cos, ret_sin, ret_log_decay[l],
                           ret_gn_w[l].reshape(1, RET_W), B, S)
        o_lru = _lru(lx, lgt, lru_conv_w[l], lru_conv_b[l], lru_gate_w[l], lru_gate_b[l],
                     lru_lambda[l], B, S, tl["tb"])
        qp, kt, vp = _attn_prep(aq, ak, av, ax_cos, ax_sin, attn_q_norm[l], attn_k_norm[l],
                                B, S, tl["tp"])
        o_att = _attention(qp, kt, vp, B, S, tl["tq"], tl["tk"])
        x2, xn, q = _outproj(x2, o_ret, o_lru, o_att, w_out[l].astype(BF16),
                             ln2_w[l].reshape(1, D), peer_wq[l].astype(BF16), tl["tm"])
        thr, w0, s1, w1 = _peer_route(q, peer_keys[l], tl["tr"])
        x2 = _peer_mix(xn, x2, peer_u[l].astype(BF16), peer_v[l].T.astype(BF16),
                       thr, w0, s1, w1, tl["tmix"], tl["ti"], tl["tl"])
    return _final_norm(x2, lnf_w.reshape(1, D), tl["tm"]).reshape(B, S, D)
```

```python
import functools

import jax
import jax.numpy as jnp
from jax import lax
from jax.experimental import pallas as pl
from jax.experimental.pallas import tpu as pltpu

F32 = jnp.float32
BF16 = jnp.bfloat16
HIGHEST = lax.Precision.HIGHEST
EPS = 1e-6

GRID_W = 64
CHUNK = 128
HEAD_DIM = 64
RET_HEADS = 4
RET_W = RET_HEADS * HEAD_DIM
LRU_W = 256
LRU_CONV = 4
LRU_C = 8.0
ATT_HEADS = 8
ATT_KV_HEADS = 2
ATT_GROUP = ATT_HEADS // ATT_KV_HEADS
ATT_W = ATT_HEADS * HEAD_DIM
KV_W = ATT_KV_HEADS * HEAD_DIM
ROPE_BASE = 10000.0
PEER_TOPK = 16
RANK_UNSELECTED = 64.0
BF16_SUBLANES = 16

VMEM_LIMIT_BYTES = 56 * 1024 * 1024
NEG_INF = float("-inf")


def _params(*sem):
    return pltpu.CompilerParams(dimension_semantics=sem, vmem_limit_bytes=VMEM_LIMIT_BYTES)


def _rms(x, w):
    return x * lax.rsqrt(jnp.mean(x * x, axis=-1, keepdims=True) + EPS) * w


def _gelu_tanh(x):
    return 0.5 * x * (1.0 + jnp.tanh(0.7978845608028654 * (x + 0.044715 * (x * x * x))))


def _sigmoid(x):
    return 1.0 / (1.0 + jnp.exp(-x))


def _nt_dot(a, b, precision=None):
    return lax.dot_general(a, b, (((1,), (1,)), ((), ())), precision=precision,
                           preferred_element_type=F32)


def _tn_dot(a, b, precision=None):
    return lax.dot_general(a, b, (((0,), (0,)), ((), ())), precision=precision,
                           preferred_element_type=F32)


def _inproj_body(x_ref, lnw_ref, w_ref, *out_refs, widths):
    hb = _rms(x_ref[...], lnw_ref[...]).astype(BF16)
    off = 0
    for o_ref, wd in zip(out_refs, widths):
        o_ref[...] = jnp.dot(hb, w_ref[:, off:off + wd],
                             preferred_element_type=F32).astype(o_ref.dtype)
        off += wd


def _inproj(x2, lnw, w_bf, widths, tm):
    T, D = x2.shape
    n_in = w_bf.shape[1]
    return pl.pallas_call(
        functools.partial(_inproj_body, widths=widths),
        grid=(T // tm,),
        in_specs=[pl.BlockSpec((tm, D), lambda i: (i, 0)),
                  pl.BlockSpec((1, D), lambda i: (0, 0)),
                  pl.BlockSpec((D, n_in), lambda i: (0, 0))],
        out_specs=[pl.BlockSpec((tm, wd), lambda i: (i, 0)) for wd in widths],
        out_shape=[jax.ShapeDtypeStruct((T, wd), F32) for wd in widths],
        compiler_params=_params("parallel"),
        name="inproj",
    )(x2, lnw, w_bf)


def _ret_body(ld_ref, q_ref, k_ref, v_ref, g_ref, cos_ref, sin_ref, lgl_ref, gnw_ref, o_ref,
              qr_s, kr_s, st_s, d_s, *, n_chunks):
    C, W = CHUNK, RET_W
    lane = lax.broadcasted_iota(jnp.int32, (1, W), 1)
    first_half = (lane % HEAD_DIM) < (HEAD_DIM // 2)
    head_masks = [(lane // HEAD_DIM) == h for h in range(RET_HEADS)]
    same_head = (lax.broadcasted_iota(jnp.int32, (W, W), 0) // HEAD_DIM
                 == lax.broadcasted_iota(jnp.int32, (W, W), 1) // HEAD_DIM)
    bd = same_head.astype(F32)
    lgf = lgl_ref[0:1, :]
    lgb = lgl_ref[1:2, :]
    idx = lax.broadcasted_iota(jnp.int32, (C, 1), 0).astype(F32)
    decf = jnp.exp(lgf * float(C))
    decb = jnp.exp(lgb * float(C))

    diff = (lax.broadcasted_iota(jnp.int32, (C, C), 0)
            - lax.broadcasted_iota(jnp.int32, (C, C), 1)).astype(F32)
    for h in range(RET_HEADS):
        d_s[h] = jnp.where(diff >= 0.0,
                           jnp.exp(ld_ref[0, h] * jnp.maximum(diff, 0.0)),
                           jnp.exp(ld_ref[1, h] * jnp.maximum(-diff, 0.0)))

    def rot_half(x):
        return jnp.where(first_half, -pltpu.roll(x, W - HEAD_DIM // 2, 1),
                         pltpu.roll(x, HEAD_DIM // 2, 1))

    def prep(c, carry):
        r0 = pl.multiple_of(c * C, C)
        cs = cos_ref[pl.ds(r0, C), :]
        sn = sin_ref[pl.ds(r0, C), :]
        q = q_ref[pl.ds(r0, C), :]
        k = k_ref[pl.ds(r0, C), :]
        qr_s[pl.ds(r0, C), :] = q * cs + rot_half(q) * sn
        kr_s[pl.ds(r0, C), :] = (k * cs + rot_half(k) * sn) * (HEAD_DIM ** -0.5)
        return carry

    lax.fori_loop(0, n_chunks, prep, 0)

    st_s[...] = jnp.zeros((W, W), F32)

    def fwd(c, carry):
        r0 = pl.multiple_of(c * C, C)
        q = qr_s[pl.ds(r0, C), :]
        k = kr_s[pl.ds(r0, C), :]
        v = v_ref[pl.ds(r0, C), :]
        st = st_s[...]
        qw = q * jnp.exp(lgf * (idx + 1.0))
        o_ref[pl.ds(r0, C), :] = jnp.dot(qw, st * bd, precision=HIGHEST,
                                         preferred_element_type=F32)
        kw = k * jnp.exp(lgf * (float(C) - 1.0 - idx))
        st_s[...] = st * decf + _tn_dot(kw, v, HIGHEST)
        return carry

    lax.fori_loop(0, n_chunks, fwd, 0)

    st_s[...] = jnp.zeros((W, W), F32)

    def bwd(t, carry):
        c = n_chunks - 1 - t
        r0 = pl.multiple_of(c * C, C)
        q = qr_s[pl.ds(r0, C), :]
        k = kr_s[pl.ds(r0, C), :]
        v = v_ref[pl.ds(r0, C), :]
        st = st_s[...]
        qw = q * jnp.exp(lgb * (float(C) - idx))
        o = o_ref[pl.ds(r0, C), :] + jnp.dot(qw, st * bd, precision=HIGHEST,
                                              preferred_element_type=F32)
        kw = k * jnp.exp(lgb * idx)
        st_s[...] = st * decb + _tn_dot(kw, v, HIGHEST)
        for h in range(RET_HEADS):
            qm = jnp.where(head_masks[h], q, 0.0)
            s = _nt_dot(qm, k, HIGHEST) * d_s[h]
            oh = jnp.dot(s, v, precision=HIGHEST, preferred_element_type=F32)
            o = o + jnp.where(head_masks[h], oh, 0.0)
        ms = jnp.dot(o * o, bd, precision=HIGHEST, preferred_element_type=F32) * (1.0 / HEAD_DIM)
        on = o * lax.rsqrt(ms + EPS) * gnw_ref[...]
        g = g_ref[pl.ds(r0, C), :]
        o_ref[pl.ds(r0, C), :] = (g * _sigmoid(g)) * on
        return carry

    lax.fori_loop(0, n_chunks, bwd, 0)


def _retention(rq, rk, rv, rg, cos, sin, log_decay, gn_w, B, S):
    W = RET_W
    lgl = jnp.repeat(log_decay, HEAD_DIM, axis=1)
    seq = pl.BlockSpec((S, W), lambda b: (b, 0), pipeline_mode=pl.Buffered(1))
    tab = pl.BlockSpec((S, W), lambda b: (0, 0), pipeline_mode=pl.Buffered(1))
    return pl.pallas_call(
        functools.partial(_ret_body, n_chunks=S // CHUNK),
        grid=(B,),
        in_specs=[pl.BlockSpec(memory_space=pltpu.SMEM), seq, seq, seq, seq, tab, tab,
                  pl.BlockSpec((2, W), lambda b: (0, 0)),
                  pl.BlockSpec((1, W), lambda b: (0, 0))],
        out_specs=pl.BlockSpec((S, W), lambda b: (b, 0)),
        out_shape=jax.ShapeDtypeStruct((B * S, W), F32),
        scratch_shapes=[pltpu.VMEM((S, W), F32), pltpu.VMEM((S, W), F32),
                        pltpu.VMEM((W, W), F32), pltpu.VMEM((RET_HEADS, CHUNK, CHUNK), F32)],
        compiler_params=_params("parallel"),
        name="retention",
    )(log_decay, rq, rk, rv, rg, cos, sin, lgl, gn_w)


def _lru_body(x_ref, gt_ref, cw_ref, cb_ref, gw_ref, gb_ref, lam_ref, o_ref, xc_s, h_s,
              *, n_blocks, tb):
    W = LRU_W
    S = n_blocks * tb
    z = -lam_ref[...]
    softplus = jnp.maximum(z, 0.0) + jnp.log(1.0 + jnp.exp(-jnp.abs(z)))
    cl = -LRU_C * softplus
    row = lax.broadcasted_iota(jnp.int32, (tb, 1), 0)
    left = LRU_CONV // 2

    def conv_block(kb):
        r0 = pl.multiple_of(kb * tb, tb)
        prev = x_ref[pl.ds(pl.multiple_of(jnp.maximum(r0 - 8, 0), 8), 8), :]
        nxt = x_ref[pl.ds(pl.multiple_of(jnp.minimum(r0 + tb, S - 8), 8), 8), :]
        ext = jnp.concatenate([prev, x_ref[pl.ds(r0, tb), :], nxt], axis=0)
        t = row + r0
        acc = cb_ref[...] + jnp.zeros((tb, W), F32)
        for j in range(LRU_CONV):
            off = j - left
            if off == 0:
                xs = ext[8:8 + tb]
            else:
                xs = pltpu.roll(ext, (-off) % (tb + 16), 0)[8:8 + tb]
                xs = jnp.where((t + off >= 0) & (t + off < S), xs, 0.0)
            acc = acc + xs * cw_ref[j:j + 1, :]
        return acc

    def gates(xc, d):
        g = jnp.dot(xc, gw_ref[:, d * 2 * W:(d + 1) * 2 * W], precision=HIGHEST,
                    preferred_element_type=F32) + gb_ref[:, d * 2 * W:(d + 1) * 2 * W]
        r = _sigmoid(g[:, :W])
        i = _sigmoid(g[:, W:])
        log_a = cl[d:d + 1, :] * r
        a = jnp.exp(log_a)
        b = jnp.sqrt(-jnp.tanh(log_a) * (a * a + 1.0)) * (i * xc)
        return a, b

    def scan_block(a, b, reverse):
        d = 1
        while d < tb:
            if reverse:
                keep = row < tb - d
                a_sh = jnp.where(keep, pltpu.roll(a, tb - d, 0), 1.0)
                b_sh = jnp.where(keep, pltpu.roll(b, tb - d, 0), 0.0)
            else:
                keep = row >= d
                a_sh = jnp.where(keep, pltpu.roll(a, d, 0), 1.0)
                b_sh = jnp.where(keep, pltpu.roll(b, d, 0), 0.0)
            b = a * b_sh + b
            a = a * a_sh
            d *= 2
        return a, b

    def fwd(kb, carry):
        r0 = pl.multiple_of(kb * tb, tb)
        xc = conv_block(kb)
        xc_s[pl.ds(r0, tb), :] = xc
        a, b = gates(xc, 0)
        a, b = scan_block(a, b, False)
        h = a * carry + b
        h_s[pl.ds(r0, tb), :] = h
        return h[tb - 1:tb, :]

    lax.fori_loop(0, n_blocks, fwd, jnp.zeros((1, W), F32))

    def bwd(t, carry):
        kb = n_blocks - 1 - t
        r0 = pl.multiple_of(kb * tb, tb)
        xc = xc_s[pl.ds(r0, tb), :]
        a, b = gates(xc, 1)
        a, b = scan_block(a, b, True)
        h = a * carry + b
        o_ref[pl.ds(r0, tb), :] = (h_s[pl.ds(r0, tb), :] + h) * _gelu_tanh(gt_ref[pl.ds(r0, tb), :])
        return h[0:1, :]

    lax.fori_loop(0, n_blocks, bwd, jnp.zeros((1, W), F32))


def _lru(lx, lgt, conv_w, conv_b, gate_w, gate_b, lam, B, S, tb):
    W = LRU_W
    nb = W // HEAD_DIM
    eye = jnp.eye(nb, dtype=F32)
    gw = jnp.einsum('dgnkm,nj->nkdgjm', gate_w, eye).reshape(W, 4 * W)
    gb = gate_b.reshape(1, 4 * W)
    seq = pl.BlockSpec((S, W), lambda b: (b, 0))
    const = lambda shape: pl.BlockSpec(shape, lambda b: (0,) * len(shape))
    return pl.pallas_call(
        functools.partial(_lru_body, n_blocks=S // tb, tb=tb),
        grid=(B,),
        in_specs=[seq, seq, const((LRU_CONV, W)), const((1, W)), const((W, 4 * W)),
                  const((1, 4 * W)), const((2, W))],
        out_specs=seq,
        out_shape=jax.ShapeDtypeStruct((B * S, W), F32),
        scratch_shapes=[pltpu.VMEM((S, W), F32), pltpu.VMEM((S, W), F32)],
        compiler_params=_params("parallel"),
        name="rglru",
    )(lx, lgt, conv_w, conv_b.reshape(1, W), gw, gb, lam)


def _attn_prep_body(q_ref, k_ref, v_ref, cos_ref, sin_ref, qnw_ref, knw_ref, rep_ref, rept_ref,
                    qo_ref, kt_ref, vo_ref):
    def head_norm_rot(x, w, cs, sn):
        W = x.shape[-1]
        lane = lax.broadcasted_iota(jnp.int32, (1, W), 1)
        same_head = (lax.broadcasted_iota(jnp.int32, (W, W), 0) // HEAD_DIM
                     == lax.broadcasted_iota(jnp.int32, (W, W), 1) // HEAD_DIM)
        ms = jnp.dot(x * x, same_head.astype(F32), precision=HIGHEST,
                     preferred_element_type=F32) * (1.0 / HEAD_DIM)
        xn = x * lax.rsqrt(ms + EPS) * w
        quarter = HEAD_DIM // 4
        rot = jnp.where((lane % (2 * quarter)) < quarter, -pltpu.roll(xn, W - quarter, 1),
                        pltpu.roll(xn, quarter, 1))
        return xn * cs + rot * sn

    cs = cos_ref[...]
    sn = sin_ref[...]
    q = head_norm_rot(q_ref[...], qnw_ref[...], cs, sn)
    qo_ref[...] = (q * (HEAD_DIM ** -0.5)).astype(BF16)
    k = head_norm_rot(k_ref[...], knw_ref[...], cs[:, :KV_W], sn[:, :KV_W]).astype(BF16)
    kt_ref[...] = _nt_dot(rept_ref[...], k).astype(BF16)
    vo_ref[...] = jnp.dot(v_ref[...].astype(BF16), rep_ref[...],
                          preferred_element_type=F32).astype(BF16)


def _attn_prep(aq, ak, av, cos, sin, qn_w, kn_w, B, S, tm):
    T = B * S
    ns = S // tm
    GW = ATT_GROUP * HEAD_DIM
    src = jnp.arange(ATT_KV_HEADS * GW)
    src = (src // GW) * HEAD_DIM + src % HEAD_DIM
    rep = (jnp.arange(KV_W)[:, None] == src[None, :]).astype(BF16)
    return pl.pallas_call(
        _attn_prep_body,
        grid=(B, ns),
        in_specs=[pl.BlockSpec((tm, ATT_W), lambda b, s: (b * ns + s, 0)),
                  pl.BlockSpec((tm, KV_W), lambda b, s: (b * ns + s, 0)),
                  pl.BlockSpec((tm, KV_W), lambda b, s: (b * ns + s, 0)),
                  pl.BlockSpec((tm, ATT_W), lambda b, s: (s, 0)),
                  pl.BlockSpec((tm, ATT_W), lambda b, s: (s, 0)),
                  pl.BlockSpec((1, ATT_W), lambda b, s: (0, 0)),
                  pl.BlockSpec((1, KV_W), lambda b, s: (0, 0)),
                  pl.BlockSpec((KV_W, ATT_KV_HEADS * GW), lambda b, s: (0, 0)),
                  pl.BlockSpec((ATT_KV_HEADS * GW, KV_W), lambda b, s: (0, 0))],
        out_specs=[pl.BlockSpec((tm, ATT_W), lambda b, s: (b * ns + s, 0)),
                   pl.BlockSpec((None, ATT_KV_HEADS * GW, tm), lambda b, s: (b, 0, s)),
                   pl.BlockSpec((tm, ATT_KV_HEADS * GW), lambda b, s: (b * ns + s, 0))],
        out_shape=[jax.ShapeDtypeStruct((T, ATT_W), BF16),
                   jax.ShapeDtypeStruct((B, ATT_KV_HEADS * GW, S), BF16),
                   jax.ShapeDtypeStruct((T, ATT_KV_HEADS * GW), BF16)],
        compiler_params=_params("parallel", "parallel"),
        name="attn_prep",
    )(aq, ak, av, cos, sin, jnp.tile(qn_w, ATT_HEADS).reshape(1, ATT_W),
      jnp.tile(kn_w, ATT_KV_HEADS).reshape(1, KV_W), rep, rep.T)


def _attn_body(q_ref, kt_ref, v_ref, o_ref, *, tq, tk, nk):
    GW = ATT_GROUP * HEAD_DIM
    lane = lax.broadcasted_iota(jnp.int32, (1, GW), 1)
    masks = [(lane // HEAD_DIM) == g for g in range(ATT_GROUP)]
    q = q_ref[...]
    zero = jnp.zeros_like(q)
    qs = jnp.concatenate([jnp.where(m, q, zero) for m in masks], axis=0)
    rows = ATT_GROUP * tq
    m_run = jnp.full((rows, 1), NEG_INF, F32)
    l_run = jnp.zeros((rows, 1), F32)
    acc = jnp.zeros((rows, GW), F32)
    for c in range(nk):
        s = jnp.dot(qs, kt_ref[:, c * tk:(c + 1) * tk], preferred_element_type=F32)
        m_new = jnp.maximum(m_run, jnp.max(s, axis=-1, keepdims=True))
        alpha = jnp.exp(m_run - m_new)
        p = jnp.exp(s - m_new)
        l_run = alpha * l_run + jnp.sum(p, axis=-1, keepdims=True)
        acc = alpha * acc + jnp.dot(p.astype(BF16), v_ref[c * tk:(c + 1) * tk, :],
                                    preferred_element_type=F32)
        m_run = m_new
    on = acc * (1.0 / l_run)
    o = jnp.zeros((tq, GW), F32)
    for g in range(ATT_GROUP):
        o = o + jnp.where(masks[g], on[g * tq:(g + 1) * tq], 0.0)
    o_ref[...] = o


def _attention(qp, kt, vp, B, S, tq, tk):
    T = B * S
    nq = S // tq
    GW = ATT_GROUP * HEAD_DIM
    return pl.pallas_call(
        functools.partial(_attn_body, tq=tq, tk=tk, nk=S // tk),
        grid=(B, ATT_KV_HEADS, nq),
        in_specs=[pl.BlockSpec((tq, GW), lambda b, h, i: (b * nq + i, h)),
                  pl.BlockSpec((None, GW, S), lambda b, h, i: (b, h, 0)),
                  pl.BlockSpec((S, GW), lambda b, h, i: (b, h))],
        out_specs=pl.BlockSpec((tq, GW), lambda b, h, i: (b * nq + i, h)),
        out_shape=jax.ShapeDtypeStruct((T, ATT_W), F32),
        compiler_params=_params("parallel", "parallel", "parallel"),
        name="attention",
    )(qp, kt, vp)


def _outproj_body(x_ref, oret_ref, olru_ref, oatt_ref, w_ref, ln2_ref, wq_ref,
                  xo_ref, xn_ref, q_ref):
    y = jnp.dot(oret_ref[...].astype(BF16), w_ref[0:RET_W, :], preferred_element_type=F32)
    y = y + jnp.dot(olru_ref[...].astype(BF16), w_ref[RET_W:RET_W + LRU_W, :],
                    preferred_element_type=F32)
    y = y + jnp.dot(oatt_ref[...].astype(BF16), w_ref[RET_W + LRU_W:, :],
                    preferred_element_type=F32)
    x = x_ref[...] + y
    xo_ref[...] = x
    xn = _rms(x, ln2_ref[...]).astype(BF16)
    xn_ref[...] = xn
    q_ref[...] = jnp.dot(xn, wq_ref[...], preferred_element_type=F32)


def _outproj(x2, o_ret, o_lru, o_att, w_out_bf, ln2, wq_bf, tm):
    T, D = x2.shape
    QW = wq_bf.shape[1]
    MW = w_out_bf.shape[0]
    tok = lambda w: pl.BlockSpec((tm, w), lambda i: (i, 0))
    return pl.pallas_call(
        _outproj_body,
        grid=(T // tm,),
        in_specs=[tok(D), tok(RET_W), tok(LRU_W), tok(ATT_W),
                  pl.BlockSpec((MW, D), lambda i: (0, 0)),
                  pl.BlockSpec((1, D), lambda i: (0, 0)),
                  pl.BlockSpec((D, QW), lambda i: (0, 0))],
        out_specs=[tok(D), tok(D), tok(QW)],
        out_shape=[jax.ShapeDtypeStruct((T, D), F32), jax.ShapeDtypeStruct((T, D), BF16),
                   jax.ShapeDtypeStruct((T, QW), F32)],
        compiler_params=_params("parallel"),
        name="outproj",
    )(x2, o_ret, o_lru, o_att, w_out_bf, ln2, wq_bf)


def _pair_counts(n):
    return [n // (i + 1) for i in range(n)]


def _peer_route_body(q_ref, keys_ref, cnt_ref, w0_ref, rk_ref, w1_ref, a_s, b_s, f_s,
                     *, half, topk):
    s0 = _nt_dot(keys_ref[0], q_ref[:, :half])
    s1 = _nt_dot(keys_ref[1], q_ref[:, half:])

    def top_sorted(x, n, out_ref, want_rank=False):
        rank = jnp.full(x.shape, RANK_UNSELECTED, F32)
        for r in range(n):
            m = jnp.max(x, axis=0, keepdims=True)
            out_ref[r:r + 1, :] = m
            hit = x == m
            if want_rank:
                rank = jnp.where(hit, float(r), rank)
            x = jnp.where(hit, NEG_INF, x)
        return rank

    n = topk + 1
    top_sorted(s0, n, a_s)
    rank1 = top_sorted(s1, n, b_s, want_rank=True)
    a = a_s[0:topk, :]
    b = b_s[0:topk, :]
    counts = _pair_counts(n)
    row8 = lax.broadcasted_iota(jnp.int32, (8, 1), 0)
    parts = [a[0:1, :] + b]
    i = 1
    while counts[i] > 1:
        parts.append(jnp.where(row8 < counts[i], a[i:i + 1, :] + b[0:8, :], NEG_INF))
        i += 1
    parts.append(a[i:, :] + b[0:1, :])
    last = jnp.where(row8 == 0, a[0:1, :] + b_s[topk:n, :],
                     jnp.where(row8 == 1, a_s[topk:n, :] + b[0:1, :], NEG_INF))
    parts.append(last)
    cand = jnp.concatenate(parts, axis=0)
    top_sorted(cand, n, f_s)
    f = f_s[0:topk, :]
    z = jnp.sum(jnp.exp(f - f[0:1, :]), axis=0, keepdims=True)
    tau = 0.5 * (f[topk - 1:topk, :] + f_s[topk:n, :])
    thr = tau - s0
    cnt = jnp.zeros_like(thr)
    for r in range(topk):
        cnt = cnt + jnp.where(b[r:r + 1, :] >= thr, 1.0, 0.0)
    cnt_ref[...] = cnt
    w0_ref[...] = jnp.exp(s0 - a[0:1, :]) * (1.0 / z)
    rk_ref[...] = rank1.astype(BF16)
    w1_ref[...] = jnp.exp(s1 - b[0:1, :]).astype(BF16)


def _peer_route(q, keys, tm):
    T = q.shape[0]
    H, _, NK, half = keys.shape
    assert _pair_counts(PEER_TOPK + 1)[7] > 1 >= _pair_counts(PEER_TOPK + 1)[8] and NK > PEER_TOPK
    assert RANK_UNSELECTED > PEER_TOPK
    per_key = jax.ShapeDtypeStruct((H, NK, T), F32)
    per_col = jax.ShapeDtypeStruct((H, NK, T), BF16)
    ospec = pl.BlockSpec((None, NK, tm), lambda i, h: (h, 0, i))
    return pl.pallas_call(
        functools.partial(_peer_route_body, half=half, topk=PEER_TOPK),
        grid=(T // tm, H),
        in_specs=[pl.BlockSpec((tm, 2 * half), lambda i, h: (i, h)),
                  pl.BlockSpec((None, 2, NK, half), lambda i, h: (h, 0, 0, 0))],
        out_specs=[ospec, ospec, ospec, ospec],
        out_shape=[per_key, per_key, per_col, per_col],
        scratch_shapes=[pltpu.VMEM((PEER_TOPK + 8, tm), F32)] * 3,
        compiler_params=_params("parallel", "parallel"),
        name="peer_route",
    )(q, keys)


def _peer_mix_body(xn_ref, xres_ref, u_ref, vt_ref, cnt_ref, w0_ref, rk_ref, w1_ref, o_ref,
                   h_s, gh_s, acc_s, *, tl):
    H, ti, tm = cnt_ref.shape
    NK = rk_ref.shape[1]
    j = pl.program_id(1)

    @pl.when(j == 0)
    def _():
        acc_s[...] = jnp.zeros_like(acc_s)

    h_s[...] = _nt_dot(u_ref[...], xn_ref[...])

    def row_tile(row):
        tile = jnp.broadcast_to(row, (BF16_SUBLANES, row.shape[1])).astype(BF16)
        return pltpu.repeat(tile, NK // BF16_SUBLANES, axis=0)

    for ii in range(ti):
        r0 = ii * NK
        for lc in range(tm // tl):
            ls = slice(lc * tl, (lc + 1) * tl)
            g = None
            for h in range(H):
                c = row_tile(cnt_ref[h, ii:ii + 1, ls])
                a = row_tile(w0_ref[h, ii:ii + 1, ls])
                term = a * jnp.where(rk_ref[h, :, ls] < c, w1_ref[h, :, ls], 0)
                g = term if g is None else g + term
            gh_s[r0:r0 + NK, ls] = g * _gelu_tanh(h_s[r0:r0 + NK, ls]).astype(BF16)
    acc_s[...] += jnp.dot(vt_ref[...], gh_s[...], preferred_element_type=F32)

    @pl.when(j == pl.num_programs(1) - 1)
    def _():
        o_ref[...] = xres_ref[...] + acc_s[...].T


def _peer_mix(xn, xres, u_bf, vt_bf, cnt, w0, rk, w1, tm, ti, tl):
    T, D = xn.shape
    H, NK, _ = cnt.shape
    E = u_bf.shape[0]
    te = ti * NK
    keys = pl.BlockSpec((H, ti, tm), lambda i, j: (0, j, i))
    full = pl.BlockSpec((H, NK, tm), lambda i, j: (0, 0, i))
    return pl.pallas_call(
        functools.partial(_peer_mix_body, tl=tl),
        grid=(T // tm, E // te),
        in_specs=[pl.BlockSpec((tm, D), lambda i, j: (i, 0)),
                  pl.BlockSpec((tm, D), lambda i, j: (i, 0)),
                  pl.BlockSpec((te, D), lambda i, j: (j, 0)),
                  pl.BlockSpec((D, te), lambda i, j: (0, j)),
                  keys, keys, full, full],
        out_specs=pl.BlockSpec((tm, D), lambda i, j: (i, 0)),
        out_shape=jax.ShapeDtypeStruct((T, D), F32),
        scratch_shapes=[pltpu.VMEM((te, tm), F32), pltpu.VMEM((te, tm), BF16),
                        pltpu.VMEM((D, tm), F32)],
        compiler_params=_params("parallel", "arbitrary"),
        name="peer_mix",
    )(xn, xres, u_bf, vt_bf, cnt, w0, rk, w1)


def _final_norm_body(x_ref, w_ref, o_ref):
    o_ref[...] = _rms(x_ref[...], w_ref[...])


def _final_norm(x2, w, tm):
    T, D = x2.shape
    return pl.pallas_call(
        _final_norm_body,
        grid=(T // tm,),
        in_specs=[pl.BlockSpec((tm, D), lambda i: (i, 0)), pl.BlockSpec((1, D), lambda i: (0, 0))],
        out_specs=pl.BlockSpec((tm, D), lambda i: (i, 0)),
        out_shape=jax.ShapeDtypeStruct((T, D), F32),
        compiler_params=_params("parallel"),
        name="final_norm",
    )(x2, w)


def _rope_table(pos, inv_freq):
    ang = pos[:, None] * inv_freq[None, :]
    ang = jnp.concatenate([ang, ang], axis=-1)
    return jnp.cos(ang), jnp.sin(ang)


def _position_tables(S):
    t = jnp.arange(S, dtype=F32)
    n_rows = S // GRID_W
    rows = jnp.repeat(jnp.arange(n_rows, dtype=F32), GRID_W)
    cols = jnp.tile(jnp.arange(GRID_W, dtype=F32), n_rows)
    ret_inv = 1.0 / (10000.0 ** jnp.linspace(0.0, 1.0, HEAD_DIM // 2, dtype=F32))
    ret_cos, ret_sin = _rope_table(t, ret_inv)
    ax_n = HEAD_DIM // 4
    ax_inv = ROPE_BASE ** (-jnp.arange(ax_n, dtype=F32) / ax_n)
    cr, sr = _rope_table(rows, ax_inv)
    cc, sc = _rope_table(cols, ax_inv)
    ax_cos = jnp.concatenate([cr, cc], axis=-1)
    ax_sin = jnp.concatenate([sr, sc], axis=-1)
    return (jnp.tile(ret_cos, (1, RET_HEADS)), jnp.tile(ret_sin, (1, RET_HEADS)),
            jnp.tile(ax_cos, (1, ATT_HEADS)), jnp.tile(ax_sin, (1, ATT_HEADS)))


def _tiles(B, S, n_keys):
    T = B * S
    return dict(
        tm=min(512, T),
        tb=min(256, S),
        tp=min(512, S),
        tq=min(128, S),
        tk=min(512, S),
        tr=min(256, T),
        tmix=min(512, T),
        ti=min(8, n_keys),
        tl=min(256, T),
    )


def kernel(x, ln1_w, w_in, ret_log_decay, ret_gn_w, lru_conv_w, lru_conv_b, lru_gate_w, lru_gate_b,
           lru_lambda, attn_q_norm, attn_k_norm, w_out, ln2_w, peer_wq, peer_keys, peer_u, peer_v,
           lnf_w):
    B, S, D = x.shape
    T = B * S
    depth = w_in.shape[0]
    n_keys = peer_keys.shape[3]
    tl = _tiles(B, S, n_keys)
    widths = (RET_W,) * 4 + (LRU_W,) * 2 + (ATT_W, KV_W, KV_W)
    ret_cos, ret_sin, ax_cos, ax_sin = _position_tables(S)
    x2 = x.reshape(T, D)
    for l in range(depth):
        rq, rk, rv, rg, lx, lgt, aq, ak, av = _inproj(
            x2, ln1_w[l].reshape(1, D), w_in[l].astype(BF16), widths, tl["tm"])
        o_ret = _retention(rq, rk, rv, rg, ret_cos, ret_sin, ret_log_decay[l],
                           ret_gn_w[l].reshape(1, RET_W), B, S)
        o_lru = _lru(lx, lgt, lru_conv_w[l], lru_conv_b[l], lru_gate_w[l], lru_gate_b[l],
                     lru_lambda[l], B, S, tl["tb"])
        qp, kt, vp = _attn_prep(aq, ak, av, ax_cos, ax_sin, attn_q_norm[l], attn_k_norm[l],
                                B, S, tl["tp"])
        o_att = _attention(qp, kt, vp, B, S, tl["tq"], tl["tk"])
        x2, xn, q = _outproj(x2, o_ret, o_lru, o_att, w_out[l].astype(BF16),
                             ln2_w[l].reshape(1, D), peer_wq[l].astype(BF16), tl["tm"])
        cnt, w0, rk, w1 = _peer_route(q, peer_keys[l], tl["tr"])
        x2 = _peer_mix(xn, x2, peer_u[l].astype(BF16), peer_v[l].T.astype(BF16),
                       cnt, w0, rk, w1, tl["tmix"], tl["ti"], tl["tl"])
    return _final_norm(x2, lnf_w.reshape(1, D), tl["tm"]).reshape(B, S, D)
```

```python
import functools

import jax
import jax.numpy as jnp
from jax import lax
from jax.experimental import pallas as pl
from jax.experimental.pallas import tpu as pltpu

F32 = jnp.float32
BF16 = jnp.bfloat16
HIGHEST = lax.Precision.HIGHEST
EPS = 1e-6
LOG2_E = 1.4426950408889634

GRID_W = 64
CHUNK = 128
HEAD_DIM = 64
RET_HEADS = 4
RET_W = RET_HEADS * HEAD_DIM
LRU_W = 256
LRU_CONV = 4
LRU_C = 8.0
ATT_HEADS = 8
ATT_KV_HEADS = 2
ATT_GROUP = ATT_HEADS // ATT_KV_HEADS
ATT_W = ATT_HEADS * HEAD_DIM
KV_W = ATT_KV_HEADS * HEAD_DIM
ROPE_BASE = 10000.0
PEER_TOPK = 16
RANK_UNSELECTED = 64.0
BF16_SUBLANES = 16

VMEM_LIMIT_BYTES = 56 * 1024 * 1024
NEG_INF = float("-inf")


def _params(*sem):
    return pltpu.CompilerParams(dimension_semantics=sem, vmem_limit_bytes=VMEM_LIMIT_BYTES)


def _rms(x, w):
    return x * lax.rsqrt(jnp.mean(x * x, axis=-1, keepdims=True) + EPS) * w


def _gelu_tanh(x):
    return 0.5 * x * (1.0 + jnp.tanh(0.7978845608028654 * (x + 0.044715 * (x * x * x))))


def _gelu_tanh_x2(x):
    return x + x * jnp.tanh(x * (0.7978845608028654 + (0.7978845608028654 * 0.044715) * (x * x)))


def _sigmoid(x):
    return 1.0 / (1.0 + jnp.exp(-x))


def _nt_dot(a, b, precision=None):
    return lax.dot_general(a, b, (((1,), (1,)), ((), ())), precision=precision,
                           preferred_element_type=F32)


def _tn_dot(a, b, precision=None):
    return lax.dot_general(a, b, (((0,), (0,)), ((), ())), precision=precision,
                           preferred_element_type=F32)


def _inproj_body(x_ref, lnw_ref, w_ref, *out_refs, widths):
    hb = _rms(x_ref[...], lnw_ref[...]).astype(BF16)
    off = 0
    for o_ref, wd in zip(out_refs, widths):
        o_ref[...] = jnp.dot(hb, w_ref[:, off:off + wd],
                             preferred_element_type=F32).astype(o_ref.dtype)
        off += wd


def _inproj(x2, lnw, w_bf, widths, tm):
    T, D = x2.shape
    n_in = w_bf.shape[1]
    return pl.pallas_call(
        functools.partial(_inproj_body, widths=widths),
        grid=(T // tm,),
        in_specs=[pl.BlockSpec((tm, D), lambda i: (i, 0)),
                  pl.BlockSpec((1, D), lambda i: (0, 0)),
                  pl.BlockSpec((D, n_in), lambda i: (0, 0))],
        out_specs=[pl.BlockSpec((tm, wd), lambda i: (i, 0)) for wd in widths],
        out_shape=[jax.ShapeDtypeStruct((T, wd), F32) for wd in widths],
        compiler_params=_params("parallel"),
        name="inproj",
    )(x2, lnw, w_bf)


def _ret_body(ld_ref, q_ref, k_ref, v_ref, g_ref, cos_ref, sin_ref, lgl_ref, gnw_ref, o_ref,
              qr_s, kr_s, st_s, d_s, *, n_chunks):
    C, W = CHUNK, RET_W
    lane = lax.broadcasted_iota(jnp.int32, (1, W), 1)
    first_half = (lane % HEAD_DIM) < (HEAD_DIM // 2)
    head_masks = [(lane // HEAD_DIM) == h for h in range(RET_HEADS)]
    same_head = (lax.broadcasted_iota(jnp.int32, (W, W), 0) // HEAD_DIM
                 == lax.broadcasted_iota(jnp.int32, (W, W), 1) // HEAD_DIM)
    bd = same_head.astype(F32)
    lgf = lgl_ref[0:1, :]
    lgb = lgl_ref[1:2, :]
    idx = lax.broadcasted_iota(jnp.int32, (C, 1), 0).astype(F32)
    decf = jnp.exp(lgf * float(C))
    decb = jnp.exp(lgb * float(C))

    diff = (lax.broadcasted_iota(jnp.int32, (C, C), 0)
            - lax.broadcasted_iota(jnp.int32, (C, C), 1)).astype(F32)
    for h in range(RET_HEADS):
        d_s[h] = jnp.where(diff >= 0.0,
                           jnp.exp(ld_ref[0, h] * jnp.maximum(diff, 0.0)),
                           jnp.exp(ld_ref[1, h] * jnp.maximum(-diff, 0.0)))

    def rot_half(x):
        return jnp.where(first_half, -pltpu.roll(x, W - HEAD_DIM // 2, 1),
                         pltpu.roll(x, HEAD_DIM // 2, 1))

    def prep(c, carry):
        r0 = pl.multiple_of(c * C, C)
        cs = cos_ref[pl.ds(r0, C), :]
        sn = sin_ref[pl.ds(r0, C), :]
        q = q_ref[pl.ds(r0, C), :]
        k = k_ref[pl.ds(r0, C), :]
        qr_s[pl.ds(r0, C), :] = q * cs + rot_half(q) * sn
        kr_s[pl.ds(r0, C), :] = (k * cs + rot_half(k) * sn) * (HEAD_DIM ** -0.5)
        return carry

    lax.fori_loop(0, n_chunks, prep, 0)

    st_s[...] = jnp.zeros((W, W), F32)

    def fwd(c, carry):
        r0 = pl.multiple_of(c * C, C)
        q = qr_s[pl.ds(r0, C), :]
        k = kr_s[pl.ds(r0, C), :]
        v = v_ref[pl.ds(r0, C), :]
        st = st_s[...]
        qw = q * jnp.exp(lgf * (idx + 1.0))
        o_ref[pl.ds(r0, C), :] = jnp.dot(qw.astype(BF16), (st * bd).astype(BF16),
                                         preferred_element_type=F32)
        kw = k * jnp.exp(lgf * (float(C) - 1.0 - idx))
        st_s[...] = st * decf + _tn_dot(kw.astype(BF16), v.astype(BF16))
        return carry

    lax.fori_loop(0, n_chunks, fwd, 0)

    st_s[...] = jnp.zeros((W, W), F32)

    def bwd(t, carry):
        c = n_chunks - 1 - t
        r0 = pl.multiple_of(c * C, C)
        q = qr_s[pl.ds(r0, C), :]
        k = kr_s[pl.ds(r0, C), :]
        v = v_ref[pl.ds(r0, C), :]
        st = st_s[...]
        qw = q * jnp.exp(lgb * (float(C) - idx))
        vb = v.astype(BF16)
        kb = k.astype(BF16)
        o = o_ref[pl.ds(r0, C), :] + jnp.dot(qw.astype(BF16), (st * bd).astype(BF16),
                                              preferred_element_type=F32)
        kw = k * jnp.exp(lgb * idx)
        st_s[...] = st * decb + _tn_dot(kw.astype(BF16), vb)
        for h in range(RET_HEADS):
            qm = jnp.where(head_masks[h], q, 0.0).astype(BF16)
            s = _nt_dot(qm, kb) * d_s[h]
            oh = jnp.dot(s.astype(BF16), vb, preferred_element_type=F32)
            o = o + jnp.where(head_masks[h], oh, 0.0)
        ms = jnp.dot(o * o, bd, precision=HIGHEST, preferred_element_type=F32) * (1.0 / HEAD_DIM)
        on = o * lax.rsqrt(ms + EPS) * gnw_ref[...]
        g = g_ref[pl.ds(r0, C), :]
        o_ref[pl.ds(r0, C), :] = (g * _sigmoid(g)) * on
        return carry

    lax.fori_loop(0, n_chunks, bwd, 0)


def _retention(rq, rk, rv, rg, cos, sin, log_decay, gn_w, B, S):
    W = RET_W
    lgl = jnp.repeat(log_decay, HEAD_DIM, axis=1)
    seq = pl.BlockSpec((S, W), lambda b: (b, 0), pipeline_mode=pl.Buffered(1))
    tab = pl.BlockSpec((S, W), lambda b: (0, 0), pipeline_mode=pl.Buffered(1))
    return pl.pallas_call(
        functools.partial(_ret_body, n_chunks=S // CHUNK),
        grid=(B,),
        in_specs=[pl.BlockSpec(memory_space=pltpu.SMEM), seq, seq, seq, seq, tab, tab,
                  pl.BlockSpec((2, W), lambda b: (0, 0)),
                  pl.BlockSpec((1, W), lambda b: (0, 0))],
        out_specs=pl.BlockSpec((S, W), lambda b: (b, 0)),
        out_shape=jax.ShapeDtypeStruct((B * S, W), F32),
        scratch_shapes=[pltpu.VMEM((S, W), F32), pltpu.VMEM((S, W), F32),
                        pltpu.VMEM((W, W), F32), pltpu.VMEM((RET_HEADS, CHUNK, CHUNK), F32)],
        compiler_params=_params("parallel"),
        name="retention",
    )(log_decay, rq, rk, rv, rg, cos, sin, lgl, gn_w)


def _lru_body(x_ref, gt_ref, cw_ref, cb_ref, gw_ref, gb_ref, lam_ref, o_ref, xc_s, h_s,
              *, n_blocks, tb):
    W = LRU_W
    S = n_blocks * tb
    z = -lam_ref[...]
    softplus = jnp.maximum(z, 0.0) + jnp.log(1.0 + jnp.exp(-jnp.abs(z)))
    cl = -LRU_C * softplus
    row = lax.broadcasted_iota(jnp.int32, (tb, 1), 0)
    left = LRU_CONV // 2

    def conv_block(kb):
        r0 = pl.multiple_of(kb * tb, tb)
        prev = x_ref[pl.ds(pl.multiple_of(jnp.maximum(r0 - 8, 0), 8), 8), :]
        nxt = x_ref[pl.ds(pl.multiple_of(jnp.minimum(r0 + tb, S - 8), 8), 8), :]
        ext = jnp.concatenate([prev, x_ref[pl.ds(r0, tb), :], nxt], axis=0)
        t = row + r0
        acc = cb_ref[...] + jnp.zeros((tb, W), F32)
        for j in range(LRU_CONV):
            off = j - left
            if off == 0:
                xs = ext[8:8 + tb]
            else:
                xs = pltpu.roll(ext, (-off) % (tb + 16), 0)[8:8 + tb]
                xs = jnp.where((t + off >= 0) & (t + off < S), xs, 0.0)
            acc = acc + xs * cw_ref[j:j + 1, :]
        return acc

    def gates(xc, d):
        g = jnp.dot(xc.astype(BF16), gw_ref[:, d * 2 * W:(d + 1) * 2 * W],
                    preferred_element_type=F32) + gb_ref[:, d * 2 * W:(d + 1) * 2 * W]
        r = _sigmoid(g[:, :W])
        i = _sigmoid(g[:, W:])
        log_a = cl[d:d + 1, :] * r
        a = jnp.exp(log_a)
        b = jnp.sqrt(-jnp.tanh(log_a) * (a * a + 1.0)) * (i * xc)
        return a, b

    def scan_block(a, b, reverse):
        d = 1
        while d < tb:
            if reverse:
                keep = row < tb - d
                a_sh = jnp.where(keep, pltpu.roll(a, tb - d, 0), 1.0)
                b_sh = jnp.where(keep, pltpu.roll(b, tb - d, 0), 0.0)
            else:
                keep = row >= d
                a_sh = jnp.where(keep, pltpu.roll(a, d, 0), 1.0)
                b_sh = jnp.where(keep, pltpu.roll(b, d, 0), 0.0)
            b = a * b_sh + b
            a = a * a_sh
            d *= 2
        return a, b

    def fwd(kb, carry):
        r0 = pl.multiple_of(kb * tb, tb)
        xc = conv_block(kb)
        xc_s[pl.ds(r0, tb), :] = xc
        a, b = gates(xc, 0)
        a, b = scan_block(a, b, False)
        h = a * carry + b
        h_s[pl.ds(r0, tb), :] = h
        return h[tb - 1:tb, :]

    lax.fori_loop(0, n_blocks, fwd, jnp.zeros((1, W), F32))

    def bwd(t, carry):
        kb = n_blocks - 1 - t
        r0 = pl.multiple_of(kb * tb, tb)
        xc = xc_s[pl.ds(r0, tb), :]
        a, b = gates(xc, 1)
        a, b = scan_block(a, b, True)
        h = a * carry + b
        o_ref[pl.ds(r0, tb), :] = (h_s[pl.ds(r0, tb), :] + h) * _gelu_tanh(gt_ref[pl.ds(r0, tb), :])
        return h[0:1, :]

    lax.fori_loop(0, n_blocks, bwd, jnp.zeros((1, W), F32))


def _lru(lx, lgt, conv_w, conv_b, gate_w, gate_b, lam, B, S, tb):
    W = LRU_W
    nb = W // HEAD_DIM
    eye = jnp.eye(nb, dtype=F32)
    gw = jnp.einsum('dgnkm,nj->nkdgjm', gate_w, eye).reshape(W, 4 * W).astype(BF16)
    gb = gate_b.reshape(1, 4 * W)
    seq = pl.BlockSpec((S, W), lambda b: (b, 0))
    const = lambda shape: pl.BlockSpec(shape, lambda b: (0,) * len(shape))
    return pl.pallas_call(
        functools.partial(_lru_body, n_blocks=S // tb, tb=tb),
        grid=(B,),
        in_specs=[seq, seq, const((LRU_CONV, W)), const((1, W)), const((W, 4 * W)),
                  const((1, 4 * W)), const((2, W))],
        out_specs=seq,
        out_shape=jax.ShapeDtypeStruct((B * S, W), F32),
        scratch_shapes=[pltpu.VMEM((S, W), F32), pltpu.VMEM((S, W), F32)],
        compiler_params=_params("parallel"),
        name="rglru",
    )(lx, lgt, conv_w, conv_b.reshape(1, W), gw, gb, lam)


def _attn_prep_body(q_ref, k_ref, v_ref, cos_ref, sin_ref, qnw_ref, knw_ref, rep_ref, rept_ref,
                    qo_ref, kt_ref, vo_ref):
    def head_norm_rot(x, w, cs, sn):
        W = x.shape[-1]
        lane = lax.broadcasted_iota(jnp.int32, (1, W), 1)
        same_head = (lax.broadcasted_iota(jnp.int32, (W, W), 0) // HEAD_DIM
                     == lax.broadcasted_iota(jnp.int32, (W, W), 1) // HEAD_DIM)
        ms = jnp.dot(x * x, same_head.astype(F32), precision=HIGHEST,
                     preferred_element_type=F32) * (1.0 / HEAD_DIM)
        xn = x * lax.rsqrt(ms + EPS) * w
        quarter = HEAD_DIM // 4
        rot = jnp.where((lane % (2 * quarter)) < quarter, -pltpu.roll(xn, W - quarter, 1),
                        pltpu.roll(xn, quarter, 1))
        return xn * cs + rot * sn

    cs = cos_ref[...]
    sn = sin_ref[...]
    q = head_norm_rot(q_ref[...], qnw_ref[...], cs, sn)
    qo_ref[...] = (q * (HEAD_DIM ** -0.5 * LOG2_E)).astype(BF16)
    k = head_norm_rot(k_ref[...], knw_ref[...], cs[:, :KV_W], sn[:, :KV_W]).astype(BF16)
    kt_ref[...] = _nt_dot(rept_ref[...], k).astype(BF16)
    vo_ref[...] = jnp.dot(v_ref[...].astype(BF16), rep_ref[...],
                          preferred_element_type=F32).astype(BF16)


def _attn_prep(aq, ak, av, cos, sin, qn_w, kn_w, B, S, tm):
    T = B * S
    ns = S // tm
    GW = ATT_GROUP * HEAD_DIM
    src = jnp.arange(ATT_KV_HEADS * GW)
    src = (src // GW) * HEAD_DIM + src % HEAD_DIM
    rep = (jnp.arange(KV_W)[:, None] == src[None, :]).astype(BF16)
    return pl.pallas_call(
        _attn_prep_body,
        grid=(B, ns),
        in_specs=[pl.BlockSpec((tm, ATT_W), lambda b, s: (b * ns + s, 0)),
                  pl.BlockSpec((tm, KV_W), lambda b, s: (b * ns + s, 0)),
                  pl.BlockSpec((tm, KV_W), lambda b, s: (b * ns + s, 0)),
                  pl.BlockSpec((tm, ATT_W), lambda b, s: (s, 0)),
                  pl.BlockSpec((tm, ATT_W), lambda b, s: (s, 0)),
                  pl.BlockSpec((1, ATT_W), lambda b, s: (0, 0)),
                  pl.BlockSpec((1, KV_W), lambda b, s: (0, 0)),
                  pl.BlockSpec((KV_W, ATT_KV_HEADS * GW), lambda b, s: (0, 0)),
                  pl.BlockSpec((ATT_KV_HEADS * GW, KV_W), lambda b, s: (0, 0))],
        out_specs=[pl.BlockSpec((tm, ATT_W), lambda b, s: (b * ns + s, 0)),
                   pl.BlockSpec((None, ATT_KV_HEADS * GW, tm), lambda b, s: (b, 0, s)),
                   pl.BlockSpec((tm, ATT_KV_HEADS * GW), lambda b, s: (b * ns + s, 0))],
        out_shape=[jax.ShapeDtypeStruct((T, ATT_W), BF16),
                   jax.ShapeDtypeStruct((B, ATT_KV_HEADS * GW, S), BF16),
                   jax.ShapeDtypeStruct((T, ATT_KV_HEADS * GW), BF16)],
        compiler_params=_params("parallel", "parallel"),
        name="attn_prep",
    )(aq, ak, av, cos, sin, jnp.tile(qn_w, ATT_HEADS).reshape(1, ATT_W),
      jnp.tile(kn_w, ATT_KV_HEADS).reshape(1, KV_W), rep, rep.T)


def _attn_body(q_ref, kt_ref, v_ref, o_ref, *, tq, tk, nk):
    GW = ATT_GROUP * HEAD_DIM
    lane = lax.broadcasted_iota(jnp.int32, (1, GW), 1)
    masks = [(lane // HEAD_DIM) == g for g in range(ATT_GROUP)]
    q = q_ref[...]
    zero = jnp.zeros_like(q)
    qs = jnp.concatenate([jnp.where(m, q, zero) for m in masks], axis=0)
    rows = ATT_GROUP * tq
    m_run = jnp.full((rows, 1), NEG_INF, F32)
    l_run = jnp.zeros((rows, 1), F32)
    acc = jnp.zeros((rows, GW), F32)
    for c in range(nk):
        s = jnp.dot(qs, kt_ref[:, c * tk:(c + 1) * tk], preferred_element_type=F32)
        m_new = jnp.maximum(m_run, jnp.max(s, axis=-1, keepdims=True))
        alpha = jnp.exp2(m_run - m_new)
        p = jnp.exp2(s - m_new)
        l_run = alpha * l_run + jnp.sum(p, axis=-1, keepdims=True)
        acc = alpha * acc + jnp.dot(p.astype(BF16), v_ref[c * tk:(c + 1) * tk, :],
                                    preferred_element_type=F32)
        m_run = m_new
    on = acc * (1.0 / l_run)
    o = jnp.zeros((tq, GW), F32)
    for g in range(ATT_GROUP):
        o = o + jnp.where(masks[g], on[g * tq:(g + 1) * tq], 0.0)
    o_ref[...] = o


def _attention(qp, kt, vp, B, S, tq, tk):
    T = B * S
    nq = S // tq
    GW = ATT_GROUP * HEAD_DIM
    return pl.pallas_call(
        functools.partial(_attn_body, tq=tq, tk=tk, nk=S // tk),
        grid=(B, ATT_KV_HEADS, nq),
        in_specs=[pl.BlockSpec((tq, GW), lambda b, h, i: (b * nq + i, h)),
                  pl.BlockSpec((None, GW, S), lambda b, h, i: (b, h, 0)),
                  pl.BlockSpec((S, GW), lambda b, h, i: (b, h))],
        out_specs=pl.BlockSpec((tq, GW), lambda b, h, i: (b * nq + i, h)),
        out_shape=jax.ShapeDtypeStruct((T, ATT_W), F32),
        compiler_params=_params("parallel", "parallel", "parallel"),
        name="attention",
    )(qp, kt, vp)


def _outproj_body(x_ref, oret_ref, olru_ref, oatt_ref, w_ref, ln2_ref, wq_ref,
                  xo_ref, xn_ref, q_ref):
    y = jnp.dot(oret_ref[...].astype(BF16), w_ref[0:RET_W, :], preferred_element_type=F32)
    y = y + jnp.dot(olru_ref[...].astype(BF16), w_ref[RET_W:RET_W + LRU_W, :],
                    preferred_element_type=F32)
    y = y + jnp.dot(oatt_ref[...].astype(BF16), w_ref[RET_W + LRU_W:, :],
                    preferred_element_type=F32)
    x = x_ref[...] + y
    xo_ref[...] = x
    xn = _rms(x, ln2_ref[...]).astype(BF16)
    xn_ref[...] = xn
    q_ref[...] = jnp.dot(xn, wq_ref[...], preferred_element_type=F32)


def _outproj(x2, o_ret, o_lru, o_att, w_out_bf, ln2, wq_bf, tm):
    T, D = x2.shape
    QW = wq_bf.shape[1]
    MW = w_out_bf.shape[0]
    tok = lambda w: pl.BlockSpec((tm, w), lambda i: (i, 0))
    return pl.pallas_call(
        _outproj_body,
        grid=(T // tm,),
        in_specs=[tok(D), tok(RET_W), tok(LRU_W), tok(ATT_W),
                  pl.BlockSpec((MW, D), lambda i: (0, 0)),
                  pl.BlockSpec((1, D), lambda i: (0, 0)),
                  pl.BlockSpec((D, QW), lambda i: (0, 0))],
        out_specs=[tok(D), tok(D), tok(QW)],
        out_shape=[jax.ShapeDtypeStruct((T, D), F32), jax.ShapeDtypeStruct((T, D), BF16),
                   jax.ShapeDtypeStruct((T, QW), F32)],
        compiler_params=_params("parallel"),
        name="outproj",
    )(x2, o_ret, o_lru, o_att, w_out_bf, ln2, wq_bf)


def _pair_counts(n):
    return [n // (i + 1) for i in range(n)]


def _peer_route_body(q_ref, keys_ref, cnt_ref, w0_ref, rk_ref, w1_ref, a_s, b_s, f_s,
                     *, half, topk):
    s0 = _nt_dot(keys_ref[0], q_ref[:, :half])
    s1 = _nt_dot(keys_ref[1], q_ref[:, half:])

    def top_sorted(x, n, out_ref, want_rank=False):
        rank = jnp.full(x.shape, RANK_UNSELECTED, F32)
        for r in range(n):
            m = jnp.max(x, axis=0, keepdims=True)
            out_ref[r:r + 1, :] = m
            hit = x == m
            if want_rank:
                rank = jnp.where(hit, float(r), rank)
            x = jnp.where(hit, NEG_INF, x)
        return rank

    n = topk + 1
    top_sorted(s0, n, a_s)
    rank1 = top_sorted(s1, n, b_s, want_rank=True)
    a = a_s[0:topk, :]
    b = b_s[0:topk, :]
    counts = _pair_counts(n)
    row8 = lax.broadcasted_iota(jnp.int32, (8, 1), 0)
    parts = [a[0:1, :] + b]
    i = 1
    while counts[i] > 1:
        parts.append(jnp.where(row8 < counts[i], a[i:i + 1, :] + b[0:8, :], NEG_INF))
        i += 1
    parts.append(a[i:, :] + b[0:1, :])
    last = jnp.where(row8 == 0, a[0:1, :] + b_s[topk:n, :],
                     jnp.where(row8 == 1, a_s[topk:n, :] + b[0:1, :], NEG_INF))
    parts.append(last)
    cand = jnp.concatenate(parts, axis=0)
    top_sorted(cand, n, f_s)
    f = f_s[0:topk, :]
    z = jnp.sum(jnp.exp(f - f[0:1, :]), axis=0, keepdims=True)
    tau = 0.5 * (f[topk - 1:topk, :] + f_s[topk:n, :])
    thr = tau - s0
    cnt = jnp.zeros_like(thr)
    for r in range(topk):
        cnt = cnt + jnp.where(b[r:r + 1, :] >= thr, 1.0, 0.0)
    cnt_ref[...] = cnt
    w0_ref[...] = jnp.exp(s0 - a[0:1, :]) * (0.5 / z)
    rk_ref[...] = rank1.astype(BF16)
    w1_ref[...] = jnp.exp(s1 - b[0:1, :]).astype(BF16)


def _peer_route(q, keys, tm):
    T = q.shape[0]
    H, _, NK, half = keys.shape
    assert _pair_counts(PEER_TOPK + 1)[7] > 1 >= _pair_counts(PEER_TOPK + 1)[8] and NK > PEER_TOPK
    assert RANK_UNSELECTED > PEER_TOPK
    per_key = jax.ShapeDtypeStruct((H, NK, T), F32)
    per_col = jax.ShapeDtypeStruct((H, NK, T), BF16)
    ospec = pl.BlockSpec((None, NK, tm), lambda i, h: (h, 0, i))
    return pl.pallas_call(
        functools.partial(_peer_route_body, half=half, topk=PEER_TOPK),
        grid=(T // tm, H),
        in_specs=[pl.BlockSpec((tm, 2 * half), lambda i, h: (i, h)),
                  pl.BlockSpec((None, 2, NK, half), lambda i, h: (h, 0, 0, 0))],
        out_specs=[ospec, ospec, ospec, ospec],
        out_shape=[per_key, per_key, per_col, per_col],
        scratch_shapes=[pltpu.VMEM((PEER_TOPK + 8, tm), F32)] * 3,
        compiler_params=_params("parallel", "parallel"),
        name="peer_route",
    )(q, keys)


def _peer_mix_body(xn_ref, xres_ref, u_ref, vt_ref, cnt_ref, w0_ref, rk_ref, w1_ref, o_ref,
                   h_s, gh_s, acc_s, *, tl):
    H, ti, tm = cnt_ref.shape
    NK = rk_ref.shape[1]
    j = pl.program_id(1)

    @pl.when(j == 0)
    def _():
        acc_s[...] = jnp.zeros_like(acc_s)
        gh_s[1] = jnp.zeros(gh_s.shape[1:], BF16)

    cur = j % 2
    acc_s[...] += jnp.dot(vt_ref[...], gh_s[1 - cur], preferred_element_type=F32)

    def row_tile(row):
        tile = jnp.broadcast_to(row, (BF16_SUBLANES, row.shape[1])).astype(BF16)
        return pltpu.repeat(tile, NK // BF16_SUBLANES, axis=0)

    for ii in range(ti):
        rs = slice(ii * NK, (ii + 1) * NK)
        for lc in range(tm // tl):
            ls = slice(lc * tl, (lc + 1) * tl)
            g = None
            for h in range(H):
                c = row_tile(cnt_ref[h, ii:ii + 1, ls])
                a = row_tile(w0_ref[h, ii:ii + 1, ls])
                term = a * jnp.where(rk_ref[h, :, ls] < c, w1_ref[h, :, ls], 0)
                g = term if g is None else g + term
            gh_s[cur, rs, ls] = g

    h_s[...] = _nt_dot(u_ref[...], xn_ref[...])
    for ii in range(ti):
        rs = slice(ii * NK, (ii + 1) * NK)
        gh_s[cur, rs, :] = gh_s[cur, rs, :] * _gelu_tanh_x2(h_s[rs, :]).astype(BF16)

    @pl.when(j == pl.num_programs(1) - 1)
    def _():
        o_ref[...] = xres_ref[...] + acc_s[...].T


def _peer_mix(xn, xres, u_bf, vt_bf, cnt, w0, rk, w1, tm, ti, tl):
    T, D = xn.shape
    H, NK, _ = cnt.shape
    E = u_bf.shape[0]
    te = ti * NK
    n_e = E // te
    build = lambda j: jnp.minimum(j, n_e - 1)
    fold = lambda j: jnp.maximum(j - 1, 0)
    keys = pl.BlockSpec((H, ti, tm), lambda i, j: (0, build(j), i))
    full = pl.BlockSpec((H, NK, tm), lambda i, j: (0, 0, i))
    return pl.pallas_call(
        functools.partial(_peer_mix_body, tl=tl),
        grid=(T // tm, n_e + 1),
        in_specs=[pl.BlockSpec((tm, D), lambda i, j: (i, 0)),
                  pl.BlockSpec((tm, D), lambda i, j: (i, 0)),
                  pl.BlockSpec((te, D), lambda i, j: (build(j), 0)),
                  pl.BlockSpec((D, te), lambda i, j: (0, fold(j))),
                  keys, keys, full, full],
        out_specs=pl.BlockSpec((tm, D), lambda i, j: (i, 0)),
        out_shape=jax.ShapeDtypeStruct((T, D), F32),
        scratch_shapes=[pltpu.VMEM((te, tm), F32), pltpu.VMEM((2, te, tm), BF16),
                        pltpu.VMEM((D, tm), F32)],
        compiler_params=_params("parallel", "arbitrary"),
        name="peer_mix",
    )(xn, xres, u_bf, vt_bf, cnt, w0, rk, w1)


def _final_norm_body(x_ref, w_ref, o_ref):
    o_ref[...] = _rms(x_ref[...], w_ref[...])


def _final_norm(x2, w, tm):
    T, D = x2.shape
    return pl.pallas_call(
        _final_norm_body,
        grid=(T // tm,),
        in_specs=[pl.BlockSpec((tm, D), lambda i: (i, 0)), pl.BlockSpec((1, D), lambda i: (0, 0))],
        out_specs=pl.BlockSpec((tm, D), lambda i: (i, 0)),
        out_shape=jax.ShapeDtypeStruct((T, D), F32),
        compiler_params=_params("parallel"),
        name="final_norm",
    )(x2, w)


def _rope_table(pos, inv_freq):
    ang = pos[:, None] * inv_freq[None, :]
    ang = jnp.concatenate([ang, ang], axis=-1)
    return jnp.cos(ang), jnp.sin(ang)


def _position_tables(S):
    t = jnp.arange(S, dtype=F32)
    n_rows = S // GRID_W
    rows = jnp.repeat(jnp.arange(n_rows, dtype=F32), GRID_W)
    cols = jnp.tile(jnp.arange(GRID_W, dtype=F32), n_rows)
    ret_inv = 1.0 / (10000.0 ** jnp.linspace(0.0, 1.0, HEAD_DIM // 2, dtype=F32))
    ret_cos, ret_sin = _rope_table(t, ret_inv)
    ax_n = HEAD_DIM // 4
    ax_inv = ROPE_BASE ** (-jnp.arange(ax_n, dtype=F32) / ax_n)
    cr, sr = _rope_table(rows, ax_inv)
    cc, sc = _rope_table(cols, ax_inv)
    ax_cos = jnp.concatenate([cr, cc], axis=-1)
    ax_sin = jnp.concatenate([sr, sc], axis=-1)
    return (jnp.tile(ret_cos, (1, RET_HEADS)), jnp.tile(ret_sin, (1, RET_HEADS)),
            jnp.tile(ax_cos, (1, ATT_HEADS)), jnp.tile(ax_sin, (1, ATT_HEADS)))


def _tiles(B, S, n_keys):
    T = B * S
    return dict(
        tm=min(512, T),
        tb=min(256, S),
        tp=min(512, S),
        tq=min(128, S),
        tk=min(512, S),
        tr=min(256, T),
        tmix=min(512, T),
        ti=min(8, n_keys),
        tl=min(256, T),
    )


def kernel(x, ln1_w, w_in, ret_log_decay, ret_gn_w, lru_conv_w, lru_conv_b, lru_gate_w, lru_gate_b,
           lru_lambda, attn_q_norm, attn_k_norm, w_out, ln2_w, peer_wq, peer_keys, peer_u, peer_v,
           lnf_w):
    B, S, D = x.shape
    T = B * S
    depth = w_in.shape[0]
    n_keys = peer_keys.shape[3]
    tl = _tiles(B, S, n_keys)
    widths = (RET_W,) * 4 + (LRU_W,) * 2 + (ATT_W, KV_W, KV_W)
    ret_cos, ret_sin, ax_cos, ax_sin = _position_tables(S)
    x2 = x.reshape(T, D)
    for l in range(depth):
        rq, rk, rv, rg, lx, lgt, aq, ak, av = _inproj(
            x2, ln1_w[l].reshape(1, D), w_in[l].astype(BF16), widths, tl["tm"])
        o_ret = _retention(rq, rk, rv, rg, ret_cos, ret_sin, ret_log_decay[l],
                           ret_gn_w[l].reshape(1, RET_W), B, S)
        o_lru = _lru(lx, lgt, lru_conv_w[l], lru_conv_b[l], lru_gate_w[l], lru_gate_b[l],
                     lru_lambda[l], B, S, tl["tb"])
        qp, kt, vp = _attn_prep(aq, ak, av, ax_cos, ax_sin, attn_q_norm[l], attn_k_norm[l],
                                B, S, tl["tp"])
        o_att = _attention(qp, kt, vp, B, S, tl["tq"], tl["tk"])
        x2, xn, q = _outproj(x2, o_ret, o_lru, o_att, w_out[l].astype(BF16),
                             ln2_w[l].reshape(1, D), peer_wq[l].astype(BF16), tl["tm"])
        cnt, w0, rk, w1 = _peer_route(q, peer_keys[l], tl["tr"])
        x2 = _peer_mix(xn, x2, peer_u[l].astype(BF16), peer_v[l].T.astype(BF16),
                       cnt, w0, rk, w1, tl["tmix"], tl["ti"], tl["tl"])
    return _final_norm(x2, lnf_w.reshape(1, D), tl["tm"]).reshape(B, S, D)
```

```python
import functools

import jax
import jax.numpy as jnp
from jax import lax
from jax.experimental import pallas as pl
from jax.experimental.pallas import tpu as pltpu

F32 = jnp.float32
BF16 = jnp.bfloat16
EPS = 1e-6
LOG2_E = 1.4426950408889634

GRID_W = 64
CHUNK = 128
HEAD_DIM = 64
RET_HEADS = 4
RET_W = RET_HEADS * HEAD_DIM
LRU_W = 256
LRU_CONV = 4
LRU_C = 8.0
ATT_HEADS = 8
ATT_KV_HEADS = 2
ATT_GROUP = ATT_HEADS // ATT_KV_HEADS
ATT_W = ATT_HEADS * HEAD_DIM
KV_W = ATT_KV_HEADS * HEAD_DIM
ROPE_BASE = 10000.0
PEER_TOPK = 16
RANK_UNSELECTED = 64.0
BF16_SUBLANES = 16

VMEM_LIMIT_BYTES = 56 * 1024 * 1024
NEG_INF = float("-inf")


def _params(*sem):
    return pltpu.CompilerParams(dimension_semantics=sem, vmem_limit_bytes=VMEM_LIMIT_BYTES)


def _rms(x, w):
    return x * lax.rsqrt(jnp.mean(x * x, axis=-1, keepdims=True) + EPS) * w


def _gelu_tanh(x):
    return 0.5 * x * (1.0 + jnp.tanh(0.7978845608028654 * (x + 0.044715 * (x * x * x))))


def _gelu_tanh_x2(x):
    return x + x * jnp.tanh(x * (0.7978845608028654 + (0.7978845608028654 * 0.044715) * (x * x)))


def _sigmoid(x):
    return 1.0 / (1.0 + jnp.exp(-x))


def _nt_dot(a, b, precision=None):
    return lax.dot_general(a, b, (((1,), (1,)), ((), ())), precision=precision,
                           preferred_element_type=F32)


def _group_sum(x, member):
    hi = x.astype(BF16)
    lo = (x - hi.astype(F32)).astype(BF16)
    return (jnp.dot(hi, member, preferred_element_type=F32)
            + jnp.dot(lo, member, preferred_element_type=F32))


def _tn_dot(a, b, precision=None):
    return lax.dot_general(a, b, (((0,), (0,)), ((), ())), precision=precision,
                           preferred_element_type=F32)


def _inproj_body(x_ref, lnw_ref, w_ref, *out_refs, widths):
    hb = _rms(x_ref[...], lnw_ref[...]).astype(BF16)
    off = 0
    for o_ref, wd in zip(out_refs, widths):
        o_ref[...] = jnp.dot(hb, w_ref[:, off:off + wd],
                             preferred_element_type=F32).astype(o_ref.dtype)
        off += wd


def _inproj(x2, lnw, w_bf, widths, tm):
    T, D = x2.shape
    n_in = w_bf.shape[1]
    return pl.pallas_call(
        functools.partial(_inproj_body, widths=widths),
        grid=(T // tm,),
        in_specs=[pl.BlockSpec((tm, D), lambda i: (i, 0)),
                  pl.BlockSpec((1, D), lambda i: (0, 0)),
                  pl.BlockSpec((D, n_in), lambda i: (0, 0))],
        out_specs=[pl.BlockSpec((tm, wd), lambda i: (i, 0)) for wd in widths],
        out_shape=[jax.ShapeDtypeStruct((T, wd), F32) for wd in widths],
        compiler_params=_params("parallel"),
        name="inproj",
    )(x2, lnw, w_bf)


def _ret_body(ld_ref, q_ref, k_ref, v_ref, g_ref, cos_ref, sin_ref, lgl_ref, gnw_ref, o_ref,
              qr_s, kr_s, st_s, d_s, *, n_chunks):
    C, W = CHUNK, RET_W
    lane = lax.broadcasted_iota(jnp.int32, (1, W), 1)
    first_half = (lane % HEAD_DIM) < (HEAD_DIM // 2)
    head_masks = [(lane // HEAD_DIM) == h for h in range(RET_HEADS)]
    same_head = (lax.broadcasted_iota(jnp.int32, (W, W), 0) // HEAD_DIM
                 == lax.broadcasted_iota(jnp.int32, (W, W), 1) // HEAD_DIM)
    bd = same_head.astype(F32)
    lgf = lgl_ref[0:1, :]
    lgb = lgl_ref[1:2, :]
    idx = lax.broadcasted_iota(jnp.int32, (C, 1), 0).astype(F32)
    decf = jnp.exp(lgf * float(C))
    decb = jnp.exp(lgb * float(C))

    diff = (lax.broadcasted_iota(jnp.int32, (C, C), 0)
            - lax.broadcasted_iota(jnp.int32, (C, C), 1)).astype(F32)
    for h in range(RET_HEADS):
        d_s[h] = jnp.where(diff >= 0.0,
                           jnp.exp(ld_ref[0, h] * jnp.maximum(diff, 0.0)),
                           jnp.exp(ld_ref[1, h] * jnp.maximum(-diff, 0.0)))

    def rot_half(x):
        return jnp.where(first_half, -pltpu.roll(x, W - HEAD_DIM // 2, 1),
                         pltpu.roll(x, HEAD_DIM // 2, 1))

    def prep(c, carry):
        r0 = pl.multiple_of(c * C, C)
        cs = cos_ref[pl.ds(r0, C), :]
        sn = sin_ref[pl.ds(r0, C), :]
        q = q_ref[pl.ds(r0, C), :]
        k = k_ref[pl.ds(r0, C), :]
        qr_s[pl.ds(r0, C), :] = q * cs + rot_half(q) * sn
        kr_s[pl.ds(r0, C), :] = (k * cs + rot_half(k) * sn) * (HEAD_DIM ** -0.5)
        return carry

    lax.fori_loop(0, n_chunks, prep, 0)

    st_s[...] = jnp.zeros((W, W), F32)

    def fwd(c, carry):
        r0 = pl.multiple_of(c * C, C)
        q = qr_s[pl.ds(r0, C), :]
        k = kr_s[pl.ds(r0, C), :]
        v = v_ref[pl.ds(r0, C), :]
        st = st_s[...]
        qw = q * jnp.exp(lgf * (idx + 1.0))
        o_ref[pl.ds(r0, C), :] = jnp.dot(qw.astype(BF16), (st * bd).astype(BF16),
                                         preferred_element_type=F32)
        kw = k * jnp.exp(lgf * (float(C) - 1.0 - idx))
        st_s[...] = st * decf + _tn_dot(kw.astype(BF16), v.astype(BF16))
        return carry

    lax.fori_loop(0, n_chunks, fwd, 0)

    st_s[...] = jnp.zeros((W, W), F32)

    def bwd(t, carry):
        c = n_chunks - 1 - t
        r0 = pl.multiple_of(c * C, C)
        q = qr_s[pl.ds(r0, C), :]
        k = kr_s[pl.ds(r0, C), :]
        v = v_ref[pl.ds(r0, C), :]
        st = st_s[...]
        qw = q * jnp.exp(lgb * (float(C) - idx))
        vb = v.astype(BF16)
        kb = k.astype(BF16)
        o = o_ref[pl.ds(r0, C), :] + jnp.dot(qw.astype(BF16), (st * bd).astype(BF16),
                                              preferred_element_type=F32)
        kw = k * jnp.exp(lgb * idx)
        st_s[...] = st * decb + _tn_dot(kw.astype(BF16), vb)
        for h in range(RET_HEADS):
            qm = jnp.where(head_masks[h], q, 0.0).astype(BF16)
            s = _nt_dot(qm, kb) * d_s[h]
            oh = jnp.dot(s.astype(BF16), vb, preferred_element_type=F32)
            o = o + jnp.where(head_masks[h], oh, 0.0)
        ms = _group_sum(o * o, bd.astype(BF16)) * (1.0 / HEAD_DIM)
        on = o * lax.rsqrt(ms + EPS) * gnw_ref[...]
        g = g_ref[pl.ds(r0, C), :]
        o_ref[pl.ds(r0, C), :] = (g * _sigmoid(g)) * on
        return carry

    lax.fori_loop(0, n_chunks, bwd, 0)


def _retention(rq, rk, rv, rg, cos, sin, log_decay, gn_w, B, S):
    W = RET_W
    lgl = jnp.repeat(log_decay, HEAD_DIM, axis=1)
    seq = pl.BlockSpec((S, W), lambda b: (b, 0), pipeline_mode=pl.Buffered(1))
    tab = pl.BlockSpec((S, W), lambda b: (0, 0), pipeline_mode=pl.Buffered(1))
    return pl.pallas_call(
        functools.partial(_ret_body, n_chunks=S // CHUNK),
        grid=(B,),
        in_specs=[pl.BlockSpec(memory_space=pltpu.SMEM), seq, seq, seq, seq, tab, tab,
                  pl.BlockSpec((2, W), lambda b: (0, 0)),
                  pl.BlockSpec((1, W), lambda b: (0, 0))],
        out_specs=pl.BlockSpec((S, W), lambda b: (b, 0)),
        out_shape=jax.ShapeDtypeStruct((B * S, W), F32),
        scratch_shapes=[pltpu.VMEM((S, W), F32), pltpu.VMEM((S, W), F32),
                        pltpu.VMEM((W, W), F32), pltpu.VMEM((RET_HEADS, CHUNK, CHUNK), F32)],
        compiler_params=_params("parallel"),
        name="retention",
    )(log_decay, rq, rk, rv, rg, cos, sin, lgl, gn_w)


def _lru_body(x_ref, gt_ref, cw_ref, cb_ref, gw_ref, gb_ref, lam_ref, o_ref, xc_s, h_s,
              *, n_blocks, tb):
    W = LRU_W
    S = n_blocks * tb
    z = -lam_ref[...]
    softplus = jnp.maximum(z, 0.0) + jnp.log(1.0 + jnp.exp(-jnp.abs(z)))
    cl = -LRU_C * softplus
    row = lax.broadcasted_iota(jnp.int32, (tb, 1), 0)
    left = LRU_CONV // 2

    def conv_block(kb):
        r0 = pl.multiple_of(kb * tb, tb)
        prev = x_ref[pl.ds(pl.multiple_of(jnp.maximum(r0 - 8, 0), 8), 8), :]
        nxt = x_ref[pl.ds(pl.multiple_of(jnp.minimum(r0 + tb, S - 8), 8), 8), :]
        ext = jnp.concatenate([prev, x_ref[pl.ds(r0, tb), :], nxt], axis=0)
        t = row + r0
        acc = cb_ref[...] + jnp.zeros((tb, W), F32)
        for j in range(LRU_CONV):
            off = j - left
            if off == 0:
                xs = ext[8:8 + tb]
            else:
                xs = pltpu.roll(ext, (-off) % (tb + 16), 0)[8:8 + tb]
                xs = jnp.where((t + off >= 0) & (t + off < S), xs, 0.0)
            acc = acc + xs * cw_ref[j:j + 1, :]
        return acc

    def gates(xc, d):
        g = jnp.dot(xc.astype(BF16), gw_ref[:, d * 2 * W:(d + 1) * 2 * W],
                    preferred_element_type=F32) + gb_ref[:, d * 2 * W:(d + 1) * 2 * W]
        r = _sigmoid(g[:, :W])
        i = _sigmoid(g[:, W:])
        log_a = cl[d:d + 1, :] * r
        a = jnp.exp(log_a)
        b = jnp.sqrt(-jnp.tanh(log_a) * (a * a + 1.0)) * (i * xc)
        return a, b

    def scan_block(a, b, reverse):
        d = 1
        while d < tb:
            if reverse:
                keep = row < tb - d
                a_sh = jnp.where(keep, pltpu.roll(a, tb - d, 0), 1.0)
                b_sh = jnp.where(keep, pltpu.roll(b, tb - d, 0), 0.0)
            else:
                keep = row >= d
                a_sh = jnp.where(keep, pltpu.roll(a, d, 0), 1.0)
                b_sh = jnp.where(keep, pltpu.roll(b, d, 0), 0.0)
            b = a * b_sh + b
            a = a * a_sh
            d *= 2
        return a, b

    def fwd(kb, carry):
        r0 = pl.multiple_of(kb * tb, tb)
        xc = conv_block(kb)
        xc_s[pl.ds(r0, tb), :] = xc
        a, b = gates(xc, 0)
        a, b = scan_block(a, b, False)
        h = a * carry + b
        h_s[pl.ds(r0, tb), :] = h
        return h[tb - 1:tb, :]

    lax.fori_loop(0, n_blocks, fwd, jnp.zeros((1, W), F32))

    def bwd(t, carry):
        kb = n_blocks - 1 - t
        r0 = pl.multiple_of(kb * tb, tb)
        xc = xc_s[pl.ds(r0, tb), :]
        a, b = gates(xc, 1)
        a, b = scan_block(a, b, True)
        h = a * carry + b
        o_ref[pl.ds(r0, tb), :] = (h_s[pl.ds(r0, tb), :] + h) * _gelu_tanh(gt_ref[pl.ds(r0, tb), :])
        return h[0:1, :]

    lax.fori_loop(0, n_blocks, bwd, jnp.zeros((1, W), F32))


def _lru(lx, lgt, conv_w, conv_b, gate_w, gate_b, lam, B, S, tb):
    W = LRU_W
    nb = W // HEAD_DIM
    eye = jnp.eye(nb, dtype=F32)
    gw = jnp.einsum('dgnkm,nj->nkdgjm', gate_w, eye).reshape(W, 4 * W).astype(BF16)
    gb = gate_b.reshape(1, 4 * W)
    seq = pl.BlockSpec((S, W), lambda b: (b, 0))
    const = lambda shape: pl.BlockSpec(shape, lambda b: (0,) * len(shape))
    return pl.pallas_call(
        functools.partial(_lru_body, n_blocks=S // tb, tb=tb),
        grid=(B,),
        in_specs=[seq, seq, const((LRU_CONV, W)), const((1, W)), const((W, 4 * W)),
                  const((1, 4 * W)), const((2, W))],
        out_specs=seq,
        out_shape=jax.ShapeDtypeStruct((B * S, W), F32),
        scratch_shapes=[pltpu.VMEM((S, W), F32), pltpu.VMEM((S, W), F32)],
        compiler_params=_params("parallel"),
        name="rglru",
    )(lx, lgt, conv_w, conv_b.reshape(1, W), gw, gb, lam)


def _attn_prep_body(q_ref, k_ref, v_ref, cos_ref, sin_ref, qnw_ref, knw_ref, rep_ref, rept_ref,
                    qo_ref, kt_ref, vo_ref):
    def head_norm_rot(x, w, cs, sn):
        W = x.shape[-1]
        lane = lax.broadcasted_iota(jnp.int32, (1, W), 1)
        same_head = (lax.broadcasted_iota(jnp.int32, (W, W), 0) // HEAD_DIM
                     == lax.broadcasted_iota(jnp.int32, (W, W), 1) // HEAD_DIM)
        ms = _group_sum(x * x, same_head.astype(BF16)) * (1.0 / HEAD_DIM)
        xn = x * lax.rsqrt(ms + EPS) * w
        quarter = HEAD_DIM // 4
        rot = jnp.where((lane % (2 * quarter)) < quarter, -pltpu.roll(xn, W - quarter, 1),
                        pltpu.roll(xn, quarter, 1))
        return xn * cs + rot * sn

    cs = cos_ref[...]
    sn = sin_ref[...]
    q = head_norm_rot(q_ref[...], qnw_ref[...], cs, sn)
    qo_ref[...] = (q * (HEAD_DIM ** -0.5 * LOG2_E)).astype(BF16)
    k = head_norm_rot(k_ref[...], knw_ref[...], cs[:, :KV_W], sn[:, :KV_W]).astype(BF16)
    kt_ref[...] = _nt_dot(rept_ref[...], k).astype(BF16)
    vo_ref[...] = jnp.dot(v_ref[...].astype(BF16), rep_ref[...],
                          preferred_element_type=F32).astype(BF16)


def _attn_prep(aq, ak, av, cos, sin, qn_w, kn_w, B, S, tm):
    T = B * S
    ns = S // tm
    GW = ATT_GROUP * HEAD_DIM
    src = jnp.arange(ATT_KV_HEADS * GW)
    src = (src // GW) * HEAD_DIM + src % HEAD_DIM
    rep = (jnp.arange(KV_W)[:, None] == src[None, :]).astype(BF16)
    return pl.pallas_call(
        _attn_prep_body,
        grid=(B, ns),
        in_specs=[pl.BlockSpec((tm, ATT_W), lambda b, s: (b * ns + s, 0)),
                  pl.BlockSpec((tm, KV_W), lambda b, s: (b * ns + s, 0)),
                  pl.BlockSpec((tm, KV_W), lambda b, s: (b * ns + s, 0)),
                  pl.BlockSpec((tm, ATT_W), lambda b, s: (s, 0)),
                  pl.BlockSpec((tm, ATT_W), lambda b, s: (s, 0)),
                  pl.BlockSpec((1, ATT_W), lambda b, s: (0, 0)),
                  pl.BlockSpec((1, KV_W), lambda b, s: (0, 0)),
                  pl.BlockSpec((KV_W, ATT_KV_HEADS * GW), lambda b, s: (0, 0)),
                  pl.BlockSpec((ATT_KV_HEADS * GW, KV_W), lambda b, s: (0, 0))],
        out_specs=[pl.BlockSpec((tm, ATT_W), lambda b, s: (b * ns + s, 0)),
                   pl.BlockSpec((None, ATT_KV_HEADS * GW, tm), lambda b, s: (b, 0, s)),
                   pl.BlockSpec((tm, ATT_KV_HEADS * GW), lambda b, s: (b * ns + s, 0))],
        out_shape=[jax.ShapeDtypeStruct((T, ATT_W), BF16),
                   jax.ShapeDtypeStruct((B, ATT_KV_HEADS * GW, S), BF16),
                   jax.ShapeDtypeStruct((T, ATT_KV_HEADS * GW), BF16)],
        compiler_params=_params("parallel", "parallel"),
        name="attn_prep",
    )(aq, ak, av, cos, sin, jnp.tile(qn_w, ATT_HEADS).reshape(1, ATT_W),
      jnp.tile(kn_w, ATT_KV_HEADS).reshape(1, KV_W), rep, rep.T)


def _attn_body(q_ref, kt_ref, v_ref, o_ref, *, tq, tk, nk):
    GW = ATT_GROUP * HEAD_DIM
    lane = lax.broadcasted_iota(jnp.int32, (1, GW), 1)
    masks = [(lane // HEAD_DIM) == g for g in range(ATT_GROUP)]
    q = q_ref[...]
    zero = jnp.zeros_like(q)
    qs = jnp.concatenate([jnp.where(m, q, zero) for m in masks], axis=0)
    rows = ATT_GROUP * tq
    m_run = jnp.full((rows, 1), NEG_INF, F32)
    l_run = jnp.zeros((rows, 1), F32)
    acc = jnp.zeros((rows, GW), F32)
    for c in range(nk):
        s = jnp.dot(qs, kt_ref[:, c * tk:(c + 1) * tk], preferred_element_type=F32)
        m_new = jnp.maximum(m_run, jnp.max(s, axis=-1, keepdims=True))
        alpha = jnp.exp2(m_run - m_new)
        p = jnp.exp2(s - m_new)
        l_run = alpha * l_run + jnp.sum(p, axis=-1, keepdims=True)
        acc = alpha * acc + jnp.dot(p.astype(BF16), v_ref[c * tk:(c + 1) * tk, :],
                                    preferred_element_type=F32)
        m_run = m_new
    on = acc * (1.0 / l_run)
    o = jnp.zeros((tq, GW), F32)
    for g in range(ATT_GROUP):
        o = o + jnp.where(masks[g], on[g * tq:(g + 1) * tq], 0.0)
    o_ref[...] = o


def _attention(qp, kt, vp, B, S, tq, tk):
    T = B * S
    nq = S // tq
    GW = ATT_GROUP * HEAD_DIM
    return pl.pallas_call(
        functools.partial(_attn_body, tq=tq, tk=tk, nk=S // tk),
        grid=(B, ATT_KV_HEADS, nq),
        in_specs=[pl.BlockSpec((tq, GW), lambda b, h, i: (b * nq + i, h)),
                  pl.BlockSpec((None, GW, S), lambda b, h, i: (b, h, 0)),
                  pl.BlockSpec((S, GW), lambda b, h, i: (b, h))],
        out_specs=pl.BlockSpec((tq, GW), lambda b, h, i: (b * nq + i, h)),
        out_shape=jax.ShapeDtypeStruct((T, ATT_W), F32),
        compiler_params=_params("parallel", "parallel", "parallel"),
        name="attention",
    )(qp, kt, vp)


def _outproj_body(x_ref, oret_ref, olru_ref, oatt_ref, w_ref, ln2_ref, wq_ref,
                  xo_ref, xn_ref, q_ref):
    y = jnp.dot(oret_ref[...].astype(BF16), w_ref[0:RET_W, :], preferred_element_type=F32)
    y = y + jnp.dot(olru_ref[...].astype(BF16), w_ref[RET_W:RET_W + LRU_W, :],
                    preferred_element_type=F32)
    y = y + jnp.dot(oatt_ref[...].astype(BF16), w_ref[RET_W + LRU_W:, :],
                    preferred_element_type=F32)
    x = x_ref[...] + y
    xo_ref[...] = x
    xn = _rms(x, ln2_ref[...]).astype(BF16)
    xn_ref[...] = xn
    q_ref[...] = jnp.dot(xn, wq_ref[...], preferred_element_type=F32)


def _outproj(x2, o_ret, o_lru, o_att, w_out_bf, ln2, wq_bf, tm):
    T, D = x2.shape
    QW = wq_bf.shape[1]
    MW = w_out_bf.shape[0]
    tok = lambda w: pl.BlockSpec((tm, w), lambda i: (i, 0))
    return pl.pallas_call(
        _outproj_body,
        grid=(T // tm,),
        in_specs=[tok(D), tok(RET_W), tok(LRU_W), tok(ATT_W),
                  pl.BlockSpec((MW, D), lambda i: (0, 0)),
                  pl.BlockSpec((1, D), lambda i: (0, 0)),
                  pl.BlockSpec((D, QW), lambda i: (0, 0))],
        out_specs=[tok(D), tok(D), tok(QW)],
        out_shape=[jax.ShapeDtypeStruct((T, D), F32), jax.ShapeDtypeStruct((T, D), BF16),
                   jax.ShapeDtypeStruct((T, QW), F32)],
        compiler_params=_params("parallel"),
        name="outproj",
    )(x2, o_ret, o_lru, o_att, w_out_bf, ln2, wq_bf)


def _pair_counts(n):
    return [n // (i + 1) for i in range(n)]


def _peer_route_body(q_ref, keys_ref, cnt_ref, w0_ref, rk_ref, w1_ref, a_s, b_s, f_s,
                     *, half, topk):
    s0 = _nt_dot(keys_ref[0], q_ref[:, :half])
    s1 = _nt_dot(keys_ref[1], q_ref[:, half:])

    def top_sorted(x, n, out_ref, want_rank=False):
        rank = jnp.full(x.shape, RANK_UNSELECTED, F32)
        for r in range(n):
            m = jnp.max(x, axis=0, keepdims=True)
            out_ref[r:r + 1, :] = m
            hit = x == m
            if want_rank:
                rank = jnp.where(hit, float(r), rank)
            x = jnp.where(hit, NEG_INF, x)
        return rank

    n = topk + 1
    top_sorted(s0, n, a_s)
    rank1 = top_sorted(s1, n, b_s, want_rank=True)
    a = a_s[0:topk, :]
    b = b_s[0:topk, :]
    counts = _pair_counts(n)
    row8 = lax.broadcasted_iota(jnp.int32, (8, 1), 0)
    parts = [a[0:1, :] + b]
    i = 1
    while counts[i] > 1:
        parts.append(jnp.where(row8 < counts[i], a[i:i + 1, :] + b[0:8, :], NEG_INF))
        i += 1
    parts.append(a[i:, :] + b[0:1, :])
    last = jnp.where(row8 == 0, a[0:1, :] + b_s[topk:n, :],
                     jnp.where(row8 == 1, a_s[topk:n, :] + b[0:1, :], NEG_INF))
    parts.append(last)
    cand = jnp.concatenate(parts, axis=0)
    top_sorted(cand, n, f_s)
    f = f_s[0:topk, :]
    z = jnp.sum(jnp.exp(f - f[0:1, :]), axis=0, keepdims=True)
    tau = 0.5 * (f[topk - 1:topk, :] + f_s[topk:n, :])
    thr = tau - s0
    cnt = jnp.zeros_like(thr)
    for r in range(topk):
        cnt = jnp.where(b[r:r + 1, :] >= thr, float(r + 1), cnt)
    cnt_ref[...] = cnt
    w0_ref[...] = jnp.exp(s0 - a[0:1, :]) * (0.5 / z)
    rk_ref[...] = rank1.astype(BF16)
    w1_ref[...] = jnp.exp(s1 - b[0:1, :]).astype(BF16)


def _peer_route(q, keys, tm):
    T = q.shape[0]
    H, _, NK, half = keys.shape
    assert _pair_counts(PEER_TOPK + 1)[7] > 1 >= _pair_counts(PEER_TOPK + 1)[8] and NK > PEER_TOPK
    assert RANK_UNSELECTED > PEER_TOPK
    per_key = jax.ShapeDtypeStruct((H, NK, T), F32)
    per_col = jax.ShapeDtypeStruct((H, NK, T), BF16)
    ospec = pl.BlockSpec((None, NK, tm), lambda i, h: (h, 0, i))
    return pl.pallas_call(
        functools.partial(_peer_route_body, half=half, topk=PEER_TOPK),
        grid=(T // tm, H),
        in_specs=[pl.BlockSpec((tm, 2 * half), lambda i, h: (i, h)),
                  pl.BlockSpec((None, 2, NK, half), lambda i, h: (h, 0, 0, 0))],
        out_specs=[ospec, ospec, ospec, ospec],
        out_shape=[per_key, per_key, per_col, per_col],
        scratch_shapes=[pltpu.VMEM((PEER_TOPK + 8, tm), F32)] * 3,
        compiler_params=_params("parallel", "parallel"),
        name="peer_route",
    )(q, keys)


def _peer_mix_body(xn_ref, xres_ref, u_ref, vt_ref, cnt_ref, w0_ref, rk_ref, w1_ref, o_ref,
                   h_s, gh_s, acc_s, *, tl):
    H, ti, tm = cnt_ref.shape
    NK = rk_ref.shape[1]
    j = pl.program_id(1)

    last = pl.num_programs(1) - 1
    cur = j % 2

    def fold():
        acc_s[...] += jnp.dot(vt_ref[...], gh_s[1 - cur], preferred_element_type=F32)

    def row_tile(row):
        tile = jnp.broadcast_to(row, (BF16_SUBLANES, row.shape[1])).astype(BF16)
        return pltpu.repeat(tile, NK // BF16_SUBLANES, axis=0)

    def build():
        for ii in range(ti):
            rs = slice(ii * NK, (ii + 1) * NK)
            for lc in range(tm // tl):
                ls = slice(lc * tl, (lc + 1) * tl)
                g = None
                for h in range(H):
                    c = row_tile(cnt_ref[h, ii:ii + 1, ls])
                    a = row_tile(w0_ref[h, ii:ii + 1, ls])
                    term = a * jnp.where(rk_ref[h, :, ls] < c, w1_ref[h, :, ls], 0)
                    g = term if g is None else g + term
                gh_s[cur, rs, ls] = g
        h_s[...] = _nt_dot(u_ref[...], xn_ref[...])
        for ii in range(ti):
            rs = slice(ii * NK, (ii + 1) * NK)
            gh_s[cur, rs, :] = gh_s[cur, rs, :] * _gelu_tanh_x2(h_s[rs, :]).astype(BF16)

    @pl.when(j == 0)
    def _():
        acc_s[...] = jnp.zeros_like(acc_s)
        build()

    @pl.when(jnp.logical_and(j > 0, j < last))
    def _():
        fold()
        build()

    @pl.when(j == last)
    def _():
        fold()
        o_ref[...] = xres_ref[...] + acc_s[...].T


def _peer_mix(xn, xres, u_bf, vt_bf, cnt, w0, rk, w1, tm, ti, tl):
    T, D = xn.shape
    H, NK, _ = cnt.shape
    E = u_bf.shape[0]
    te = ti * NK
    n_e = E // te
    build = lambda j: jnp.minimum(j, n_e - 1)
    fold = lambda j: jnp.maximum(j - 1, 0)
    keys = pl.BlockSpec((H, ti, tm), lambda i, j: (0, build(j), i))
    full = pl.BlockSpec((H, NK, tm), lambda i, j: (0, 0, i))
    return pl.pallas_call(
        functools.partial(_peer_mix_body, tl=tl),
        grid=(T // tm, n_e + 1),
        in_specs=[pl.BlockSpec((tm, D), lambda i, j: (i, 0)),
                  pl.BlockSpec((tm, D), lambda i, j: (i, 0)),
                  pl.BlockSpec((te, D), lambda i, j: (build(j), 0)),
                  pl.BlockSpec((D, te), lambda i, j: (0, fold(j))),
                  keys, keys, full, full],
        out_specs=pl.BlockSpec((tm, D), lambda i, j: (i, 0)),
        out_shape=jax.ShapeDtypeStruct((T, D), F32),
        scratch_shapes=[pltpu.VMEM((te, tm), F32), pltpu.VMEM((2, te, tm), BF16),
                        pltpu.VMEM((D, tm), F32)],
        compiler_params=_params("parallel", "arbitrary"),
        name="peer_mix",
    )(xn, xres, u_bf, vt_bf, cnt, w0, rk, w1)


def _final_norm_body(x_ref, w_ref, o_ref):
    o_ref[...] = _rms(x_ref[...], w_ref[...])


def _final_norm(x2, w, tm):
    T, D = x2.shape
    return pl.pallas_call(
        _final_norm_body,
        grid=(T // tm,),
        in_specs=[pl.BlockSpec((tm, D), lambda i: (i, 0)), pl.BlockSpec((1, D), lambda i: (0, 0))],
        out_specs=pl.BlockSpec((tm, D), lambda i: (i, 0)),
        out_shape=jax.ShapeDtypeStruct((T, D), F32),
        compiler_params=_params("parallel"),
        name="final_norm",
    )(x2, w)


def _rope_table(pos, inv_freq):
    ang = pos[:, None] * inv_freq[None, :]
    ang = jnp.concatenate([ang, ang], axis=-1)
    return jnp.cos(ang), jnp.sin(ang)


def _position_tables(S):
    t = jnp.arange(S, dtype=F32)
    n_rows = S // GRID_W
    rows = jnp.repeat(jnp.arange(n_rows, dtype=F32), GRID_W)
    cols = jnp.tile(jnp.arange(GRID_W, dtype=F32), n_rows)
    ret_inv = 1.0 / (10000.0 ** jnp.linspace(0.0, 1.0, HEAD_DIM // 2, dtype=F32))
    ret_cos, ret_sin = _rope_table(t, ret_inv)
    ax_n = HEAD_DIM // 4
    ax_inv = ROPE_BASE ** (-jnp.arange(ax_n, dtype=F32) / ax_n)
    cr, sr = _rope_table(rows, ax_inv)
    cc, sc = _rope_table(cols, ax_inv)
    ax_cos = jnp.concatenate([cr, cc], axis=-1)
    ax_sin = jnp.concatenate([sr, sc], axis=-1)
    return (jnp.tile(ret_cos, (1, RET_HEADS)), jnp.tile(ret_sin, (1, RET_HEADS)),
            jnp.tile(ax_cos, (1, ATT_HEADS)), jnp.tile(ax_sin, (1, ATT_HEADS)))


def _tiles(B, S, n_keys):
    T = B * S
    return dict(
        tm=min(512, T),
        tb=min(256, S),
        tp=min(512, S),
        tq=min(256, S),
        tk=min(512, S),
        tr=min(512, T),
        tmix=min(512, T),
        ti=min(8, n_keys),
        tl=min(256, T),
    )


def kernel(x, ln1_w, w_in, ret_log_decay, ret_gn_w, lru_conv_w, lru_conv_b, lru_gate_w, lru_gate_b,
           lru_lambda, attn_q_norm, attn_k_norm, w_out, ln2_w, peer_wq, peer_keys, peer_u, peer_v,
           lnf_w):
    B, S, D = x.shape
    T = B * S
    depth = w_in.shape[0]
    n_keys = peer_keys.shape[3]
    tl = _tiles(B, S, n_keys)
    widths = (RET_W,) * 4 + (LRU_W,) * 2 + (ATT_W, KV_W, KV_W)
    ret_cos, ret_sin, ax_cos, ax_sin = _position_tables(S)
    x2 = x.reshape(T, D)
    for l in range(depth):
        rq, rk, rv, rg, lx, lgt, aq, ak, av = _inproj(
            x2, ln1_w[l].reshape(1, D), w_in[l].astype(BF16), widths, tl["tm"])
        o_ret = _retention(rq, rk, rv, rg, ret_cos, ret_sin, ret_log_decay[l],
                           ret_gn_w[l].reshape(1, RET_W), B, S)
        o_lru = _lru(lx, lgt, lru_conv_w[l], lru_conv_b[l], lru_gate_w[l], lru_gate_b[l],
                     lru_lambda[l], B, S, tl["tb"])
        qp, kt, vp = _attn_prep(aq, ak, av, ax_cos, ax_sin, attn_q_norm[l], attn_k_norm[l],
                                B, S, tl["tp"])
        o_att = _attention(qp, kt, vp, B, S, tl["tq"], tl["tk"])
        x2, xn, q = _outproj(x2, o_ret, o_lru, o_att, w_out[l].astype(BF16),
                             ln2_w[l].reshape(1, D), peer_wq[l].astype(BF16), tl["tm"])
        cnt, w0, rk, w1 = _peer_route(q, peer_keys[l], tl["tr"])
        x2 = _peer_mix(xn, x2, peer_u[l].astype(BF16), peer_v[l].T.astype(BF16),
                       cnt, w0, rk, w1, tl["tmix"], tl["ti"], tl["tl"])
    return _final_norm(x2, lnf_w.reshape(1, D), tl["tm"]).reshape(B, S, D)
```

```python
import functools

import jax
import jax.numpy as jnp
from jax import lax
from jax.experimental import pallas as pl
from jax.experimental.pallas import tpu as pltpu

F32 = jnp.float32
BF16 = jnp.bfloat16
EPS = 1e-6
LOG2_E = 1.4426950408889634

GRID_W = 64
CHUNK = 128
HEAD_DIM = 64
RET_HEADS = 4
RET_W = RET_HEADS * HEAD_DIM
LRU_W = 256
LRU_CONV = 4
LRU_C = 8.0
ATT_HEADS = 8
ATT_KV_HEADS = 2
ATT_GROUP = ATT_HEADS // ATT_KV_HEADS
ATT_W = ATT_HEADS * HEAD_DIM
KV_W = ATT_KV_HEADS * HEAD_DIM
ROPE_BASE = 10000.0
PEER_TOPK = 16
BF16_SUBLANES = 16

VMEM_LIMIT_BYTES = 56 * 1024 * 1024
NEG_INF = float("-inf")


def _params(*sem):
    return pltpu.CompilerParams(dimension_semantics=sem, vmem_limit_bytes=VMEM_LIMIT_BYTES)


def _rms(x, w):
    return x * lax.rsqrt(jnp.mean(x * x, axis=-1, keepdims=True) + EPS) * w


def _gelu_tanh(x):
    return 0.5 * x * (1.0 + jnp.tanh(0.7978845608028654 * (x + 0.044715 * (x * x * x))))


def _gelu_tanh_x2(x):
    return x + x * jnp.tanh(x * (0.7978845608028654 + (0.7978845608028654 * 0.044715) * (x * x)))


def _sigmoid(x):
    return 1.0 / (1.0 + jnp.exp(-x))


def _nt_dot(a, b, precision=None):
    return lax.dot_general(a, b, (((1,), (1,)), ((), ())), precision=precision,
                           preferred_element_type=F32)


def _group_sum(x, member):
    hi = x.astype(BF16)
    lo = (x - hi.astype(F32)).astype(BF16)
    return (jnp.dot(hi, member, preferred_element_type=F32)
            + jnp.dot(lo, member, preferred_element_type=F32))


def _tn_dot(a, b, precision=None):
    return lax.dot_general(a, b, (((0,), (0,)), ((), ())), precision=precision,
                           preferred_element_type=F32)


def _inproj_body(x_ref, lnw_ref, w_ref, *out_refs, widths):
    hb = _rms(x_ref[...], lnw_ref[...]).astype(BF16)
    off = 0
    for o_ref, wd in zip(out_refs, widths):
        o_ref[...] = jnp.dot(hb, w_ref[:, off:off + wd],
                             preferred_element_type=F32).astype(o_ref.dtype)
        off += wd


def _inproj(x2, lnw, w_bf, widths, tm):
    T, D = x2.shape
    n_in = w_bf.shape[1]
    return pl.pallas_call(
        functools.partial(_inproj_body, widths=widths),
        grid=(T // tm,),
        in_specs=[pl.BlockSpec((tm, D), lambda i: (i, 0)),
                  pl.BlockSpec((1, D), lambda i: (0, 0)),
                  pl.BlockSpec((D, n_in), lambda i: (0, 0))],
        out_specs=[pl.BlockSpec((tm, wd), lambda i: (i, 0)) for wd in widths],
        out_shape=[jax.ShapeDtypeStruct((T, wd), F32) for wd in widths],
        compiler_params=_params("parallel"),
        name="inproj",
    )(x2, lnw, w_bf)


def _ret_body(ld_ref, q_ref, k_ref, v_ref, g_ref, cos_ref, sin_ref, lgl_ref, gnw_ref, o_ref,
              qr_s, kr_s, st_s, d_s, *, n_chunks):
    C, W = CHUNK, RET_W
    lane = lax.broadcasted_iota(jnp.int32, (1, W), 1)
    first_half = (lane % HEAD_DIM) < (HEAD_DIM // 2)
    head_masks = [(lane // HEAD_DIM) == h for h in range(RET_HEADS)]
    same_head = (lax.broadcasted_iota(jnp.int32, (W, W), 0) // HEAD_DIM
                 == lax.broadcasted_iota(jnp.int32, (W, W), 1) // HEAD_DIM)
    bd = same_head.astype(F32)
    lgf = lgl_ref[0:1, :]
    lgb = lgl_ref[1:2, :]
    idx = lax.broadcasted_iota(jnp.int32, (C, 1), 0).astype(F32)
    decf = jnp.exp(lgf * float(C))
    decb = jnp.exp(lgb * float(C))

    diff = (lax.broadcasted_iota(jnp.int32, (C, C), 0)
            - lax.broadcasted_iota(jnp.int32, (C, C), 1)).astype(F32)
    for h in range(RET_HEADS):
        d_s[h] = jnp.where(diff >= 0.0,
                           jnp.exp(ld_ref[0, h] * jnp.maximum(diff, 0.0)),
                           jnp.exp(ld_ref[1, h] * jnp.maximum(-diff, 0.0)))

    def rot_half(x):
        return jnp.where(first_half, -pltpu.roll(x, W - HEAD_DIM // 2, 1),
                         pltpu.roll(x, HEAD_DIM // 2, 1))

    def prep(c, carry):
        r0 = pl.multiple_of(c * C, C)
        cs = cos_ref[pl.ds(r0, C), :]
        sn = sin_ref[pl.ds(r0, C), :]
        q = q_ref[pl.ds(r0, C), :]
        k = k_ref[pl.ds(r0, C), :]
        qr_s[pl.ds(r0, C), :] = q * cs + rot_half(q) * sn
        kr_s[pl.ds(r0, C), :] = (k * cs + rot_half(k) * sn) * (HEAD_DIM ** -0.5)
        return carry

    lax.fori_loop(0, n_chunks, prep, 0)

    st_s[...] = jnp.zeros((W, W), F32)

    def fwd(c, carry):
        r0 = pl.multiple_of(c * C, C)
        q = qr_s[pl.ds(r0, C), :]
        k = kr_s[pl.ds(r0, C), :]
        v = v_ref[pl.ds(r0, C), :]
        st = st_s[...]
        qw = q * jnp.exp(lgf * (idx + 1.0))
        o_ref[pl.ds(r0, C), :] = jnp.dot(qw.astype(BF16), (st * bd).astype(BF16),
                                         preferred_element_type=F32)
        kw = k * jnp.exp(lgf * (float(C) - 1.0 - idx))
        st_s[...] = st * decf + _tn_dot(kw.astype(BF16), v.astype(BF16))
        return carry

    lax.fori_loop(0, n_chunks, fwd, 0)

    st_s[...] = jnp.zeros((W, W), F32)

    def bwd(t, carry):
        c = n_chunks - 1 - t
        r0 = pl.multiple_of(c * C, C)
        q = qr_s[pl.ds(r0, C), :]
        k = kr_s[pl.ds(r0, C), :]
        v = v_ref[pl.ds(r0, C), :]
        st = st_s[...]
        qw = q * jnp.exp(lgb * (float(C) - idx))
        vb = v.astype(BF16)
        kb = k.astype(BF16)
        o = o_ref[pl.ds(r0, C), :] + jnp.dot(qw.astype(BF16), (st * bd).astype(BF16),
                                              preferred_element_type=F32)
        kw = k * jnp.exp(lgb * idx)
        st_s[...] = st * decb + _tn_dot(kw.astype(BF16), vb)
        for h in range(RET_HEADS):
            qm = jnp.where(head_masks[h], q, 0.0).astype(BF16)
            s = _nt_dot(qm, kb) * d_s[h]
            oh = jnp.dot(s.astype(BF16), vb, preferred_element_type=F32)
            o = o + jnp.where(head_masks[h], oh, 0.0)
        ms = _group_sum(o * o, bd.astype(BF16)) * (1.0 / HEAD_DIM)
        on = o * lax.rsqrt(ms + EPS) * gnw_ref[...]
        g = g_ref[pl.ds(r0, C), :]
        o_ref[pl.ds(r0, C), :] = (g * _sigmoid(g)) * on
        return carry

    lax.fori_loop(0, n_chunks, bwd, 0)


def _retention(rq, rk, rv, rg, cos, sin, log_decay, gn_w, B, S):
    W = RET_W
    lgl = jnp.repeat(log_decay, HEAD_DIM, axis=1)
    seq = pl.BlockSpec((S, W), lambda b: (b, 0), pipeline_mode=pl.Buffered(1))
    tab = pl.BlockSpec((S, W), lambda b: (0, 0), pipeline_mode=pl.Buffered(1))
    return pl.pallas_call(
        functools.partial(_ret_body, n_chunks=S // CHUNK),
        grid=(B,),
        in_specs=[pl.BlockSpec(memory_space=pltpu.SMEM), seq, seq, seq, seq, tab, tab,
                  pl.BlockSpec((2, W), lambda b: (0, 0)),
                  pl.BlockSpec((1, W), lambda b: (0, 0))],
        out_specs=pl.BlockSpec((S, W), lambda b: (b, 0)),
        out_shape=jax.ShapeDtypeStruct((B * S, W), F32),
        scratch_shapes=[pltpu.VMEM((S, W), F32), pltpu.VMEM((S, W), F32),
                        pltpu.VMEM((W, W), F32), pltpu.VMEM((RET_HEADS, CHUNK, CHUNK), F32)],
        compiler_params=_params("parallel"),
        name="retention",
    )(log_decay, rq, rk, rv, rg, cos, sin, lgl, gn_w)


def _lru_body(x_ref, gt_ref, cw_ref, cb_ref, gw_ref, gb_ref, lam_ref, o_ref, xc_s, h_s,
              *, n_blocks, tb):
    W = LRU_W
    S = n_blocks * tb
    z = -lam_ref[...]
    softplus = jnp.maximum(z, 0.0) + jnp.log(1.0 + jnp.exp(-jnp.abs(z)))
    cl = -LRU_C * softplus
    row = lax.broadcasted_iota(jnp.int32, (tb, 1), 0)
    left = LRU_CONV // 2

    def conv_block(kb):
        r0 = pl.multiple_of(kb * tb, tb)
        prev = x_ref[pl.ds(pl.multiple_of(jnp.maximum(r0 - 8, 0), 8), 8), :]
        nxt = x_ref[pl.ds(pl.multiple_of(jnp.minimum(r0 + tb, S - 8), 8), 8), :]
        ext = jnp.concatenate([prev, x_ref[pl.ds(r0, tb), :], nxt], axis=0)
        t = row + r0
        acc = cb_ref[...] + jnp.zeros((tb, W), F32)
        for j in range(LRU_CONV):
            off = j - left
            if off == 0:
                xs = ext[8:8 + tb]
            else:
                xs = pltpu.roll(ext, (-off) % (tb + 16), 0)[8:8 + tb]
                xs = jnp.where((t + off >= 0) & (t + off < S), xs, 0.0)
            acc = acc + xs * cw_ref[j:j + 1, :]
        return acc

    def gates(xc, d):
        g = jnp.dot(xc.astype(BF16), gw_ref[:, d * 2 * W:(d + 1) * 2 * W],
                    preferred_element_type=F32) + gb_ref[:, d * 2 * W:(d + 1) * 2 * W]
        r = _sigmoid(g[:, :W])
        i = _sigmoid(g[:, W:])
        log_a = cl[d:d + 1, :] * r
        a = jnp.exp(log_a)
        b = jnp.sqrt(-jnp.tanh(log_a) * (a * a + 1.0)) * (i * xc)
        return a, b

    def scan_block(a, b, reverse):
        d = 1
        while d < tb:
            if reverse:
                keep = row < tb - d
                a_sh = jnp.where(keep, pltpu.roll(a, tb - d, 0), 1.0)
                b_sh = jnp.where(keep, pltpu.roll(b, tb - d, 0), 0.0)
            else:
                keep = row >= d
                a_sh = jnp.where(keep, pltpu.roll(a, d, 0), 1.0)
                b_sh = jnp.where(keep, pltpu.roll(b, d, 0), 0.0)
            b = a * b_sh + b
            a = a * a_sh
            d *= 2
        return a, b

    def fwd(kb, carry):
        r0 = pl.multiple_of(kb * tb, tb)
        xc = conv_block(kb)
        xc_s[pl.ds(r0, tb), :] = xc
        a, b = gates(xc, 0)
        a, b = scan_block(a, b, False)
        h = a * carry + b
        h_s[pl.ds(r0, tb), :] = h
        return h[tb - 1:tb, :]

    lax.fori_loop(0, n_blocks, fwd, jnp.zeros((1, W), F32))

    def bwd(t, carry):
        kb = n_blocks - 1 - t
        r0 = pl.multiple_of(kb * tb, tb)
        xc = xc_s[pl.ds(r0, tb), :]
        a, b = gates(xc, 1)
        a, b = scan_block(a, b, True)
        h = a * carry + b
        o_ref[pl.ds(r0, tb), :] = (h_s[pl.ds(r0, tb), :] + h) * _gelu_tanh(gt_ref[pl.ds(r0, tb), :])
        return h[0:1, :]

    lax.fori_loop(0, n_blocks, bwd, jnp.zeros((1, W), F32))


def _lru(lx, lgt, conv_w, conv_b, gate_w, gate_b, lam, B, S, tb):
    W = LRU_W
    nb = W // HEAD_DIM
    eye = jnp.eye(nb, dtype=F32)
    gw = jnp.einsum('dgnkm,nj->nkdgjm', gate_w, eye).reshape(W, 4 * W).astype(BF16)
    gb = gate_b.reshape(1, 4 * W)
    seq = pl.BlockSpec((S, W), lambda b: (b, 0))
    const = lambda shape: pl.BlockSpec(shape, lambda b: (0,) * len(shape))
    return pl.pallas_call(
        functools.partial(_lru_body, n_blocks=S // tb, tb=tb),
        grid=(B,),
        in_specs=[seq, seq, const((LRU_CONV, W)), const((1, W)), const((W, 4 * W)),
                  const((1, 4 * W)), const((2, W))],
        out_specs=seq,
        out_shape=jax.ShapeDtypeStruct((B * S, W), F32),
        scratch_shapes=[pltpu.VMEM((S, W), F32), pltpu.VMEM((S, W), F32)],
        compiler_params=_params("parallel"),
        name="rglru",
    )(lx, lgt, conv_w, conv_b.reshape(1, W), gw, gb, lam)


def _attn_prep_body(q_ref, k_ref, v_ref, cos_ref, sin_ref, qnw_ref, knw_ref, rep_ref, rept_ref,
                    qo_ref, kt_ref, vo_ref):
    def head_norm_rot(x, w, cs, sn):
        W = x.shape[-1]
        lane = lax.broadcasted_iota(jnp.int32, (1, W), 1)
        same_head = (lax.broadcasted_iota(jnp.int32, (W, W), 0) // HEAD_DIM
                     == lax.broadcasted_iota(jnp.int32, (W, W), 1) // HEAD_DIM)
        ms = _group_sum(x * x, same_head.astype(BF16)) * (1.0 / HEAD_DIM)
        xn = x * lax.rsqrt(ms + EPS) * w
        quarter = HEAD_DIM // 4
        rot = jnp.where((lane % (2 * quarter)) < quarter, -pltpu.roll(xn, W - quarter, 1),
                        pltpu.roll(xn, quarter, 1))
        return xn * cs + rot * sn

    cs = cos_ref[...]
    sn = sin_ref[...]
    q = head_norm_rot(q_ref[...], qnw_ref[...], cs, sn)
    qo_ref[...] = (q * (HEAD_DIM ** -0.5 * LOG2_E)).astype(BF16)
    k = head_norm_rot(k_ref[...], knw_ref[...], cs[:, :KV_W], sn[:, :KV_W]).astype(BF16)
    kt_ref[...] = _nt_dot(rept_ref[...], k).astype(BF16)
    vo_ref[...] = jnp.dot(v_ref[...].astype(BF16), rep_ref[...],
                          preferred_element_type=F32).astype(BF16)


def _attn_prep(aq, ak, av, cos, sin, qn_w, kn_w, B, S, tm):
    T = B * S
    ns = S // tm
    GW = ATT_GROUP * HEAD_DIM
    src = jnp.arange(ATT_KV_HEADS * GW)
    src = (src // GW) * HEAD_DIM + src % HEAD_DIM
    rep = (jnp.arange(KV_W)[:, None] == src[None, :]).astype(BF16)
    return pl.pallas_call(
        _attn_prep_body,
        grid=(B, ns),
        in_specs=[pl.BlockSpec((tm, ATT_W), lambda b, s: (b * ns + s, 0)),
                  pl.BlockSpec((tm, KV_W), lambda b, s: (b * ns + s, 0)),
                  pl.BlockSpec((tm, KV_W), lambda b, s: (b * ns + s, 0)),
                  pl.BlockSpec((tm, ATT_W), lambda b, s: (s, 0)),
                  pl.BlockSpec((tm, ATT_W), lambda b, s: (s, 0)),
                  pl.BlockSpec((1, ATT_W), lambda b, s: (0, 0)),
                  pl.BlockSpec((1, KV_W), lambda b, s: (0, 0)),
                  pl.BlockSpec((KV_W, ATT_KV_HEADS * GW), lambda b, s: (0, 0)),
                  pl.BlockSpec((ATT_KV_HEADS * GW, KV_W), lambda b, s: (0, 0))],
        out_specs=[pl.BlockSpec((tm, ATT_W), lambda b, s: (b * ns + s, 0)),
                   pl.BlockSpec((None, ATT_KV_HEADS * GW, tm), lambda b, s: (b, 0, s)),
                   pl.BlockSpec((tm, ATT_KV_HEADS * GW), lambda b, s: (b * ns + s, 0))],
        out_shape=[jax.ShapeDtypeStruct((T, ATT_W), BF16),
                   jax.ShapeDtypeStruct((B, ATT_KV_HEADS * GW, S), BF16),
                   jax.ShapeDtypeStruct((T, ATT_KV_HEADS * GW), BF16)],
        compiler_params=_params("parallel", "parallel"),
        name="attn_prep",
    )(aq, ak, av, cos, sin, jnp.tile(qn_w, ATT_HEADS).reshape(1, ATT_W),
      jnp.tile(kn_w, ATT_KV_HEADS).reshape(1, KV_W), rep, rep.T)


def _attn_body(q_ref, kt_ref, v_ref, o_ref, *, tq, tk, nk):
    GW = ATT_GROUP * HEAD_DIM
    lane = lax.broadcasted_iota(jnp.int32, (1, GW), 1)
    masks = [(lane // HEAD_DIM) == g for g in range(ATT_GROUP)]
    q = q_ref[...]
    zero = jnp.zeros_like(q)
    qs = jnp.concatenate([jnp.where(m, q, zero) for m in masks], axis=0)
    rows = ATT_GROUP * tq
    m_run = jnp.full((rows, 1), NEG_INF, F32)
    l_run = jnp.zeros((rows, 1), F32)
    acc = jnp.zeros((rows, GW), F32)
    for c in range(nk):
        s = jnp.dot(qs, kt_ref[:, c * tk:(c + 1) * tk], preferred_element_type=F32)
        m_new = jnp.maximum(m_run, jnp.max(s, axis=-1, keepdims=True))
        alpha = jnp.exp2(m_run - m_new)
        p = jnp.exp2(s - m_new)
        l_run = alpha * l_run + jnp.sum(p, axis=-1, keepdims=True)
        acc = alpha * acc + jnp.dot(p.astype(BF16), v_ref[c * tk:(c + 1) * tk, :],
                                    preferred_element_type=F32)
        m_run = m_new
    on = acc * (1.0 / l_run)
    o = jnp.zeros((tq, GW), F32)
    for g in range(ATT_GROUP):
        o = o + jnp.where(masks[g], on[g * tq:(g + 1) * tq], 0.0)
    o_ref[...] = o


def _attention(qp, kt, vp, B, S, tq, tk):
    T = B * S
    nq = S // tq
    GW = ATT_GROUP * HEAD_DIM
    return pl.pallas_call(
        functools.partial(_attn_body, tq=tq, tk=tk, nk=S // tk),
        grid=(B, ATT_KV_HEADS, nq),
        in_specs=[pl.BlockSpec((tq, GW), lambda b, h, i: (b * nq + i, h)),
                  pl.BlockSpec((None, GW, S), lambda b, h, i: (b, h, 0)),
                  pl.BlockSpec((S, GW), lambda b, h, i: (b, h))],
        out_specs=pl.BlockSpec((tq, GW), lambda b, h, i: (b * nq + i, h)),
        out_shape=jax.ShapeDtypeStruct((T, ATT_W), F32),
        compiler_params=_params("parallel", "parallel", "parallel"),
        name="attention",
    )(qp, kt, vp)


def _outproj_body(x_ref, oret_ref, olru_ref, oatt_ref, w_ref, ln2_ref, wq_ref,
                  xo_ref, xn_ref, q_ref):
    y = jnp.dot(oret_ref[...].astype(BF16), w_ref[0:RET_W, :], preferred_element_type=F32)
    y = y + jnp.dot(olru_ref[...].astype(BF16), w_ref[RET_W:RET_W + LRU_W, :],
                    preferred_element_type=F32)
    y = y + jnp.dot(oatt_ref[...].astype(BF16), w_ref[RET_W + LRU_W:, :],
                    preferred_element_type=F32)
    x = x_ref[...] + y
    xo_ref[...] = x
    xn = _rms(x, ln2_ref[...]).astype(BF16)
    xn_ref[...] = xn
    q_ref[...] = jnp.dot(xn, wq_ref[...], preferred_element_type=F32)


def _outproj(x2, o_ret, o_lru, o_att, w_out_bf, ln2, wq_bf, tm):
    T, D = x2.shape
    QW = wq_bf.shape[1]
    MW = w_out_bf.shape[0]
    tok = lambda w: pl.BlockSpec((tm, w), lambda i: (i, 0))
    return pl.pallas_call(
        _outproj_body,
        grid=(T // tm,),
        in_specs=[tok(D), tok(RET_W), tok(LRU_W), tok(ATT_W),
                  pl.BlockSpec((MW, D), lambda i: (0, 0)),
                  pl.BlockSpec((1, D), lambda i: (0, 0)),
                  pl.BlockSpec((D, QW), lambda i: (0, 0))],
        out_specs=[tok(D), tok(D), tok(QW)],
        out_shape=[jax.ShapeDtypeStruct((T, D), F32), jax.ShapeDtypeStruct((T, D), BF16),
                   jax.ShapeDtypeStruct((T, QW), F32)],
        compiler_params=_params("parallel"),
        name="outproj",
    )(x2, o_ret, o_lru, o_att, w_out_bf, ln2, wq_bf)


def _sorting_network(n):
    pairs = []
    p = 1
    while p < n:
        k = p
        while k >= 1:
            for j in range(k % p, n - k, 2 * k):
                for i in range(min(k, n - j - k)):
                    if (i + j) // (2 * p) == (i + j + k) // (2 * p):
                        pairs.append((i + j, i + j + k))
            k //= 2
        p *= 2
    return pairs


def _pair_counts(n):
    return [n // (i + 1) for i in range(n)]


def _peer_route_body(q_ref, keys_ref, cnt_ref, w0_ref, rk_ref, w1_ref, a_s, b_s, f_s,
                     *, half, topk):
    s0 = _nt_dot(keys_ref[0], q_ref[:, :half])
    s1 = _nt_dot(keys_ref[1], q_ref[:, half:])

    def top_sorted(tiles, n, out_ref):
        rows = list(tiles) + [None] * (pl.next_power_of_2(len(tiles)) - len(tiles))
        for lo_i, hi_i in _sorting_network(len(rows)):
            x, y = rows[lo_i], rows[hi_i]
            if y is None:
                continue
            if x is None:
                rows[lo_i], rows[hi_i] = y, None
            else:
                rows[lo_i], rows[hi_i] = jnp.maximum(x, y), jnp.minimum(x, y)
        rows = [t for t in rows if t is not None]
        for r in range(n):
            m = jnp.max(rows[0], axis=0, keepdims=True)
            out_ref[r:r + 1, :] = m
            hit = rows[0] == m
            for v in range(min(len(rows), n - r - 1)):
                nxt = rows[v + 1] if v + 1 < len(rows) else NEG_INF
                rows[v] = jnp.where(hit, nxt, rows[v])

    def sublane_tiles(x):
        return [x[8 * v:8 * v + 8, :] for v in range(x.shape[0] // 8)]

    n = topk + 1
    top_sorted(sublane_tiles(s0), n, a_s)
    top_sorted(sublane_tiles(s1), n, b_s)
    a = a_s[0:topk, :]
    b = b_s[0:topk, :]
    counts = _pair_counts(n)
    row8 = lax.broadcasted_iota(jnp.int32, (8, 1), 0)
    parts = [a[0:1, :] + b[0:8, :], a[0:1, :] + b[8:16, :]]
    i = 1
    while counts[i] > 1:
        parts.append(jnp.where(row8 < counts[i], a[i:i + 1, :] + b[0:8, :], NEG_INF))
        i += 1
    parts.append(a[i:, :] + b[0:1, :])
    last = jnp.where(row8 == 0, a[0:1, :] + b_s[topk:n, :],
                     jnp.where(row8 == 1, a_s[topk:n, :] + b[0:1, :], NEG_INF))
    parts.append(last)
    top_sorted(parts, n, f_s)
    f = f_s[0:topk, :]
    z = jnp.sum(jnp.exp(f - f[0:1, :]), axis=0, keepdims=True)
    tau = 0.5 * (f[topk - 1:topk, :] + f_s[topk:n, :])
    thr = tau - s0
    cnt = jnp.zeros_like(thr)
    for r in range(topk):
        cnt = jnp.where(b[r:r + 1, :] >= thr, float(r + 1), cnt)
    cnt_ref[...] = cnt
    rank1 = jnp.zeros_like(s1)
    for r in range(n):
        rank1 = jnp.where(b_s[r:r + 1, :] > s1, float(r + 1), rank1)
    w0_ref[...] = jnp.exp(s0 - a[0:1, :]) * (0.5 / z)
    rk_ref[...] = rank1.astype(BF16)
    w1_ref[...] = jnp.exp(s1 - b[0:1, :]).astype(BF16)


def _peer_route(q, keys, tm):
    T = q.shape[0]
    H, _, NK, half = keys.shape
    assert _pair_counts(PEER_TOPK + 1)[7] > 1 >= _pair_counts(PEER_TOPK + 1)[8] and NK > PEER_TOPK
    assert PEER_TOPK == 2 * 8
    per_key = jax.ShapeDtypeStruct((H, NK, T), F32)
    per_col = jax.ShapeDtypeStruct((H, NK, T), BF16)
    ospec = pl.BlockSpec((None, NK, tm), lambda i, h: (h, 0, i))
    return pl.pallas_call(
        functools.partial(_peer_route_body, half=half, topk=PEER_TOPK),
        grid=(T // tm, H),
        in_specs=[pl.BlockSpec((tm, 2 * half), lambda i, h: (i, h)),
                  pl.BlockSpec((None, 2, NK, half), lambda i, h: (h, 0, 0, 0))],
        out_specs=[ospec, ospec, ospec, ospec],
        out_shape=[per_key, per_key, per_col, per_col],
        scratch_shapes=[pltpu.VMEM((PEER_TOPK + 8, tm), F32)] * 3,
        compiler_params=_params("parallel", "parallel"),
        name="peer_route",
    )(q, keys)


def _peer_mix_body(xn_ref, xres_ref, u_ref, vt_ref, cnt_ref, w0_ref, rk_ref, w1_ref, o_ref,
                   h_s, gh_s, acc_s, *, tl):
    H, ti, tm = cnt_ref.shape
    NK = rk_ref.shape[1]
    j = pl.program_id(1)

    last = pl.num_programs(1) - 1
    cur = j % 2

    def fold():
        acc_s[...] += jnp.dot(vt_ref[...], gh_s[1 - cur], preferred_element_type=F32)

    def row_tile(row):
        tile = jnp.broadcast_to(row, (BF16_SUBLANES, row.shape[1])).astype(BF16)
        return pltpu.repeat(tile, NK // BF16_SUBLANES, axis=0)

    def build():
        for ii in range(ti):
            rs = slice(ii * NK, (ii + 1) * NK)
            for lc in range(tm // tl):
                ls = slice(lc * tl, (lc + 1) * tl)
                g = None
                for h in range(H):
                    c = row_tile(cnt_ref[h, ii:ii + 1, ls])
                    a = row_tile(w0_ref[h, ii:ii + 1, ls])
                    term = a * jnp.where(rk_ref[h, :, ls] < c, w1_ref[h, :, ls], 0)
                    g = term if g is None else g + term
                gh_s[cur, rs, ls] = g
        h_s[...] = _nt_dot(u_ref[...], xn_ref[...])
        for ii in range(ti):
            rs = slice(ii * NK, (ii + 1) * NK)
            gh_s[cur, rs, :] = gh_s[cur, rs, :] * _gelu_tanh_x2(h_s[rs, :]).astype(BF16)

    @pl.when(j == 0)
    def _():
        acc_s[...] = jnp.zeros_like(acc_s)
        build()

    @pl.when(jnp.logical_and(j > 0, j < last))
    def _():
        fold()
        build()

    @pl.when(j == last)
    def _():
        fold()
        o_ref[...] = xres_ref[...] + acc_s[...].T


def _peer_mix(xn, xres, u_bf, vt_bf, cnt, w0, rk, w1, tm, ti, tl):
    T, D = xn.shape
    H, NK, _ = cnt.shape
    E = u_bf.shape[0]
    te = ti * NK
    n_e = E // te
    build = lambda j: jnp.minimum(j, n_e - 1)
    fold = lambda j: jnp.maximum(j - 1, 0)
    keys = pl.BlockSpec((H, ti, tm), lambda i, j: (0, build(j), i))
    full = pl.BlockSpec((H, NK, tm), lambda i, j: (0, 0, i))
    return pl.pallas_call(
        functools.partial(_peer_mix_body, tl=tl),
        grid=(T // tm, n_e + 1),
        in_specs=[pl.BlockSpec((tm, D), lambda i, j: (i, 0)),
                  pl.BlockSpec((tm, D), lambda i, j: (i, 0)),
                  pl.BlockSpec((te, D), lambda i, j: (build(j), 0)),
                  pl.BlockSpec((D, te), lambda i, j: (0, fold(j))),
                  keys, keys, full, full],
        out_specs=pl.BlockSpec((tm, D), lambda i, j: (i, 0)),
        out_shape=jax.ShapeDtypeStruct((T, D), F32),
        scratch_shapes=[pltpu.VMEM((te, tm), F32), pltpu.VMEM((2, te, tm), BF16),
                        pltpu.VMEM((D, tm), F32)],
        compiler_params=_params("parallel", "arbitrary"),
        name="peer_mix",
    )(xn, xres, u_bf, vt_bf, cnt, w0, rk, w1)


def _final_norm_body(x_ref, w_ref, o_ref):
    o_ref[...] = _rms(x_ref[...], w_ref[...])


def _final_norm(x2, w, tm):
    T, D = x2.shape
    return pl.pallas_call(
        _final_norm_body,
        grid=(T // tm,),
        in_specs=[pl.BlockSpec((tm, D), lambda i: (i, 0)), pl.BlockSpec((1, D), lambda i: (0, 0))],
        out_specs=pl.BlockSpec((tm, D), lambda i: (i, 0)),
        out_shape=jax.ShapeDtypeStruct((T, D), F32),
        compiler_params=_params("parallel"),
        name="final_norm",
    )(x2, w)


def _rope_table(pos, inv_freq):
    ang = pos[:, None] * inv_freq[None, :]
    ang = jnp.concatenate([ang, ang], axis=-1)
    return jnp.cos(ang), jnp.sin(ang)


def _position_tables(S):
    t = jnp.arange(S, dtype=F32)
    n_rows = S // GRID_W
    rows = jnp.repeat(jnp.arange(n_rows, dtype=F32), GRID_W)
    cols = jnp.tile(jnp.arange(GRID_W, dtype=F32), n_rows)
    ret_inv = 1.0 / (10000.0 ** jnp.linspace(0.0, 1.0, HEAD_DIM // 2, dtype=F32))
    ret_cos, ret_sin = _rope_table(t, ret_inv)
    ax_n = HEAD_DIM // 4
    ax_inv = ROPE_BASE ** (-jnp.arange(ax_n, dtype=F32) / ax_n)
    cr, sr = _rope_table(rows, ax_inv)
    cc, sc = _rope_table(cols, ax_inv)
    ax_cos = jnp.concatenate([cr, cc], axis=-1)
    ax_sin = jnp.concatenate([sr, sc], axis=-1)
    return (jnp.tile(ret_cos, (1, RET_HEADS)), jnp.tile(ret_sin, (1, RET_HEADS)),
            jnp.tile(ax_cos, (1, ATT_HEADS)), jnp.tile(ax_sin, (1, ATT_HEADS)))


def _tiles(B, S, n_keys):
    T = B * S
    return dict(
        tm=min(512, T),
        tb=min(256, S),
        tp=min(512, S),
        tq=min(256, S),
        tk=min(512, S),
        tr=min(512, T),
        tmix=min(512, T),
        ti=min(8, n_keys),
        tl=min(256, T),
    )


def kernel(x, ln1_w, w_in, ret_log_decay, ret_gn_w, lru_conv_w, lru_conv_b, lru_gate_w, lru_gate_b,
           lru_lambda, attn_q_norm, attn_k_norm, w_out, ln2_w, peer_wq, peer_keys, peer_u, peer_v,
           lnf_w):
    B, S, D = x.shape
    T = B * S
    depth = w_in.shape[0]
    n_keys = peer_keys.shape[3]
    tl = _tiles(B, S, n_keys)
    widths = (RET_W,) * 4 + (LRU_W,) * 2 + (ATT_W, KV_W, KV_W)
    ret_cos, ret_sin, ax_cos, ax_sin = _position_tables(S)
    x2 = x.reshape(T, D)
    for l in range(depth):
        rq, rk, rv, rg, lx, lgt, aq, ak, av = _inproj(
            x2, ln1_w[l].reshape(1, D), w_in[l].astype(BF16), widths, tl["tm"])
        o_ret = _retention(rq, rk, rv, rg, ret_cos, ret_sin, ret_log_decay[l],
                           ret_gn_w[l].reshape(1, RET_W), B, S)
        o_lru = _lru(lx, lgt, lru_conv_w[l], lru_conv_b[l], lru_gate_w[l], lru_gate_b[l],
                     lru_lambda[l], B, S, tl["tb"])
        qp, kt, vp = _attn_prep(aq, ak, av, ax_cos, ax_sin, attn_q_norm[l], attn_k_norm[l],
                                B, S, tl["tp"])
        o_att = _attention(qp, kt, vp, B, S, tl["tq"], tl["tk"])
        x2, xn, q = _outproj(x2, o_ret, o_lru, o_att, w_out[l].astype(BF16),
                             ln2_w[l].reshape(1, D), peer_wq[l].astype(BF16), tl["tm"])
        cnt, w0, rk, w1 = _peer_route(q, peer_keys[l], tl["tr"])
        x2 = _peer_mix(xn, x2, peer_u[l].astype(BF16), peer_v[l].T.astype(BF16),
                       cnt, w0, rk, w1, tl["tmix"], tl["ti"], tl["tl"])
    return _final_norm(x2, lnf_w.reshape(1, D), tl["tm"]).reshape(B, S, D)
```

```python
import functools

import jax
import jax.numpy as jnp
from jax import lax
from jax.experimental import pallas as pl
from jax.experimental.pallas import tpu as pltpu

F32 = jnp.float32
BF16 = jnp.bfloat16
EPS = 1e-6
LOG2_E = 1.4426950408889634

GRID_W = 64
CHUNK = 128
HEAD_DIM = 64
RET_HEADS = 4
RET_W = RET_HEADS * HEAD_DIM
LRU_W = 256
LRU_CONV = 4
LRU_C = 8.0
ATT_HEADS = 8
ATT_KV_HEADS = 2
ATT_GROUP = ATT_HEADS // ATT_KV_HEADS
ATT_W = ATT_HEADS * HEAD_DIM
KV_W = ATT_KV_HEADS * HEAD_DIM
ROPE_BASE = 10000.0
PEER_TOPK = 16
BF16_SUBLANES = 16

VMEM_LIMIT_BYTES = 56 * 1024 * 1024
NEG_INF = float("-inf")


def _params(*sem):
    return pltpu.CompilerParams(dimension_semantics=sem, vmem_limit_bytes=VMEM_LIMIT_BYTES)


def _rms(x, w):
    return x * lax.rsqrt(jnp.mean(x * x, axis=-1, keepdims=True) + EPS) * w


def _gelu_tanh(x):
    return 0.5 * x * (1.0 + jnp.tanh(0.7978845608028654 * (x + 0.044715 * (x * x * x))))


def _gelu_tanh_x2(x):
    return x + x * jnp.tanh(x * (0.7978845608028654 + (0.7978845608028654 * 0.044715) * (x * x)))


def _sigmoid(x):
    return 1.0 / (1.0 + jnp.exp(-x))


def _nt_dot(a, b, precision=None):
    return lax.dot_general(a, b, (((1,), (1,)), ((), ())), precision=precision,
                           preferred_element_type=F32)


def _group_sum(x, member):
    hi = x.astype(BF16)
    lo = (x - hi.astype(F32)).astype(BF16)
    return (jnp.dot(hi, member, preferred_element_type=F32)
            + jnp.dot(lo, member, preferred_element_type=F32))


def _tn_dot(a, b, precision=None):
    return lax.dot_general(a, b, (((0,), (0,)), ((), ())), precision=precision,
                           preferred_element_type=F32)


def _inproj_body(x_ref, lnw_ref, w_ref, *out_refs, widths):
    hb = _rms(x_ref[...], lnw_ref[...]).astype(BF16)
    off = 0
    for o_ref, wd in zip(out_refs, widths):
        o_ref[...] = jnp.dot(hb, w_ref[:, off:off + wd],
                             preferred_element_type=F32).astype(o_ref.dtype)
        off += wd


def _inproj(x2, lnw, w_bf, widths, tm):
    T, D = x2.shape
    n_in = w_bf.shape[1]
    return pl.pallas_call(
        functools.partial(_inproj_body, widths=widths),
        grid=(T // tm,),
        in_specs=[pl.BlockSpec((tm, D), lambda i: (i, 0)),
                  pl.BlockSpec((1, D), lambda i: (0, 0)),
                  pl.BlockSpec((D, n_in), lambda i: (0, 0))],
        out_specs=[pl.BlockSpec((tm, wd), lambda i: (i, 0)) for wd in widths],
        out_shape=[jax.ShapeDtypeStruct((T, wd), F32) for wd in widths],
        compiler_params=_params("parallel"),
        name="inproj",
    )(x2, lnw, w_bf)


def _ret_body(ld_ref, q_ref, k_ref, v_ref, g_ref, cos_ref, sin_ref, lgl_ref, gnw_ref, o_ref,
              qr_s, kr_s, st_s, d_s, *, n_chunks):
    C, W = CHUNK, RET_W
    lane = lax.broadcasted_iota(jnp.int32, (1, W), 1)
    first_half = (lane % HEAD_DIM) < (HEAD_DIM // 2)
    head_masks = [(lane // HEAD_DIM) == h for h in range(RET_HEADS)]
    same_head = (lax.broadcasted_iota(jnp.int32, (W, W), 0) // HEAD_DIM
                 == lax.broadcasted_iota(jnp.int32, (W, W), 1) // HEAD_DIM)
    bd = same_head.astype(F32)
    lgf = lgl_ref[0:1, :]
    lgb = lgl_ref[1:2, :]
    idx = lax.broadcasted_iota(jnp.int32, (C, 1), 0).astype(F32)
    decf = jnp.exp(lgf * float(C))
    decb = jnp.exp(lgb * float(C))

    diff = (lax.broadcasted_iota(jnp.int32, (C, C), 0)
            - lax.broadcasted_iota(jnp.int32, (C, C), 1)).astype(F32)
    for h in range(RET_HEADS):
        d_s[h] = jnp.where(diff >= 0.0,
                           jnp.exp(ld_ref[0, h] * jnp.maximum(diff, 0.0)),
                           jnp.exp(ld_ref[1, h] * jnp.maximum(-diff, 0.0)))

    def rot_half(x):
        return jnp.where(first_half, -pltpu.roll(x, W - HEAD_DIM // 2, 1),
                         pltpu.roll(x, HEAD_DIM // 2, 1))

    def prep(c, carry):
        r0 = pl.multiple_of(c * C, C)
        cs = cos_ref[pl.ds(r0, C), :]
        sn = sin_ref[pl.ds(r0, C), :]
        q = q_ref[pl.ds(r0, C), :]
        k = k_ref[pl.ds(r0, C), :]
        qr_s[pl.ds(r0, C), :] = q * cs + rot_half(q) * sn
        kr_s[pl.ds(r0, C), :] = (k * cs + rot_half(k) * sn) * (HEAD_DIM ** -0.5)
        return carry

    lax.fori_loop(0, n_chunks, prep, 0)

    st_s[...] = jnp.zeros((W, W), F32)

    def fwd(c, carry):
        r0 = pl.multiple_of(c * C, C)
        q = qr_s[pl.ds(r0, C), :]
        k = kr_s[pl.ds(r0, C), :]
        v = v_ref[pl.ds(r0, C), :]
        st = st_s[...]
        qw = q * jnp.exp(lgf * (idx + 1.0))
        o_ref[pl.ds(r0, C), :] = jnp.dot(qw.astype(BF16), (st * bd).astype(BF16),
                                         preferred_element_type=F32)
        kw = k * jnp.exp(lgf * (float(C) - 1.0 - idx))
        st_s[...] = st * decf + _tn_dot(kw.astype(BF16), v.astype(BF16))
        return carry

    lax.fori_loop(0, n_chunks, fwd, 0)

    st_s[...] = jnp.zeros((W, W), F32)

    def bwd(t, carry):
        c = n_chunks - 1 - t
        r0 = pl.multiple_of(c * C, C)
        q = qr_s[pl.ds(r0, C), :]
        k = kr_s[pl.ds(r0, C), :]
        v = v_ref[pl.ds(r0, C), :]
        st = st_s[...]
        qw = q * jnp.exp(lgb * (float(C) - idx))
        vb = v.astype(BF16)
        kb = k.astype(BF16)
        o = o_ref[pl.ds(r0, C), :] + jnp.dot(qw.astype(BF16), (st * bd).astype(BF16),
                                              preferred_element_type=F32)
        kw = k * jnp.exp(lgb * idx)
        st_s[...] = st * decb + _tn_dot(kw.astype(BF16), vb)
        for h in range(RET_HEADS):
            qm = jnp.where(head_masks[h], q, 0.0).astype(BF16)
            s = _nt_dot(qm, kb) * d_s[h]
            oh = jnp.dot(s.astype(BF16), vb, preferred_element_type=F32)
            o = o + jnp.where(head_masks[h], oh, 0.0)
        ms = _group_sum(o * o, bd.astype(BF16)) * (1.0 / HEAD_DIM)
        on = o * lax.rsqrt(ms + EPS) * gnw_ref[...]
        g = g_ref[pl.ds(r0, C), :]
        o_ref[pl.ds(r0, C), :] = (g * _sigmoid(g)) * on
        return carry

    lax.fori_loop(0, n_chunks, bwd, 0)


def _retention(rq, rk, rv, rg, cos, sin, log_decay, gn_w, B, S):
    W = RET_W
    lgl = jnp.repeat(log_decay, HEAD_DIM, axis=1)
    seq = pl.BlockSpec((S, W), lambda b: (b, 0), pipeline_mode=pl.Buffered(1))
    tab = pl.BlockSpec((S, W), lambda b: (0, 0), pipeline_mode=pl.Buffered(1))
    return pl.pallas_call(
        functools.partial(_ret_body, n_chunks=S // CHUNK),
        grid=(B,),
        in_specs=[pl.BlockSpec(memory_space=pltpu.SMEM), seq, seq, seq, seq, tab, tab,
                  pl.BlockSpec((2, W), lambda b: (0, 0)),
                  pl.BlockSpec((1, W), lambda b: (0, 0))],
        out_specs=pl.BlockSpec((S, W), lambda b: (b, 0)),
        out_shape=jax.ShapeDtypeStruct((B * S, W), F32),
        scratch_shapes=[pltpu.VMEM((S, W), F32), pltpu.VMEM((S, W), F32),
                        pltpu.VMEM((W, W), F32), pltpu.VMEM((RET_HEADS, CHUNK, CHUNK), F32)],
        compiler_params=_params("parallel"),
        name="retention",
    )(log_decay, rq, rk, rv, rg, cos, sin, lgl, gn_w)


def _lru_body(x_ref, gt_ref, cw_ref, cb_ref, gw_ref, gb_ref, lam_ref, o_ref, xc_s, h_s,
              *, n_blocks, tb):
    W = LRU_W
    S = n_blocks * tb
    z = -lam_ref[...]
    softplus = jnp.maximum(z, 0.0) + jnp.log(1.0 + jnp.exp(-jnp.abs(z)))
    cl = -LRU_C * softplus
    row = lax.broadcasted_iota(jnp.int32, (tb, 1), 0)
    left = LRU_CONV // 2

    def conv_block(kb):
        r0 = pl.multiple_of(kb * tb, tb)
        prev = x_ref[pl.ds(pl.multiple_of(jnp.maximum(r0 - 8, 0), 8), 8), :]
        nxt = x_ref[pl.ds(pl.multiple_of(jnp.minimum(r0 + tb, S - 8), 8), 8), :]
        ext = jnp.concatenate([prev, x_ref[pl.ds(r0, tb), :], nxt], axis=0)
        t = row + r0
        acc = cb_ref[...] + jnp.zeros((tb, W), F32)
        for j in range(LRU_CONV):
            off = j - left
            if off == 0:
                xs = ext[8:8 + tb]
            else:
                xs = pltpu.roll(ext, (-off) % (tb + 16), 0)[8:8 + tb]
                xs = jnp.where((t + off >= 0) & (t + off < S), xs, 0.0)
            acc = acc + xs * cw_ref[j:j + 1, :]
        return acc

    def gates(xc, d):
        g = jnp.dot(xc.astype(BF16), gw_ref[:, d * 2 * W:(d + 1) * 2 * W],
                    preferred_element_type=F32) + gb_ref[:, d * 2 * W:(d + 1) * 2 * W]
        r = _sigmoid(g[:, :W])
        i = _sigmoid(g[:, W:])
        log_a = cl[d:d + 1, :] * r
        a = jnp.exp(log_a)
        b = jnp.sqrt(-jnp.tanh(log_a) * (a * a + 1.0)) * (i * xc)
        return a, b

    def scan_block(a, b, reverse):
        d = 1
        while d < tb:
            if reverse:
                keep = row < tb - d
                a_sh = jnp.where(keep, pltpu.roll(a, tb - d, 0), 1.0)
                b_sh = jnp.where(keep, pltpu.roll(b, tb - d, 0), 0.0)
            else:
                keep = row >= d
                a_sh = jnp.where(keep, pltpu.roll(a, d, 0), 1.0)
                b_sh = jnp.where(keep, pltpu.roll(b, d, 0), 0.0)
            b = a * b_sh + b
            a = a * a_sh
            d *= 2
        return a, b

    def fwd(kb, carry):
        r0 = pl.multiple_of(kb * tb, tb)
        xc = conv_block(kb)
        xc_s[pl.ds(r0, tb), :] = xc
        a, b = gates(xc, 0)
        a, b = scan_block(a, b, False)
        h = a * carry + b
        h_s[pl.ds(r0, tb), :] = h
        return h[tb - 1:tb, :]

    lax.fori_loop(0, n_blocks, fwd, jnp.zeros((1, W), F32))

    def bwd(t, carry):
        kb = n_blocks - 1 - t
        r0 = pl.multiple_of(kb * tb, tb)
        xc = xc_s[pl.ds(r0, tb), :]
        a, b = gates(xc, 1)
        a, b = scan_block(a, b, True)
        h = a * carry + b
        o_ref[pl.ds(r0, tb), :] = (h_s[pl.ds(r0, tb), :] + h) * _gelu_tanh(gt_ref[pl.ds(r0, tb), :])
        return h[0:1, :]

    lax.fori_loop(0, n_blocks, bwd, jnp.zeros((1, W), F32))


def _lru(lx, lgt, conv_w, conv_b, gate_w, gate_b, lam, B, S, tb):
    W = LRU_W
    nb = W // HEAD_DIM
    eye = jnp.eye(nb, dtype=F32)
    gw = jnp.einsum('dgnkm,nj->nkdgjm', gate_w, eye).reshape(W, 4 * W).astype(BF16)
    gb = gate_b.reshape(1, 4 * W)
    seq = pl.BlockSpec((S, W), lambda b: (b, 0))
    const = lambda shape: pl.BlockSpec(shape, lambda b: (0,) * len(shape))
    return pl.pallas_call(
        functools.partial(_lru_body, n_blocks=S // tb, tb=tb),
        grid=(B,),
        in_specs=[seq, seq, const((LRU_CONV, W)), const((1, W)), const((W, 4 * W)),
                  const((1, 4 * W)), const((2, W))],
        out_specs=seq,
        out_shape=jax.ShapeDtypeStruct((B * S, W), F32),
        scratch_shapes=[pltpu.VMEM((S, W), F32), pltpu.VMEM((S, W), F32)],
        compiler_params=_params("parallel"),
        name="rglru",
    )(lx, lgt, conv_w, conv_b.reshape(1, W), gw, gb, lam)


def _attn_prep_body(q_ref, k_ref, v_ref, cos_ref, sin_ref, qnw_ref, knw_ref, rep_ref, rept_ref,
                    qo_ref, kt_ref, vo_ref):
    def head_norm_rot(x, w, cs, sn):
        W = x.shape[-1]
        lane = lax.broadcasted_iota(jnp.int32, (1, W), 1)
        same_head = (lax.broadcasted_iota(jnp.int32, (W, W), 0) // HEAD_DIM
                     == lax.broadcasted_iota(jnp.int32, (W, W), 1) // HEAD_DIM)
        ms = _group_sum(x * x, same_head.astype(BF16)) * (1.0 / HEAD_DIM)
        xn = x * lax.rsqrt(ms + EPS) * w
        quarter = HEAD_DIM // 4
        rot = jnp.where((lane % (2 * quarter)) < quarter, -pltpu.roll(xn, W - quarter, 1),
                        pltpu.roll(xn, quarter, 1))
        return xn * cs + rot * sn

    cs = cos_ref[...]
    sn = sin_ref[...]
    q = head_norm_rot(q_ref[...], qnw_ref[...], cs, sn)
    qo_ref[...] = (q * (HEAD_DIM ** -0.5 * LOG2_E)).astype(BF16)
    k = head_norm_rot(k_ref[...], knw_ref[...], cs[:, :KV_W], sn[:, :KV_W]).astype(BF16)
    kt_ref[...] = _nt_dot(rept_ref[...], k).astype(BF16)
    vo_ref[...] = jnp.dot(v_ref[...].astype(BF16), rep_ref[...],
                          preferred_element_type=F32).astype(BF16)


def _attn_prep(aq, ak, av, cos, sin, qn_w, kn_w, B, S, tm):
    T = B * S
    ns = S // tm
    GW = ATT_GROUP * HEAD_DIM
    src = jnp.arange(ATT_KV_HEADS * GW)
    src = (src // GW) * HEAD_DIM + src % HEAD_DIM
    rep = (jnp.arange(KV_W)[:, None] == src[None, :]).astype(BF16)
    return pl.pallas_call(
        _attn_prep_body,
        grid=(B, ns),
        in_specs=[pl.BlockSpec((tm, ATT_W), lambda b, s: (b * ns + s, 0)),
                  pl.BlockSpec((tm, KV_W), lambda b, s: (b * ns + s, 0)),
                  pl.BlockSpec((tm, KV_W), lambda b, s: (b * ns + s, 0)),
                  pl.BlockSpec((tm, ATT_W), lambda b, s: (s, 0)),
                  pl.BlockSpec((tm, ATT_W), lambda b, s: (s, 0)),
                  pl.BlockSpec((1, ATT_W), lambda b, s: (0, 0)),
                  pl.BlockSpec((1, KV_W), lambda b, s: (0, 0)),
                  pl.BlockSpec((KV_W, ATT_KV_HEADS * GW), lambda b, s: (0, 0)),
                  pl.BlockSpec((ATT_KV_HEADS * GW, KV_W), lambda b, s: (0, 0))],
        out_specs=[pl.BlockSpec((tm, ATT_W), lambda b, s: (b * ns + s, 0)),
                   pl.BlockSpec((None, ATT_KV_HEADS * GW, tm), lambda b, s: (b, 0, s)),
                   pl.BlockSpec((tm, ATT_KV_HEADS * GW), lambda b, s: (b * ns + s, 0))],
        out_shape=[jax.ShapeDtypeStruct((T, ATT_W), BF16),
                   jax.ShapeDtypeStruct((B, ATT_KV_HEADS * GW, S), BF16),
                   jax.ShapeDtypeStruct((T, ATT_KV_HEADS * GW), BF16)],
        compiler_params=_params("parallel", "parallel"),
        name="attn_prep",
    )(aq, ak, av, cos, sin, jnp.tile(qn_w, ATT_HEADS).reshape(1, ATT_W),
      jnp.tile(kn_w, ATT_KV_HEADS).reshape(1, KV_W), rep, rep.T)


def _attn_body(q_ref, kt_ref, v_ref, o_ref, *, tq, tk, nk):
    GW = ATT_GROUP * HEAD_DIM
    lane = lax.broadcasted_iota(jnp.int32, (1, GW), 1)
    masks = [(lane // HEAD_DIM) == g for g in range(ATT_GROUP)]
    q = q_ref[...]
    zero = jnp.zeros_like(q)
    qs = jnp.concatenate([jnp.where(m, q, zero) for m in masks], axis=0)
    rows = ATT_GROUP * tq
    m_run = jnp.full((rows, 1), NEG_INF, F32)
    l_run = jnp.zeros((rows, 1), F32)
    acc = jnp.zeros((rows, GW), F32)
    for c in range(nk):
        s = jnp.dot(qs, kt_ref[:, c * tk:(c + 1) * tk], preferred_element_type=F32)
        m_new = jnp.maximum(m_run, jnp.max(s, axis=-1, keepdims=True))
        alpha = jnp.exp2(m_run - m_new)
        p = jnp.exp2(s - m_new)
        l_run = alpha * l_run + jnp.sum(p, axis=-1, keepdims=True)
        acc = alpha * acc + jnp.dot(p.astype(BF16), v_ref[c * tk:(c + 1) * tk, :],
                                    preferred_element_type=F32)
        m_run = m_new
    on = acc * (1.0 / l_run)
    o = jnp.zeros((tq, GW), F32)
    for g in range(ATT_GROUP):
        o = o + jnp.where(masks[g], on[g * tq:(g + 1) * tq], 0.0)
    o_ref[...] = o


def _attention(qp, kt, vp, B, S, tq, tk):
    T = B * S
    nq = S // tq
    GW = ATT_GROUP * HEAD_DIM
    return pl.pallas_call(
        functools.partial(_attn_body, tq=tq, tk=tk, nk=S // tk),
        grid=(B, ATT_KV_HEADS, nq),
        in_specs=[pl.BlockSpec((tq, GW), lambda b, h, i: (b * nq + i, h)),
                  pl.BlockSpec((None, GW, S), lambda b, h, i: (b, h, 0)),
                  pl.BlockSpec((S, GW), lambda b, h, i: (b, h))],
        out_specs=pl.BlockSpec((tq, GW), lambda b, h, i: (b * nq + i, h)),
        out_shape=jax.ShapeDtypeStruct((T, ATT_W), F32),
        compiler_params=_params("parallel", "parallel", "parallel"),
        name="attention",
    )(qp, kt, vp)


def _outproj_body(x_ref, oret_ref, olru_ref, oatt_ref, w_ref, ln2_ref, wq_ref,
                  xo_ref, xn_ref, q_ref):
    y = jnp.dot(oret_ref[...].astype(BF16), w_ref[0:RET_W, :], preferred_element_type=F32)
    y = y + jnp.dot(olru_ref[...].astype(BF16), w_ref[RET_W:RET_W + LRU_W, :],
                    preferred_element_type=F32)
    y = y + jnp.dot(oatt_ref[...].astype(BF16), w_ref[RET_W + LRU_W:, :],
                    preferred_element_type=F32)
    x = x_ref[...] + y
    xo_ref[...] = x
    xn = _rms(x, ln2_ref[...]).astype(BF16)
    xn_ref[...] = xn.astype(F32).T.astype(BF16)
    q_ref[...] = jnp.dot(xn, wq_ref[...], preferred_element_type=F32)


def _outproj(x2, o_ret, o_lru, o_att, w_out_bf, ln2, wq_bf, tm):
    T, D = x2.shape
    QW = wq_bf.shape[1]
    MW = w_out_bf.shape[0]
    tok = lambda w: pl.BlockSpec((tm, w), lambda i: (i, 0))
    return pl.pallas_call(
        _outproj_body,
        grid=(T // tm,),
        in_specs=[tok(D), tok(RET_W), tok(LRU_W), tok(ATT_W),
                  pl.BlockSpec((MW, D), lambda i: (0, 0)),
                  pl.BlockSpec((1, D), lambda i: (0, 0)),
                  pl.BlockSpec((D, QW), lambda i: (0, 0))],
        out_specs=[tok(D), pl.BlockSpec((D, tm), lambda i: (0, i)), tok(QW)],
        out_shape=[jax.ShapeDtypeStruct((T, D), F32), jax.ShapeDtypeStruct((D, T), BF16),
                   jax.ShapeDtypeStruct((T, QW), F32)],
        compiler_params=_params("parallel"),
        name="outproj",
    )(x2, o_ret, o_lru, o_att, w_out_bf, ln2, wq_bf)


def _sorting_network(n):
    pairs = []
    p = 1
    while p < n:
        k = p
        while k >= 1:
            for j in range(k % p, n - k, 2 * k):
                for i in range(min(k, n - j - k)):
                    if (i + j) // (2 * p) == (i + j + k) // (2 * p):
                        pairs.append((i + j, i + j + k))
            k //= 2
        p *= 2
    return pairs


def _pair_counts(n):
    return [n // (i + 1) for i in range(n)]


def _peer_route_body(q_ref, keys_ref, cnt_ref, w0_ref, rk_ref, w1_ref, a_s, b_s, f_s,
                     *, half, topk):
    s0 = _nt_dot(keys_ref[0], q_ref[:, :half])
    s1 = _nt_dot(keys_ref[1], q_ref[:, half:])

    def top_sorted(tiles, n, out_ref):
        rows = list(tiles) + [None] * (pl.next_power_of_2(len(tiles)) - len(tiles))
        for lo_i, hi_i in _sorting_network(len(rows)):
            x, y = rows[lo_i], rows[hi_i]
            if y is None:
                continue
            if x is None:
                rows[lo_i], rows[hi_i] = y, None
            else:
                rows[lo_i], rows[hi_i] = jnp.maximum(x, y), jnp.minimum(x, y)
        rows = [t for t in rows if t is not None]
        for r in range(n):
            m = jnp.max(rows[0], axis=0, keepdims=True)
            out_ref[r:r + 1, :] = m
            hit = rows[0] == m
            for v in range(min(len(rows), n - r - 1)):
                nxt = rows[v + 1] if v + 1 < len(rows) else NEG_INF
                rows[v] = jnp.where(hit, nxt, rows[v])

    def sublane_tiles(x):
        return [x[8 * v:8 * v + 8, :] for v in range(x.shape[0] // 8)]

    n = topk + 1
    top_sorted(sublane_tiles(s0), n, a_s)
    top_sorted(sublane_tiles(s1), n, b_s)
    a = a_s[0:topk, :]
    b = b_s[0:topk, :]
    counts = _pair_counts(n)
    row8 = lax.broadcasted_iota(jnp.int32, (8, 1), 0)
    parts = [a[0:1, :] + b[0:8, :], a[0:1, :] + b[8:16, :]]
    i = 1
    while counts[i] > 1:
        parts.append(jnp.where(row8 < counts[i], a[i:i + 1, :] + b[0:8, :], NEG_INF))
        i += 1
    parts.append(a[i:, :] + b[0:1, :])
    last = jnp.where(row8 == 0, a[0:1, :] + b_s[topk:n, :],
                     jnp.where(row8 == 1, a_s[topk:n, :] + b[0:1, :], NEG_INF))
    parts.append(last)
    top_sorted(parts, n, f_s)
    f = f_s[0:topk, :]
    z = jnp.sum(jnp.exp(f - f[0:1, :]), axis=0, keepdims=True)
    tau = 0.5 * (f[topk - 1:topk, :] + f_s[topk:n, :])
    thr = tau - s0
    cnt = jnp.zeros_like(thr)
    for r in range(topk):
        cnt = jnp.where(b[r:r + 1, :] >= thr, float(r + 1), cnt)
    cnt_ref[...] = cnt
    rank1 = jnp.zeros_like(s1)
    for r in range(n):
        rank1 = jnp.where(b_s[r:r + 1, :] > s1, float(r + 1), rank1)
    w0_ref[...] = jnp.exp(s0 - a[0:1, :]) * (0.5 / z)
    rk_ref[...] = rank1.astype(BF16)
    w1_ref[...] = jnp.exp(s1 - b[0:1, :]).astype(BF16)


def _peer_route(q, keys, tm):
    T = q.shape[0]
    H, _, NK, half = keys.shape
    assert _pair_counts(PEER_TOPK + 1)[7] > 1 >= _pair_counts(PEER_TOPK + 1)[8] and NK > PEER_TOPK
    assert PEER_TOPK == 2 * 8
    per_key = jax.ShapeDtypeStruct((H, NK, T), F32)
    per_col = jax.ShapeDtypeStruct((H, NK, T), BF16)
    ospec = pl.BlockSpec((None, NK, tm), lambda i, h: (h, 0, i))
    return pl.pallas_call(
        functools.partial(_peer_route_body, half=half, topk=PEER_TOPK),
        grid=(T // tm, H),
        in_specs=[pl.BlockSpec((tm, 2 * half), lambda i, h: (i, h)),
                  pl.BlockSpec((None, 2, NK, half), lambda i, h: (h, 0, 0, 0))],
        out_specs=[ospec, ospec, ospec, ospec],
        out_shape=[per_key, per_key, per_col, per_col],
        scratch_shapes=[pltpu.VMEM((PEER_TOPK + 8, tm), F32)] * 3,
        compiler_params=_params("parallel", "parallel"),
        name="peer_route",
    )(q, keys)


def _peer_mix_body(xnt_ref, xres_ref, u_ref, vt_ref, cnt_ref, w0_ref, rk_ref, w1_ref, o_ref,
                   h_s, gh_s, acc_s, *, tl):
    H, ti, tm = cnt_ref.shape
    NK = rk_ref.shape[1]
    j = pl.program_id(1)

    last = pl.num_programs(1) - 1
    cur = j % 2

    def fold():
        acc_s[...] += jnp.dot(vt_ref[...], gh_s[1 - cur], preferred_element_type=F32)

    def row_tile(row):
        tile = jnp.broadcast_to(row, (BF16_SUBLANES, row.shape[1])).astype(BF16)
        return pltpu.repeat(tile, NK // BF16_SUBLANES, axis=0)

    def build():
        for ii in range(ti):
            rs = slice(ii * NK, (ii + 1) * NK)
            for lc in range(tm // tl):
                ls = slice(lc * tl, (lc + 1) * tl)
                g = None
                for h in range(H):
                    c = row_tile(cnt_ref[h, ii:ii + 1, ls])
                    a = row_tile(w0_ref[h, ii:ii + 1, ls])
                    term = a * jnp.where(rk_ref[h, :, ls] < c, w1_ref[h, :, ls], 0)
                    g = term if g is None else g + term
                gh_s[cur, rs, ls] = g
        h_s[...] = jnp.dot(u_ref[...], xnt_ref[...], preferred_element_type=F32)
        for ii in range(ti):
            rs = slice(ii * NK, (ii + 1) * NK)
            gh_s[cur, rs, :] = gh_s[cur, rs, :] * _gelu_tanh_x2(h_s[rs, :].astype(BF16))

    @pl.when(j == 0)
    def _():
        acc_s[...] = jnp.zeros_like(acc_s)
        build()

    @pl.when(jnp.logical_and(j > 0, j < last))
    def _():
        fold()
        build()

    @pl.when(j == last)
    def _():
        fold()
        o_ref[...] = xres_ref[...] + acc_s[...].T


def _peer_mix(xnt, xres, u_bf, vt_bf, cnt, w0, rk, w1, tm, ti, tl):
    D, T = xnt.shape
    H, NK, _ = cnt.shape
    E = u_bf.shape[0]
    te = ti * NK
    n_e = E // te
    build = lambda j: jnp.minimum(j, n_e - 1)
    fold = lambda j: jnp.maximum(j - 1, 0)
    keys = pl.BlockSpec((H, ti, tm), lambda i, j: (0, build(j), i))
    full = pl.BlockSpec((H, NK, tm), lambda i, j: (0, 0, i))
    return pl.pallas_call(
        functools.partial(_peer_mix_body, tl=tl),
        grid=(T // tm, n_e + 1),
        in_specs=[pl.BlockSpec((D, tm), lambda i, j: (0, i)),
                  pl.BlockSpec((tm, D), lambda i, j: (i, 0)),
                  pl.BlockSpec((te, D), lambda i, j: (build(j), 0)),
                  pl.BlockSpec((D, te), lambda i, j: (0, fold(j))),
                  keys, keys, full, full],
        out_specs=pl.BlockSpec((tm, D), lambda i, j: (i, 0)),
        out_shape=jax.ShapeDtypeStruct((T, D), F32),
        scratch_shapes=[pltpu.VMEM((te, tm), F32), pltpu.VMEM((2, te, tm), BF16),
                        pltpu.VMEM((D, tm), F32)],
        compiler_params=_params("parallel", "arbitrary"),
        name="peer_mix",
    )(xnt, xres, u_bf, vt_bf, cnt, w0, rk, w1)


def _final_norm_body(x_ref, w_ref, o_ref):
    o_ref[...] = _rms(x_ref[...], w_ref[...])


def _final_norm(x2, w, tm):
    T, D = x2.shape
    return pl.pallas_call(
        _final_norm_body,
        grid=(T // tm,),
        in_specs=[pl.BlockSpec((tm, D), lambda i: (i, 0)), pl.BlockSpec((1, D), lambda i: (0, 0))],
        out_specs=pl.BlockSpec((tm, D), lambda i: (i, 0)),
        out_shape=jax.ShapeDtypeStruct((T, D), F32),
        compiler_params=_params("parallel"),
        name="final_norm",
    )(x2, w)


def _rope_table(pos, inv_freq):
    ang = pos[:, None] * inv_freq[None, :]
    ang = jnp.concatenate([ang, ang], axis=-1)
    return jnp.cos(ang), jnp.sin(ang)


def _position_tables(S):
    t = jnp.arange(S, dtype=F32)
    n_rows = S // GRID_W
    rows = jnp.repeat(jnp.arange(n_rows, dtype=F32), GRID_W)
    cols = jnp.tile(jnp.arange(GRID_W, dtype=F32), n_rows)
    ret_inv = 1.0 / (10000.0 ** jnp.linspace(0.0, 1.0, HEAD_DIM // 2, dtype=F32))
    ret_cos, ret_sin = _rope_table(t, ret_inv)
    ax_n = HEAD_DIM // 4
    ax_inv = ROPE_BASE ** (-jnp.arange(ax_n, dtype=F32) / ax_n)
    cr, sr = _rope_table(rows, ax_inv)
    cc, sc = _rope_table(cols, ax_inv)
    ax_cos = jnp.concatenate([cr, cc], axis=-1)
    ax_sin = jnp.concatenate([sr, sc], axis=-1)
    return (jnp.tile(ret_cos, (1, RET_HEADS)), jnp.tile(ret_sin, (1, RET_HEADS)),
            jnp.tile(ax_cos, (1, ATT_HEADS)), jnp.tile(ax_sin, (1, ATT_HEADS)))


def _tiles(B, S, n_keys):
    T = B * S
    return dict(
        tm=min(512, T),
        tb=min(256, S),
        tp=min(512, S),
        tq=min(256, S),
        tk=min(512, S),
        tr=min(512, T),
        tmix=min(512, T),
        ti=min(8, n_keys),
        tl=min(256, T),
    )


def kernel(x, ln1_w, w_in, ret_log_decay, ret_gn_w, lru_conv_w, lru_conv_b, lru_gate_w, lru_gate_b,
           lru_lambda, attn_q_norm, attn_k_norm, w_out, ln2_w, peer_wq, peer_keys, peer_u, peer_v,
           lnf_w):
    B, S, D = x.shape
    T = B * S
    depth = w_in.shape[0]
    n_keys = peer_keys.shape[3]
    tl = _tiles(B, S, n_keys)
    widths = (RET_W,) * 4 + (LRU_W,) * 2 + (ATT_W, KV_W, KV_W)
    ret_cos, ret_sin, ax_cos, ax_sin = _position_tables(S)
    x2 = x.reshape(T, D)
    for l in range(depth):
        rq, rk, rv, rg, lx, lgt, aq, ak, av = _inproj(
            x2, ln1_w[l].reshape(1, D), w_in[l].astype(BF16), widths, tl["tm"])
        o_ret = _retention(rq, rk, rv, rg, ret_cos, ret_sin, ret_log_decay[l],
                           ret_gn_w[l].reshape(1, RET_W), B, S)
        o_lru = _lru(lx, lgt, lru_conv_w[l], lru_conv_b[l], lru_gate_w[l], lru_gate_b[l],
                     lru_lambda[l], B, S, tl["tb"])
        qp, kt, vp = _attn_prep(aq, ak, av, ax_cos, ax_sin, attn_q_norm[l], attn_k_norm[l],
                                B, S, tl["tp"])
        o_att = _attention(qp, kt, vp, B, S, tl["tq"], tl["tk"])
        x2, xn, q = _outproj(x2, o_ret, o_lru, o_att, w_out[l].astype(BF16),
                             ln2_w[l].reshape(1, D), peer_wq[l].astype(BF16), tl["tm"])
        cnt, w0, rk, w1 = _peer_route(q, peer_keys[l], tl["tr"])
        x2 = _peer_mix(xn, x2, peer_u[l].astype(BF16), peer_v[l].T.astype(BF16),
                       cnt, w0, rk, w1, tl["tmix"], tl["ti"], tl["tl"])
    return _final_norm(x2, lnf_w.reshape(1, D), tl["tm"]).reshape(B, S, D)
```

```python
import functools

import jax
import jax.numpy as jnp
from jax import lax
from jax.experimental import pallas as pl
from jax.experimental.pallas import tpu as pltpu

F32 = jnp.float32
BF16 = jnp.bfloat16
EPS = 1e-6
LOG2_E = 1.4426950408889634

GRID_W = 64
CHUNK = 128
HEAD_DIM = 64
RET_HEADS = 4
RET_W = RET_HEADS * HEAD_DIM
LRU_W = 256
LRU_CONV = 4
LRU_C = 8.0
ATT_HEADS = 8
ATT_KV_HEADS = 2
ATT_GROUP = ATT_HEADS // ATT_KV_HEADS
ATT_W = ATT_HEADS * HEAD_DIM
KV_W = ATT_KV_HEADS * HEAD_DIM
ROPE_BASE = 10000.0
PEER_TOPK = 16
BF16_SUBLANES = 16

VMEM_LIMIT_BYTES = 56 * 1024 * 1024
NEG_INF = float("-inf")


def _params(*sem):
    return pltpu.CompilerParams(dimension_semantics=sem, vmem_limit_bytes=VMEM_LIMIT_BYTES)


def _rms(x, w):
    return x * lax.rsqrt(jnp.mean(x * x, axis=-1, keepdims=True) + EPS) * w


def _gelu_tanh(x):
    return 0.5 * x * (1.0 + jnp.tanh(0.7978845608028654 * (x + 0.044715 * (x * x * x))))


def _gelu_tanh_x2(x):
    return x + x * jnp.tanh(x * (0.7978845608028654 + (0.7978845608028654 * 0.044715) * (x * x)))


def _sigmoid(x):
    return 1.0 / (1.0 + jnp.exp(-x))


def _nt_dot(a, b, precision=None):
    return lax.dot_general(a, b, (((1,), (1,)), ((), ())), precision=precision,
                           preferred_element_type=F32)


def _group_sum(x, member):
    hi = x.astype(BF16)
    lo = (x - hi.astype(F32)).astype(BF16)
    return (jnp.dot(hi, member, preferred_element_type=F32)
            + jnp.dot(lo, member, preferred_element_type=F32))


def _tn_dot(a, b, precision=None):
    return lax.dot_general(a, b, (((0,), (0,)), ((), ())), precision=precision,
                           preferred_element_type=F32)


def _inproj_body(x_ref, lnw_ref, w_ref, *out_refs, widths):
    hb = _rms(x_ref[...], lnw_ref[...]).astype(BF16)
    off = 0
    for o_ref, wd in zip(out_refs, widths):
        o_ref[...] = jnp.dot(hb, w_ref[:, off:off + wd],
                             preferred_element_type=F32).astype(o_ref.dtype)
        off += wd


def _inproj(x2, lnw, w_bf, widths, tm):
    T, D = x2.shape
    n_in = w_bf.shape[1]
    return pl.pallas_call(
        functools.partial(_inproj_body, widths=widths),
        grid=(T // tm,),
        in_specs=[pl.BlockSpec((tm, D), lambda i: (i, 0)),
                  pl.BlockSpec((1, D), lambda i: (0, 0)),
                  pl.BlockSpec((D, n_in), lambda i: (0, 0))],
        out_specs=[pl.BlockSpec((tm, wd), lambda i: (i, 0)) for wd in widths],
        out_shape=[jax.ShapeDtypeStruct((T, wd), F32) for wd in widths],
        compiler_params=_params("parallel"),
        name="inproj",
    )(x2, lnw, w_bf)


def _ret_body(ld_ref, q_ref, k_ref, v_ref, g_ref, cos_ref, sin_ref, lgl_ref, gnw_ref, o_ref,
              qr_s, kr_s, st_s, d_s, *, n_chunks):
    C, W = CHUNK, RET_W
    lane = lax.broadcasted_iota(jnp.int32, (1, W), 1)
    first_half = (lane % HEAD_DIM) < (HEAD_DIM // 2)
    head_masks = [(lane // HEAD_DIM) == h for h in range(RET_HEADS)]
    same_head = (lax.broadcasted_iota(jnp.int32, (W, W), 0) // HEAD_DIM
                 == lax.broadcasted_iota(jnp.int32, (W, W), 1) // HEAD_DIM)
    bd = same_head.astype(F32)
    lgf = lgl_ref[0:1, :]
    lgb = lgl_ref[1:2, :]
    idx = lax.broadcasted_iota(jnp.int32, (C, 1), 0).astype(F32)
    decf = jnp.exp(lgf * float(C))
    decb = jnp.exp(lgb * float(C))

    diff = (lax.broadcasted_iota(jnp.int32, (C, C), 0)
            - lax.broadcasted_iota(jnp.int32, (C, C), 1)).astype(F32)
    for h in range(RET_HEADS):
        d_s[h] = jnp.where(diff >= 0.0,
                           jnp.exp(ld_ref[0, h] * jnp.maximum(diff, 0.0)),
                           jnp.exp(ld_ref[1, h] * jnp.maximum(-diff, 0.0)))

    def rot_half(x):
        return jnp.where(first_half, -pltpu.roll(x, W - HEAD_DIM // 2, 1),
                         pltpu.roll(x, HEAD_DIM // 2, 1))

    def prep(c, carry):
        r0 = pl.multiple_of(c * C, C)
        cs = cos_ref[pl.ds(r0, C), :]
        sn = sin_ref[pl.ds(r0, C), :]
        q = q_ref[pl.ds(r0, C), :]
        k = k_ref[pl.ds(r0, C), :]
        qr_s[pl.ds(r0, C), :] = q * cs + rot_half(q) * sn
        kr_s[pl.ds(r0, C), :] = (k * cs + rot_half(k) * sn) * (HEAD_DIM ** -0.5)
        return carry

    lax.fori_loop(0, n_chunks, prep, 0)

    st_s[...] = jnp.zeros((W, W), F32)

    def fwd(c, carry):
        r0 = pl.multiple_of(c * C, C)
        q = qr_s[pl.ds(r0, C), :]
        k = kr_s[pl.ds(r0, C), :]
        v = v_ref[pl.ds(r0, C), :]
        st = st_s[...]
        qw = q * jnp.exp(lgf * (idx + 1.0))
        o_ref[pl.ds(r0, C), :] = jnp.dot(qw.astype(BF16), (st * bd).astype(BF16),
                                         preferred_element_type=F32)
        kw = k * jnp.exp(lgf * (float(C) - 1.0 - idx))
        st_s[...] = st * decf + _tn_dot(kw.astype(BF16), v.astype(BF16))
        return carry

    lax.fori_loop(0, n_chunks, fwd, 0)

    st_s[...] = jnp.zeros((W, W), F32)

    def bwd(t, carry):
        c = n_chunks - 1 - t
        r0 = pl.multiple_of(c * C, C)
        q = qr_s[pl.ds(r0, C), :]
        k = kr_s[pl.ds(r0, C), :]
        v = v_ref[pl.ds(r0, C), :]
        st = st_s[...]
        qw = q * jnp.exp(lgb * (float(C) - idx))
        vb = v.astype(BF16)
        kb = k.astype(BF16)
        o = o_ref[pl.ds(r0, C), :] + jnp.dot(qw.astype(BF16), (st * bd).astype(BF16),
                                              preferred_element_type=F32)
        kw = k * jnp.exp(lgb * idx)
        st_s[...] = st * decb + _tn_dot(kw.astype(BF16), vb)
        for h in range(RET_HEADS):
            qm = jnp.where(head_masks[h], q, 0.0).astype(BF16)
            s = _nt_dot(qm, kb) * d_s[h]
            oh = jnp.dot(s.astype(BF16), vb, preferred_element_type=F32)
            o = o + jnp.where(head_masks[h], oh, 0.0)
        ms = _group_sum(o * o, bd.astype(BF16)) * (1.0 / HEAD_DIM)
        on = o * lax.rsqrt(ms + EPS) * gnw_ref[...]
        g = g_ref[pl.ds(r0, C), :]
        o_ref[pl.ds(r0, C), :] = (g * _sigmoid(g)) * on
        return carry

    lax.fori_loop(0, n_chunks, bwd, 0)


def _retention(rq, rk, rv, rg, cos, sin, log_decay, gn_w, B, S):
    W = RET_W
    lgl = jnp.repeat(log_decay, HEAD_DIM, axis=1)
    seq = pl.BlockSpec((S, W), lambda b: (b, 0), pipeline_mode=pl.Buffered(1))
    tab = pl.BlockSpec((S, W), lambda b: (0, 0), pipeline_mode=pl.Buffered(1))
    return pl.pallas_call(
        functools.partial(_ret_body, n_chunks=S // CHUNK),
        grid=(B,),
        in_specs=[pl.BlockSpec(memory_space=pltpu.SMEM), seq, seq, seq, seq, tab, tab,
                  pl.BlockSpec((2, W), lambda b: (0, 0)),
                  pl.BlockSpec((1, W), lambda b: (0, 0))],
        out_specs=pl.BlockSpec((S, W), lambda b: (b, 0)),
        out_shape=jax.ShapeDtypeStruct((B * S, W), F32),
        scratch_shapes=[pltpu.VMEM((S, W), F32), pltpu.VMEM((S, W), F32),
                        pltpu.VMEM((W, W), F32), pltpu.VMEM((RET_HEADS, CHUNK, CHUNK), F32)],
        compiler_params=_params("parallel"),
        name="retention",
    )(log_decay, rq, rk, rv, rg, cos, sin, lgl, gn_w)


def _lru_body(x_ref, gt_ref, cw_ref, cb_ref, gw_ref, gb_ref, lam_ref, o_ref, xc_s, h_s,
              *, n_blocks, tb):
    W = LRU_W
    S = n_blocks * tb
    z = -lam_ref[...]
    softplus = jnp.maximum(z, 0.0) + jnp.log(1.0 + jnp.exp(-jnp.abs(z)))
    cl = -LRU_C * softplus
    row = lax.broadcasted_iota(jnp.int32, (tb, 1), 0)
    left = LRU_CONV // 2

    def conv_block(kb):
        r0 = pl.multiple_of(kb * tb, tb)
        prev = x_ref[pl.ds(pl.multiple_of(jnp.maximum(r0 - 8, 0), 8), 8), :]
        nxt = x_ref[pl.ds(pl.multiple_of(jnp.minimum(r0 + tb, S - 8), 8), 8), :]
        ext = jnp.concatenate([prev, x_ref[pl.ds(r0, tb), :], nxt], axis=0)
        t = row + r0
        acc = cb_ref[...] + jnp.zeros((tb, W), F32)
        for j in range(LRU_CONV):
            off = j - left
            if off == 0:
                xs = ext[8:8 + tb]
            else:
                xs = pltpu.roll(ext, (-off) % (tb + 16), 0)[8:8 + tb]
                xs = jnp.where((t + off >= 0) & (t + off < S), xs, 0.0)
            acc = acc + xs * cw_ref[j:j + 1, :]
        return acc

    def gates(xc, d):
        g = jnp.dot(xc.astype(BF16), gw_ref[:, d * 2 * W:(d + 1) * 2 * W],
                    preferred_element_type=F32) + gb_ref[:, d * 2 * W:(d + 1) * 2 * W]
        r = _sigmoid(g[:, :W])
        i = _sigmoid(g[:, W:])
        log_a = cl[d:d + 1, :] * r
        a = jnp.exp(log_a)
        b = jnp.sqrt(-jnp.tanh(log_a) * (a * a + 1.0)) * (i * xc)
        return a, b

    def scan_block(a, b, reverse):
        d = 1
        while d < tb:
            if reverse:
                keep = row < tb - d
                a_sh = jnp.where(keep, pltpu.roll(a, tb - d, 0), 1.0)
                b_sh = jnp.where(keep, pltpu.roll(b, tb - d, 0), 0.0)
            else:
                keep = row >= d
                a_sh = jnp.where(keep, pltpu.roll(a, d, 0), 1.0)
                b_sh = jnp.where(keep, pltpu.roll(b, d, 0), 0.0)
            b = a * b_sh + b
            a = a * a_sh
            d *= 2
        return a, b

    def fwd(kb, carry):
        r0 = pl.multiple_of(kb * tb, tb)
        xc = conv_block(kb)
        xc_s[pl.ds(r0, tb), :] = xc
        a, b = gates(xc, 0)
        a, b = scan_block(a, b, False)
        h = a * carry + b
        h_s[pl.ds(r0, tb), :] = h
        return h[tb - 1:tb, :]

    lax.fori_loop(0, n_blocks, fwd, jnp.zeros((1, W), F32))

    def bwd(t, carry):
        kb = n_blocks - 1 - t
        r0 = pl.multiple_of(kb * tb, tb)
        xc = xc_s[pl.ds(r0, tb), :]
        a, b = gates(xc, 1)
        a, b = scan_block(a, b, True)
        h = a * carry + b
        o_ref[pl.ds(r0, tb), :] = (h_s[pl.ds(r0, tb), :] + h) * _gelu_tanh(gt_ref[pl.ds(r0, tb), :])
        return h[0:1, :]

    lax.fori_loop(0, n_blocks, bwd, jnp.zeros((1, W), F32))


def _lru(lx, lgt, conv_w, conv_b, gate_w, gate_b, lam, B, S, tb):
    W = LRU_W
    nb = W // HEAD_DIM
    eye = jnp.eye(nb, dtype=F32)
    gw = jnp.einsum('dgnkm,nj->nkdgjm', gate_w, eye).reshape(W, 4 * W).astype(BF16)
    gb = gate_b.reshape(1, 4 * W)
    seq = pl.BlockSpec((S, W), lambda b: (b, 0))
    const = lambda shape: pl.BlockSpec(shape, lambda b: (0,) * len(shape))
    return pl.pallas_call(
        functools.partial(_lru_body, n_blocks=S // tb, tb=tb),
        grid=(B,),
        in_specs=[seq, seq, const((LRU_CONV, W)), const((1, W)), const((W, 4 * W)),
                  const((1, 4 * W)), const((2, W))],
        out_specs=seq,
        out_shape=jax.ShapeDtypeStruct((B * S, W), F32),
        scratch_shapes=[pltpu.VMEM((S, W), F32), pltpu.VMEM((S, W), F32)],
        compiler_params=_params("parallel"),
        name="rglru",
    )(lx, lgt, conv_w, conv_b.reshape(1, W), gw, gb, lam)


def _attn_prep_body(q_ref, k_ref, v_ref, cos_ref, sin_ref, qnw_ref, knw_ref, rep_ref, rept_ref,
                    qo_ref, kt_ref, vo_ref):
    def head_norm_rot(x, w, cs, sn):
        W = x.shape[-1]
        lane = lax.broadcasted_iota(jnp.int32, (1, W), 1)
        same_head = (lax.broadcasted_iota(jnp.int32, (W, W), 0) // HEAD_DIM
                     == lax.broadcasted_iota(jnp.int32, (W, W), 1) // HEAD_DIM)
        ms = _group_sum(x * x, same_head.astype(BF16)) * (1.0 / HEAD_DIM)
        xn = x * lax.rsqrt(ms + EPS) * w
        quarter = HEAD_DIM // 4
        rot = jnp.where((lane % (2 * quarter)) < quarter, -pltpu.roll(xn, W - quarter, 1),
                        pltpu.roll(xn, quarter, 1))
        return xn * cs + rot * sn

    cs = cos_ref[...]
    sn = sin_ref[...]
    q = head_norm_rot(q_ref[...], qnw_ref[...], cs, sn)
    qo_ref[...] = (q * (HEAD_DIM ** -0.5 * LOG2_E)).astype(BF16)
    k = head_norm_rot(k_ref[...], knw_ref[...], cs[:, :KV_W], sn[:, :KV_W]).astype(BF16)
    kt_ref[...] = _nt_dot(rept_ref[...], k).astype(BF16)
    vo_ref[...] = jnp.dot(v_ref[...].astype(BF16), rep_ref[...],
                          preferred_element_type=F32).astype(BF16)


def _attn_prep(aq, ak, av, cos, sin, qn_w, kn_w, B, S, tm):
    T = B * S
    ns = S // tm
    GW = ATT_GROUP * HEAD_DIM
    src = jnp.arange(ATT_KV_HEADS * GW)
    src = (src // GW) * HEAD_DIM + src % HEAD_DIM
    rep = (jnp.arange(KV_W)[:, None] == src[None, :]).astype(BF16)
    return pl.pallas_call(
        _attn_prep_body,
        grid=(B, ns),
        in_specs=[pl.BlockSpec((tm, ATT_W), lambda b, s: (b * ns + s, 0)),
                  pl.BlockSpec((tm, KV_W), lambda b, s: (b * ns + s, 0)),
                  pl.BlockSpec((tm, KV_W), lambda b, s: (b * ns + s, 0)),
                  pl.BlockSpec((tm, ATT_W), lambda b, s: (s, 0)),
                  pl.BlockSpec((tm, ATT_W), lambda b, s: (s, 0)),
                  pl.BlockSpec((1, ATT_W), lambda b, s: (0, 0)),
                  pl.BlockSpec((1, KV_W), lambda b, s: (0, 0)),
                  pl.BlockSpec((KV_W, ATT_KV_HEADS * GW), lambda b, s: (0, 0)),
                  pl.BlockSpec((ATT_KV_HEADS * GW, KV_W), lambda b, s: (0, 0))],
        out_specs=[pl.BlockSpec((tm, ATT_W), lambda b, s: (b * ns + s, 0)),
                   pl.BlockSpec((None, ATT_KV_HEADS * GW, tm), lambda b, s: (b, 0, s)),
                   pl.BlockSpec((tm, ATT_KV_HEADS * GW), lambda b, s: (b * ns + s, 0))],
        out_shape=[jax.ShapeDtypeStruct((T, ATT_W), BF16),
                   jax.ShapeDtypeStruct((B, ATT_KV_HEADS * GW, S), BF16),
                   jax.ShapeDtypeStruct((T, ATT_KV_HEADS * GW), BF16)],
        compiler_params=_params("parallel", "parallel"),
        name="attn_prep",
    )(aq, ak, av, cos, sin, jnp.tile(qn_w, ATT_HEADS).reshape(1, ATT_W),
      jnp.tile(kn_w, ATT_KV_HEADS).reshape(1, KV_W), rep, rep.T)


def _attn_body(q_ref, kt_ref, v_ref, o_ref, *, tq, tk, nk):
    GW = ATT_GROUP * HEAD_DIM
    lane = lax.broadcasted_iota(jnp.int32, (1, GW), 1)
    masks = [(lane // HEAD_DIM) == g for g in range(ATT_GROUP)]
    q = q_ref[...]
    zero = jnp.zeros_like(q)
    qs = jnp.concatenate([jnp.where(m, q, zero) for m in masks], axis=0)
    rows = ATT_GROUP * tq
    m_run = jnp.full((rows, 1), NEG_INF, F32)
    l_run = jnp.zeros((rows, 1), F32)
    acc = jnp.zeros((rows, GW), F32)
    for c in range(nk):
        s = jnp.dot(qs, kt_ref[:, c * tk:(c + 1) * tk], preferred_element_type=F32)
        m_new = jnp.maximum(m_run, jnp.max(s, axis=-1, keepdims=True))
        alpha = jnp.exp2(m_run - m_new)
        p = jnp.exp2(s - m_new)
        l_run = alpha * l_run + jnp.sum(p, axis=-1, keepdims=True)
        acc = alpha * acc + jnp.dot(p.astype(BF16), v_ref[c * tk:(c + 1) * tk, :],
                                    preferred_element_type=F32)
        m_run = m_new
    on = acc * (1.0 / l_run)
    o = jnp.zeros((tq, GW), F32)
    for g in range(ATT_GROUP):
        o = o + jnp.where(masks[g], on[g * tq:(g + 1) * tq], 0.0)
    o_ref[...] = o


def _attention(qp, kt, vp, B, S, tq, tk):
    T = B * S
    nq = S // tq
    GW = ATT_GROUP * HEAD_DIM
    return pl.pallas_call(
        functools.partial(_attn_body, tq=tq, tk=tk, nk=S // tk),
        grid=(B, ATT_KV_HEADS, nq),
        in_specs=[pl.BlockSpec((tq, GW), lambda b, h, i: (b * nq + i, h)),
                  pl.BlockSpec((None, GW, S), lambda b, h, i: (b, h, 0)),
                  pl.BlockSpec((S, GW), lambda b, h, i: (b, h))],
        out_specs=pl.BlockSpec((tq, GW), lambda b, h, i: (b * nq + i, h)),
        out_shape=jax.ShapeDtypeStruct((T, ATT_W), F32),
        compiler_params=_params("parallel", "parallel", "parallel"),
        name="attention",
    )(qp, kt, vp)


def _outproj_body(x_ref, oret_ref, olru_ref, oatt_ref, w_ref, ln2_ref, wq_ref,
                  xo_ref, xn_ref, q_ref):
    y = jnp.dot(oret_ref[...].astype(BF16), w_ref[0:RET_W, :], preferred_element_type=F32)
    y = y + jnp.dot(olru_ref[...].astype(BF16), w_ref[RET_W:RET_W + LRU_W, :],
                    preferred_element_type=F32)
    y = y + jnp.dot(oatt_ref[...].astype(BF16), w_ref[RET_W + LRU_W:, :],
                    preferred_element_type=F32)
    x = x_ref[...] + y
    xo_ref[...] = x
    xn = _rms(x, ln2_ref[...]).astype(BF16)
    xn_ref[...] = xn.astype(F32).T.astype(BF16)
    q_ref[...] = jnp.dot(xn, wq_ref[...], preferred_element_type=F32)


def _outproj(x2, o_ret, o_lru, o_att, w_out_bf, ln2, wq_bf, tm):
    T, D = x2.shape
    QW = wq_bf.shape[1]
    MW = w_out_bf.shape[0]
    tok = lambda w: pl.BlockSpec((tm, w), lambda i: (i, 0))
    return pl.pallas_call(
        _outproj_body,
        grid=(T // tm,),
        in_specs=[tok(D), tok(RET_W), tok(LRU_W), tok(ATT_W),
                  pl.BlockSpec((MW, D), lambda i: (0, 0)),
                  pl.BlockSpec((1, D), lambda i: (0, 0)),
                  pl.BlockSpec((D, QW), lambda i: (0, 0))],
        out_specs=[tok(D), pl.BlockSpec((D, tm), lambda i: (0, i)), tok(QW)],
        out_shape=[jax.ShapeDtypeStruct((T, D), F32), jax.ShapeDtypeStruct((D, T), BF16),
                   jax.ShapeDtypeStruct((T, QW), F32)],
        compiler_params=_params("parallel"),
        name="outproj",
    )(x2, o_ret, o_lru, o_att, w_out_bf, ln2, wq_bf)


def _sorting_network(n):
    pairs = []
    p = 1
    while p < n:
        k = p
        while k >= 1:
            for j in range(k % p, n - k, 2 * k):
                for i in range(min(k, n - j - k)):
                    if (i + j) // (2 * p) == (i + j + k) // (2 * p):
                        pairs.append((i + j, i + j + k))
            k //= 2
        p *= 2
    return pairs


def _pair_counts(n):
    return [n // (i + 1) for i in range(n)]


def _peer_route_body(q_ref, keys_ref, cnt_ref, w0_ref, rk_ref, w1_ref, a_s, b_s, f_s,
                     *, half, topk):
    s0 = _nt_dot(keys_ref[0], q_ref[:, :half])
    s1 = _nt_dot(keys_ref[1], q_ref[:, half:])

    def top_sorted(tiles, n, out_ref):
        rows = list(tiles) + [None] * (pl.next_power_of_2(len(tiles)) - len(tiles))
        for lo_i, hi_i in _sorting_network(len(rows)):
            x, y = rows[lo_i], rows[hi_i]
            if y is None:
                continue
            if x is None:
                rows[lo_i], rows[hi_i] = y, None
            else:
                rows[lo_i], rows[hi_i] = jnp.maximum(x, y), jnp.minimum(x, y)
        rows = [t for t in rows if t is not None]
        for r in range(n):
            m = jnp.max(rows[0], axis=0, keepdims=True)
            out_ref[r:r + 1, :] = m
            hit = rows[0] == m
            for v in range(min(len(rows), n - r - 1)):
                nxt = rows[v + 1] if v + 1 < len(rows) else NEG_INF
                rows[v] = jnp.where(hit, nxt, rows[v])

    def sublane_tiles(x):
        return [x[8 * v:8 * v + 8, :] for v in range(x.shape[0] // 8)]

    n = topk + 1
    top_sorted(sublane_tiles(s0), n, a_s)
    top_sorted(sublane_tiles(s1), n, b_s)
    a = a_s[0:topk, :]
    b = b_s[0:topk, :]
    counts = _pair_counts(n)
    row8 = lax.broadcasted_iota(jnp.int32, (8, 1), 0)
    parts = [a[0:1, :] + b[0:8, :], a[0:1, :] + b[8:16, :]]
    i = 1
    while counts[i] > 1:
        parts.append(jnp.where(row8 < counts[i], a[i:i + 1, :] + b[0:8, :], NEG_INF))
        i += 1
    parts.append(a[i:, :] + b[0:1, :])
    last = jnp.where(row8 == 0, a[0:1, :] + b_s[topk:n, :],
                     jnp.where(row8 == 1, a_s[topk:n, :] + b[0:1, :], NEG_INF))
    parts.append(last)
    top_sorted(parts, n, f_s)
    f = f_s[0:topk, :]
    z = jnp.sum(jnp.exp(f - f[0:1, :]), axis=0, keepdims=True)
    tau = 0.5 * (f[topk - 1:topk, :] + f_s[topk:n, :])
    thr = tau - s0
    cnt = jnp.zeros_like(thr)
    for r in range(topk):
        cnt = jnp.where(b[r:r + 1, :] >= thr, float(r + 1), cnt)
    cnt_ref[...] = cnt
    rank1 = jnp.zeros_like(s1)
    for r in range(topk):
        rank1 = jnp.where(b[r:r + 1, :] > s1, float(r + 1), rank1)
    w0_ref[...] = jnp.exp(s0 - a[0:1, :]) * (0.5 / z)
    rk_ref[...] = rank1.astype(BF16)
    w1_ref[...] = jnp.exp(s1 - b[0:1, :]).astype(BF16)


def _peer_route(q, keys, tm):
    T = q.shape[0]
    H, _, NK, half = keys.shape
    assert _pair_counts(PEER_TOPK + 1)[7] > 1 >= _pair_counts(PEER_TOPK + 1)[8] and NK > PEER_TOPK
    assert PEER_TOPK == 2 * 8
    per_key = jax.ShapeDtypeStruct((H, NK, T), F32)
    per_col = jax.ShapeDtypeStruct((H, NK, T), BF16)
    ospec = pl.BlockSpec((None, NK, tm), lambda i, h: (h, 0, i))
    return pl.pallas_call(
        functools.partial(_peer_route_body, half=half, topk=PEER_TOPK),
        grid=(T // tm, H),
        in_specs=[pl.BlockSpec((tm, 2 * half), lambda i, h: (i, h)),
                  pl.BlockSpec((None, 2, NK, half), lambda i, h: (h, 0, 0, 0))],
        out_specs=[ospec, ospec, ospec, ospec],
        out_shape=[per_key, per_key, per_col, per_col],
        scratch_shapes=[pltpu.VMEM((PEER_TOPK + 8, tm), F32)] * 3,
        compiler_params=_params("parallel", "parallel"),
        name="peer_route",
    )(q, keys)


def _peer_mix_body(xnt_ref, xres_ref, u_ref, vt_ref, cnt_ref, w0_ref, rk_ref, w1_ref, o_ref,
                   h_s, gh_s, acc_s, *, tl):
    H, ti, tm = cnt_ref.shape
    NK = rk_ref.shape[1]
    j = pl.program_id(1)

    last = pl.num_programs(1) - 1
    cur = j % 2

    def fold():
        acc_s[...] += jnp.dot(vt_ref[...], gh_s[1 - cur], preferred_element_type=F32)

    def row_tile(row):
        tile = jnp.broadcast_to(row, (BF16_SUBLANES, row.shape[1])).astype(BF16)
        return pltpu.repeat(tile, NK // BF16_SUBLANES, axis=0)

    def build():
        for ii in range(ti):
            rs = slice(ii * NK, (ii + 1) * NK)
            for lc in range(tm // tl):
                ls = slice(lc * tl, (lc + 1) * tl)
                g = None
                for h in range(H):
                    c = row_tile(cnt_ref[h, ii:ii + 1, ls])
                    a = row_tile(w0_ref[h, ii:ii + 1, ls])
                    term = a * jnp.where(rk_ref[h, :, ls] < c, w1_ref[h, :, ls], 0)
                    g = term if g is None else g + term
                gh_s[cur, rs, ls] = g
        h_s[...] = jnp.dot(u_ref[...], xnt_ref[...], preferred_element_type=F32)
        for ii in range(ti):
            rs = slice(ii * NK, (ii + 1) * NK)
            gh_s[cur, rs, :] = gh_s[cur, rs, :] * _gelu_tanh_x2(h_s[rs, :].astype(BF16))

    @pl.when(j == 0)
    def _():
        acc_s[...] = jnp.zeros_like(acc_s)
        build()

    @pl.when(jnp.logical_and(j > 0, j < last))
    def _():
        fold()
        build()

    @pl.when(j == last)
    def _():
        fold()
        o_ref[...] = xres_ref[...] + acc_s[...].T


def _peer_mix(xnt, xres, u_bf, vt_bf, cnt, w0, rk, w1, tm, ti, tl):
    D, T = xnt.shape
    H, NK, _ = cnt.shape
    E = u_bf.shape[0]
    te = ti * NK
    n_e = E // te
    build = lambda j: jnp.minimum(j, n_e - 1)
    fold = lambda j: jnp.maximum(j - 1, 0)
    keys = pl.BlockSpec((H, ti, tm), lambda i, j: (0, build(j), i))
    full = pl.BlockSpec((H, NK, tm), lambda i, j: (0, 0, i))
    return pl.pallas_call(
        functools.partial(_peer_mix_body, tl=tl),
        grid=(T // tm, n_e + 1),
        in_specs=[pl.BlockSpec((D, tm), lambda i, j: (0, i)),
                  pl.BlockSpec((tm, D), lambda i, j: (i, 0)),
                  pl.BlockSpec((te, D), lambda i, j: (build(j), 0)),
                  pl.BlockSpec((D, te), lambda i, j: (0, fold(j))),
                  keys, keys, full, full],
        out_specs=pl.BlockSpec((tm, D), lambda i, j: (i, 0)),
        out_shape=jax.ShapeDtypeStruct((T, D), F32),
        scratch_shapes=[pltpu.VMEM((te, tm), F32), pltpu.VMEM((2, te, tm), BF16),
                        pltpu.VMEM((D, tm), F32)],
        compiler_params=_params("parallel", "arbitrary"),
        name="peer_mix",
    )(xnt, xres, u_bf, vt_bf, cnt, w0, rk, w1)


def _final_norm_body(x_ref, w_ref, o_ref):
    o_ref[...] = _rms(x_ref[...], w_ref[...])


def _final_norm(x2, w, tm):
    T, D = x2.shape
    return pl.pallas_call(
        _final_norm_body,
        grid=(T // tm,),
        in_specs=[pl.BlockSpec((tm, D), lambda i: (i, 0)), pl.BlockSpec((1, D), lambda i: (0, 0))],
        out_specs=pl.BlockSpec((tm, D), lambda i: (i, 0)),
        out_shape=jax.ShapeDtypeStruct((T, D), F32),
        compiler_params=_params("parallel"),
        name="final_norm",
    )(x2, w)


def _rope_table(pos, inv_freq):
    ang = pos[:, None] * inv_freq[None, :]
    ang = jnp.concatenate([ang, ang], axis=-1)
    return jnp.cos(ang), jnp.sin(ang)


def _position_tables(S):
    t = jnp.arange(S, dtype=F32)
    n_rows = S // GRID_W
    rows = jnp.repeat(jnp.arange(n_rows, dtype=F32), GRID_W)
    cols = jnp.tile(jnp.arange(GRID_W, dtype=F32), n_rows)
    ret_inv = 1.0 / (10000.0 ** jnp.linspace(0.0, 1.0, HEAD_DIM // 2, dtype=F32))
    ret_cos, ret_sin = _rope_table(t, ret_inv)
    ax_n = HEAD_DIM // 4
    ax_inv = ROPE_BASE ** (-jnp.arange(ax_n, dtype=F32) / ax_n)
    cr, sr = _rope_table(rows, ax_inv)
    cc, sc = _rope_table(cols, ax_inv)
    ax_cos = jnp.concatenate([cr, cc], axis=-1)
    ax_sin = jnp.concatenate([sr, sc], axis=-1)
    return (jnp.tile(ret_cos, (1, RET_HEADS)), jnp.tile(ret_sin, (1, RET_HEADS)),
            jnp.tile(ax_cos, (1, ATT_HEADS)), jnp.tile(ax_sin, (1, ATT_HEADS)))


def _tiles(B, S, n_keys):
    T = B * S
    return dict(
        tm=min(512, T),
        tb=min(256, S),
        tp=min(512, S),
        tq=min(256, S),
        tk=min(256, S),
        tr=min(512, T),
        tmix=min(512, T),
        ti=min(8, n_keys),
        tl=min(256, T),
    )


def kernel(x, ln1_w, w_in, ret_log_decay, ret_gn_w, lru_conv_w, lru_conv_b, lru_gate_w, lru_gate_b,
           lru_lambda, attn_q_norm, attn_k_norm, w_out, ln2_w, peer_wq, peer_keys, peer_u, peer_v,
           lnf_w):
    B, S, D = x.shape
    T = B * S
    depth = w_in.shape[0]
    n_keys = peer_keys.shape[3]
    tl = _tiles(B, S, n_keys)
    widths = (RET_W,) * 4 + (LRU_W,) * 2 + (ATT_W, KV_W, KV_W)
    ret_cos, ret_sin, ax_cos, ax_sin = _position_tables(S)
    x2 = x.reshape(T, D)
    for l in range(depth):
        rq, rk, rv, rg, lx, lgt, aq, ak, av = _inproj(
            x2, ln1_w[l].reshape(1, D), w_in[l].astype(BF16), widths, tl["tm"])
        o_ret = _retention(rq, rk, rv, rg, ret_cos, ret_sin, ret_log_decay[l],
                           ret_gn_w[l].reshape(1, RET_W), B, S)
        o_lru = _lru(lx, lgt, lru_conv_w[l], lru_conv_b[l], lru_gate_w[l], lru_gate_b[l],
                     lru_lambda[l], B, S, tl["tb"])
        qp, kt, vp = _attn_prep(aq, ak, av, ax_cos, ax_sin, attn_q_norm[l], attn_k_norm[l],
                                B, S, tl["tp"])
        o_att = _attention(qp, kt, vp, B, S, tl["tq"], tl["tk"])
        x2, xn, q = _outproj(x2, o_ret, o_lru, o_att, w_out[l].astype(BF16),
                             ln2_w[l].reshape(1, D), peer_wq[l].astype(BF16), tl["tm"])
        cnt, w0, rk, w1 = _peer_route(q, peer_keys[l], tl["tr"])
        x2 = _peer_mix(xn, x2, peer_u[l].astype(BF16), peer_v[l].T.astype(BF16),
                       cnt, w0, rk, w1, tl["tmix"], tl["ti"], tl["tl"])
    return _final_norm(x2, lnf_w.reshape(1, D), tl["tm"]).reshape(B, S, D)
```

```python
import functools

import jax
import jax.numpy as jnp
from jax import lax
from jax.experimental import pallas as pl
from jax.experimental.pallas import tpu as pltpu

F32 = jnp.float32
BF16 = jnp.bfloat16
EPS = 1e-6
LOG2_E = 1.4426950408889634

GRID_W = 64
CHUNK = 128
HEAD_DIM = 64
RET_HEADS = 4
RET_W = RET_HEADS * HEAD_DIM
LRU_W = 256
LRU_CONV = 4
LRU_C = 8.0
ATT_HEADS = 8
ATT_KV_HEADS = 2
ATT_GROUP = ATT_HEADS // ATT_KV_HEADS
ATT_W = ATT_HEADS * HEAD_DIM
KV_W = ATT_KV_HEADS * HEAD_DIM
ROPE_BASE = 10000.0
PEER_TOPK = 16
F32_SUBLANES = 8
BF16_SUBLANES = 16

VMEM_LIMIT_BYTES = 56 * 1024 * 1024
NEG_INF = float("-inf")


def _params(*sem):
    return pltpu.CompilerParams(dimension_semantics=sem, vmem_limit_bytes=VMEM_LIMIT_BYTES)


def _rms(x, w):
    return x * lax.rsqrt(jnp.mean(x * x, axis=-1, keepdims=True) + EPS) * w


def _gelu_tanh(x):
    return 0.5 * x * (1.0 + jnp.tanh(0.7978845608028654 * (x + 0.044715 * (x * x * x))))


def _gelu_tanh_x2(x):
    return x + x * jnp.tanh(x * (0.7978845608028654 + (0.7978845608028654 * 0.044715) * (x * x)))


def _sigmoid(x):
    return 1.0 / (1.0 + jnp.exp(-x))


def _nt_dot(a, b):
    return lax.dot_general(a, b, (((1,), (1,)), ((), ())), preferred_element_type=F32)


def _group_sum(x, member):
    hi = x.astype(BF16)
    lo = (x - hi.astype(F32)).astype(BF16)
    return (jnp.dot(hi, member, preferred_element_type=F32)
            + jnp.dot(lo, member, preferred_element_type=F32))


def _tn_dot(a, b):
    return lax.dot_general(a, b, (((0,), (0,)), ((), ())), preferred_element_type=F32)


def _inproj_body(x_ref, lnw_ref, w_ref, *out_refs, widths):
    hb = _rms(x_ref[...], lnw_ref[...]).astype(BF16)
    off = 0
    for o_ref, wd in zip(out_refs, widths):
        o_ref[...] = jnp.dot(hb, w_ref[:, off:off + wd],
                             preferred_element_type=F32).astype(o_ref.dtype)
        off += wd


def _inproj(x2, lnw, w_bf, widths, tm):
    T, D = x2.shape
    n_in = w_bf.shape[1]
    return pl.pallas_call(
        functools.partial(_inproj_body, widths=widths),
        grid=(T // tm,),
        in_specs=[pl.BlockSpec((tm, D), lambda i: (i, 0)),
                  pl.BlockSpec((1, D), lambda i: (0, 0)),
                  pl.BlockSpec((D, n_in), lambda i: (0, 0))],
        out_specs=[pl.BlockSpec((tm, wd), lambda i: (i, 0)) for wd in widths],
        out_shape=[jax.ShapeDtypeStruct((T, wd), F32) for wd in widths],
        compiler_params=_params("parallel"),
        name="inproj",
    )(x2, lnw, w_bf)


def _ret_body(ld_ref, q_ref, k_ref, v_ref, g_ref, cos_ref, sin_ref, lgl_ref, gnw_ref, o_ref,
              qr_s, kr_s, st_s, d_s, *, n_chunks):
    C, W = CHUNK, RET_W
    lane = lax.broadcasted_iota(jnp.int32, (1, W), 1)
    first_half = (lane % HEAD_DIM) < (HEAD_DIM // 2)
    head_masks = [(lane // HEAD_DIM) == h for h in range(RET_HEADS)]
    same_head = (lax.broadcasted_iota(jnp.int32, (W, W), 0) // HEAD_DIM
                 == lax.broadcasted_iota(jnp.int32, (W, W), 1) // HEAD_DIM)
    bd = same_head.astype(F32)
    lgf = lgl_ref[0:1, :]
    lgb = lgl_ref[1:2, :]
    idx = lax.broadcasted_iota(jnp.int32, (C, 1), 0).astype(F32)
    decf = jnp.exp(lgf * float(C))
    decb = jnp.exp(lgb * float(C))

    diff = (lax.broadcasted_iota(jnp.int32, (C, C), 0)
            - lax.broadcasted_iota(jnp.int32, (C, C), 1)).astype(F32)
    for h in range(RET_HEADS):
        d_s[h] = jnp.where(diff >= 0.0,
                           jnp.exp(ld_ref[0, h] * jnp.maximum(diff, 0.0)),
                           jnp.exp(ld_ref[1, h] * jnp.maximum(-diff, 0.0)))

    def rot_half(x):
        return jnp.where(first_half, -pltpu.roll(x, W - HEAD_DIM // 2, 1),
                         pltpu.roll(x, HEAD_DIM // 2, 1))

    def prep(c, carry):
        r0 = pl.multiple_of(c * C, C)
        cs = cos_ref[pl.ds(r0, C), :]
        sn = sin_ref[pl.ds(r0, C), :]
        q = q_ref[pl.ds(r0, C), :]
        k = k_ref[pl.ds(r0, C), :]
        qr_s[pl.ds(r0, C), :] = q * cs + rot_half(q) * sn
        kr_s[pl.ds(r0, C), :] = (k * cs + rot_half(k) * sn) * (HEAD_DIM ** -0.5)
        return carry

    lax.fori_loop(0, n_chunks, prep, 0)

    st_s[...] = jnp.zeros((W, W), F32)

    def fwd(c, carry):
        r0 = pl.multiple_of(c * C, C)
        q = qr_s[pl.ds(r0, C), :]
        k = kr_s[pl.ds(r0, C), :]
        v = v_ref[pl.ds(r0, C), :]
        st = st_s[...]
        qw = q * jnp.exp(lgf * (idx + 1.0))
        o_ref[pl.ds(r0, C), :] = jnp.dot(qw.astype(BF16), (st * bd).astype(BF16),
                                         preferred_element_type=F32)
        kw = k * jnp.exp(lgf * (float(C) - 1.0 - idx))
        st_s[...] = st * decf + _tn_dot(kw.astype(BF16), v.astype(BF16))
        return carry

    lax.fori_loop(0, n_chunks, fwd, 0)

    st_s[...] = jnp.zeros((W, W), F32)

    def bwd(t, carry):
        c = n_chunks - 1 - t
        r0 = pl.multiple_of(c * C, C)
        q = qr_s[pl.ds(r0, C), :]
        k = kr_s[pl.ds(r0, C), :]
        v = v_ref[pl.ds(r0, C), :]
        st = st_s[...]
        qw = q * jnp.exp(lgb * (float(C) - idx))
        vb = v.astype(BF16)
        kb = k.astype(BF16)
        o = o_ref[pl.ds(r0, C), :] + jnp.dot(qw.astype(BF16), (st * bd).astype(BF16),
                                              preferred_element_type=F32)
        kw = k * jnp.exp(lgb * idx)
        st_s[...] = st * decb + _tn_dot(kw.astype(BF16), vb)
        for h in range(RET_HEADS):
            qm = jnp.where(head_masks[h], q, 0.0).astype(BF16)
            s = _nt_dot(qm, kb) * d_s[h]
            oh = jnp.dot(s.astype(BF16), vb, preferred_element_type=F32)
            o = o + jnp.where(head_masks[h], oh, 0.0)
        ms = _group_sum(o * o, bd.astype(BF16)) * (1.0 / HEAD_DIM)
        on = o * lax.rsqrt(ms + EPS) * gnw_ref[...]
        g = g_ref[pl.ds(r0, C), :]
        o_ref[pl.ds(r0, C), :] = (g * _sigmoid(g)) * on
        return carry

    lax.fori_loop(0, n_chunks, bwd, 0)


def _retention(rq, rk, rv, rg, cos, sin, log_decay, gn_w, B, S):
    W = RET_W
    lgl = jnp.repeat(log_decay, HEAD_DIM, axis=1)
    seq = pl.BlockSpec((S, W), lambda b: (b, 0), pipeline_mode=pl.Buffered(1))
    tab = pl.BlockSpec((S, W), lambda b: (0, 0), pipeline_mode=pl.Buffered(1))
    return pl.pallas_call(
        functools.partial(_ret_body, n_chunks=S // CHUNK),
        grid=(B,),
        in_specs=[pl.BlockSpec(memory_space=pltpu.SMEM), seq, seq, seq, seq, tab, tab,
                  pl.BlockSpec((2, W), lambda b: (0, 0)),
                  pl.BlockSpec((1, W), lambda b: (0, 0))],
        out_specs=pl.BlockSpec((S, W), lambda b: (b, 0)),
        out_shape=jax.ShapeDtypeStruct((B * S, W), F32),
        scratch_shapes=[pltpu.VMEM((S, W), F32), pltpu.VMEM((S, W), F32),
                        pltpu.VMEM((W, W), F32), pltpu.VMEM((RET_HEADS, CHUNK, CHUNK), F32)],
        compiler_params=_params("parallel"),
        name="retention",
    )(log_decay, rq, rk, rv, rg, cos, sin, lgl, gn_w)


def _lru_body(x_ref, gt_ref, cw_ref, cb_ref, gw_ref, gb_ref, lam_ref, o_ref, xc_s, h_s,
              *, n_blocks, tb):
    W = LRU_W
    S = n_blocks * tb
    z = -lam_ref[...]
    softplus = jnp.maximum(z, 0.0) + jnp.log(1.0 + jnp.exp(-jnp.abs(z)))
    cl = -LRU_C * softplus
    row = lax.broadcasted_iota(jnp.int32, (tb, 1), 0)
    left = LRU_CONV // 2

    def conv_block(kb):
        r0 = pl.multiple_of(kb * tb, tb)
        halo = F32_SUBLANES
        prev = x_ref[pl.ds(pl.multiple_of(jnp.maximum(r0 - halo, 0), halo), halo), :]
        nxt = x_ref[pl.ds(pl.multiple_of(jnp.minimum(r0 + tb, S - halo), halo), halo), :]
        ext = jnp.concatenate([prev, x_ref[pl.ds(r0, tb), :], nxt], axis=0)
        t = row + r0
        acc = cb_ref[...] + jnp.zeros((tb, W), F32)
        for j in range(LRU_CONV):
            off = j - left
            if off == 0:
                xs = ext[halo:halo + tb]
            else:
                xs = pltpu.roll(ext, (-off) % (tb + 2 * halo), 0)[halo:halo + tb]
                xs = jnp.where((t + off >= 0) & (t + off < S), xs, 0.0)
            acc = acc + xs * cw_ref[j:j + 1, :]
        return acc

    def gates(xc, d):
        g = jnp.dot(xc.astype(BF16), gw_ref[:, d * 2 * W:(d + 1) * 2 * W],
                    preferred_element_type=F32) + gb_ref[:, d * 2 * W:(d + 1) * 2 * W]
        r = _sigmoid(g[:, :W])
        i = _sigmoid(g[:, W:])
        log_a = cl[d:d + 1, :] * r
        a = jnp.exp(log_a)
        b = jnp.sqrt(-jnp.tanh(log_a) * (a * a + 1.0)) * (i * xc)
        return a, b

    def scan_block(a, b, reverse):
        d = 1
        while d < tb:
            if reverse:
                keep = row < tb - d
                a_sh = jnp.where(keep, pltpu.roll(a, tb - d, 0), 1.0)
                b_sh = jnp.where(keep, pltpu.roll(b, tb - d, 0), 0.0)
            else:
                keep = row >= d
                a_sh = jnp.where(keep, pltpu.roll(a, d, 0), 1.0)
                b_sh = jnp.where(keep, pltpu.roll(b, d, 0), 0.0)
            b = a * b_sh + b
            a = a * a_sh
            d *= 2
        return a, b

    def fwd(kb, carry):
        r0 = pl.multiple_of(kb * tb, tb)
        xc = conv_block(kb)
        xc_s[pl.ds(r0, tb), :] = xc
        a, b = gates(xc, 0)
        a, b = scan_block(a, b, False)
        h = a * carry + b
        h_s[pl.ds(r0, tb), :] = h
        return h[tb - 1:tb, :]

    lax.fori_loop(0, n_blocks, fwd, jnp.zeros((1, W), F32))

    def bwd(t, carry):
        kb = n_blocks - 1 - t
        r0 = pl.multiple_of(kb * tb, tb)
        xc = xc_s[pl.ds(r0, tb), :]
        a, b = gates(xc, 1)
        a, b = scan_block(a, b, True)
        h = a * carry + b
        o_ref[pl.ds(r0, tb), :] = (h_s[pl.ds(r0, tb), :] + h) * _gelu_tanh(gt_ref[pl.ds(r0, tb), :])
        return h[0:1, :]

    lax.fori_loop(0, n_blocks, bwd, jnp.zeros((1, W), F32))


def _lru(lx, lgt, conv_w, conv_b, gate_w, gate_b, lam, B, S, tb):
    W = LRU_W
    nb = W // HEAD_DIM
    eye = jnp.eye(nb, dtype=F32)
    gw = jnp.einsum('dgnkm,nj->nkdgjm', gate_w, eye).reshape(W, 4 * W).astype(BF16)
    gb = gate_b.reshape(1, 4 * W)
    seq = pl.BlockSpec((S, W), lambda b: (b, 0))
    const = lambda shape: pl.BlockSpec(shape, lambda b: (0,) * len(shape))
    return pl.pallas_call(
        functools.partial(_lru_body, n_blocks=S // tb, tb=tb),
        grid=(B,),
        in_specs=[seq, seq, const((LRU_CONV, W)), const((1, W)), const((W, 4 * W)),
                  const((1, 4 * W)), const((2, W))],
        out_specs=seq,
        out_shape=jax.ShapeDtypeStruct((B * S, W), F32),
        scratch_shapes=[pltpu.VMEM((S, W), F32), pltpu.VMEM((S, W), F32)],
        compiler_params=_params("parallel"),
        name="rglru",
    )(lx, lgt, conv_w, conv_b.reshape(1, W), gw, gb, lam)


def _attn_prep_body(q_ref, k_ref, v_ref, cos_ref, sin_ref, qnw_ref, knw_ref, rep_ref, rept_ref,
                    qo_ref, kt_ref, vo_ref):
    def head_norm_rot(x, w, cs, sn):
        W = x.shape[-1]
        lane = lax.broadcasted_iota(jnp.int32, (1, W), 1)
        same_head = (lax.broadcasted_iota(jnp.int32, (W, W), 0) // HEAD_DIM
                     == lax.broadcasted_iota(jnp.int32, (W, W), 1) // HEAD_DIM)
        ms = _group_sum(x * x, same_head.astype(BF16)) * (1.0 / HEAD_DIM)
        xn = x * lax.rsqrt(ms + EPS) * w
        quarter = HEAD_DIM // 4
        rot = jnp.where((lane % (2 * quarter)) < quarter, -pltpu.roll(xn, W - quarter, 1),
                        pltpu.roll(xn, quarter, 1))
        return xn * cs + rot * sn

    cs = cos_ref[...]
    sn = sin_ref[...]
    q = head_norm_rot(q_ref[...], qnw_ref[...], cs, sn)
    qo_ref[...] = (q * (HEAD_DIM ** -0.5 * LOG2_E)).astype(BF16)
    k = head_norm_rot(k_ref[...], knw_ref[...], cs[:, :KV_W], sn[:, :KV_W]).astype(BF16)
    kt_ref[...] = _nt_dot(rept_ref[...], k).astype(BF16)
    vo_ref[...] = jnp.dot(v_ref[...].astype(BF16), rep_ref[...],
                          preferred_element_type=F32).astype(BF16)


def _attn_prep(aq, ak, av, cos, sin, qn_w, kn_w, B, S, tm):
    T = B * S
    ns = S // tm
    GW = ATT_GROUP * HEAD_DIM
    src = jnp.arange(ATT_KV_HEADS * GW)
    src = (src // GW) * HEAD_DIM + src % HEAD_DIM
    rep = (jnp.arange(KV_W)[:, None] == src[None, :]).astype(BF16)
    return pl.pallas_call(
        _attn_prep_body,
        grid=(B, ns),
        in_specs=[pl.BlockSpec((tm, ATT_W), lambda b, s: (b * ns + s, 0)),
                  pl.BlockSpec((tm, KV_W), lambda b, s: (b * ns + s, 0)),
                  pl.BlockSpec((tm, KV_W), lambda b, s: (b * ns + s, 0)),
                  pl.BlockSpec((tm, ATT_W), lambda b, s: (s, 0)),
                  pl.BlockSpec((tm, ATT_W), lambda b, s: (s, 0)),
                  pl.BlockSpec((1, ATT_W), lambda b, s: (0, 0)),
                  pl.BlockSpec((1, KV_W), lambda b, s: (0, 0)),
                  pl.BlockSpec((KV_W, ATT_KV_HEADS * GW), lambda b, s: (0, 0)),
                  pl.BlockSpec((ATT_KV_HEADS * GW, KV_W), lambda b, s: (0, 0))],
        out_specs=[pl.BlockSpec((tm, ATT_W), lambda b, s: (b * ns + s, 0)),
                   pl.BlockSpec((None, ATT_KV_HEADS * GW, tm), lambda b, s: (b, 0, s)),
                   pl.BlockSpec((tm, ATT_KV_HEADS * GW), lambda b, s: (b * ns + s, 0))],
        out_shape=[jax.ShapeDtypeStruct((T, ATT_W), BF16),
                   jax.ShapeDtypeStruct((B, ATT_KV_HEADS * GW, S), BF16),
                   jax.ShapeDtypeStruct((T, ATT_KV_HEADS * GW), BF16)],
        compiler_params=_params("parallel", "parallel"),
        name="attn_prep",
    )(aq, ak, av, cos, sin, jnp.tile(qn_w, ATT_HEADS).reshape(1, ATT_W),
      jnp.tile(kn_w, ATT_KV_HEADS).reshape(1, KV_W), rep, rep.T)


def _attn_body(q_ref, kt_ref, v_ref, o_ref, *, tq, tk, nk):
    GW = ATT_GROUP * HEAD_DIM
    lane = lax.broadcasted_iota(jnp.int32, (1, GW), 1)
    masks = [(lane // HEAD_DIM) == g for g in range(ATT_GROUP)]
    q = q_ref[...]
    zero = jnp.zeros_like(q)
    qs = jnp.concatenate([jnp.where(m, q, zero) for m in masks], axis=0)
    rows = ATT_GROUP * tq
    m_run = jnp.full((rows, 1), NEG_INF, F32)
    l_run = jnp.zeros((rows, 1), F32)
    acc = jnp.zeros((rows, GW), F32)
    for c in range(nk):
        s = jnp.dot(qs, kt_ref[:, c * tk:(c + 1) * tk], preferred_element_type=F32)
        m_new = jnp.maximum(m_run, jnp.max(s, axis=-1, keepdims=True))
        alpha = jnp.exp2(m_run - m_new)
        p = jnp.exp2(s - m_new)
        l_run = alpha * l_run + jnp.sum(p, axis=-1, keepdims=True)
        acc = alpha * acc + jnp.dot(p.astype(BF16), v_ref[c * tk:(c + 1) * tk, :],
                                    preferred_element_type=F32)
        m_run = m_new
    on = acc * (1.0 / l_run)
    o = jnp.zeros((tq, GW), F32)
    for g in range(ATT_GROUP):
        o = o + jnp.where(masks[g], on[g * tq:(g + 1) * tq], 0.0)
    o_ref[...] = o


def _attention(qp, kt, vp, B, S, tq, tk):
    T = B * S
    nq = S // tq
    GW = ATT_GROUP * HEAD_DIM
    return pl.pallas_call(
        functools.partial(_attn_body, tq=tq, tk=tk, nk=S // tk),
        grid=(B, ATT_KV_HEADS, nq),
        in_specs=[pl.BlockSpec((tq, GW), lambda b, h, i: (b * nq + i, h)),
                  pl.BlockSpec((None, GW, S), lambda b, h, i: (b, h, 0)),
                  pl.BlockSpec((S, GW), lambda b, h, i: (b, h))],
        out_specs=pl.BlockSpec((tq, GW), lambda b, h, i: (b * nq + i, h)),
        out_shape=jax.ShapeDtypeStruct((T, ATT_W), F32),
        compiler_params=_params("parallel", "parallel", "parallel"),
        name="attention",
    )(qp, kt, vp)


def _outproj_body(x_ref, oret_ref, olru_ref, oatt_ref, w_ref, ln2_ref, wq_ref,
                  xo_ref, xn_ref, q_ref):
    y = jnp.dot(oret_ref[...].astype(BF16), w_ref[0:RET_W, :], preferred_element_type=F32)
    y = y + jnp.dot(olru_ref[...].astype(BF16), w_ref[RET_W:RET_W + LRU_W, :],
                    preferred_element_type=F32)
    y = y + jnp.dot(oatt_ref[...].astype(BF16), w_ref[RET_W + LRU_W:, :],
                    preferred_element_type=F32)
    x = x_ref[...] + y
    xo_ref[...] = x
    xn = _rms(x, ln2_ref[...]).astype(BF16)
    xn_ref[...] = xn.astype(F32).T.astype(BF16)
    q_ref[...] = jnp.dot(xn, wq_ref[...], preferred_element_type=F32)


def _outproj(x2, o_ret, o_lru, o_att, w_out_bf, ln2, wq_bf, tm):
    T, D = x2.shape
    QW = wq_bf.shape[1]
    MW = w_out_bf.shape[0]
    tok = lambda w: pl.BlockSpec((tm, w), lambda i: (i, 0))
    return pl.pallas_call(
        _outproj_body,
        grid=(T // tm,),
        in_specs=[tok(D), tok(RET_W), tok(LRU_W), tok(ATT_W),
                  pl.BlockSpec((MW, D), lambda i: (0, 0)),
                  pl.BlockSpec((1, D), lambda i: (0, 0)),
                  pl.BlockSpec((D, QW), lambda i: (0, 0))],
        out_specs=[tok(D), pl.BlockSpec((D, tm), lambda i: (0, i)), tok(QW)],
        out_shape=[jax.ShapeDtypeStruct((T, D), F32), jax.ShapeDtypeStruct((D, T), BF16),
                   jax.ShapeDtypeStruct((T, QW), F32)],
        compiler_params=_params("parallel"),
        name="outproj",
    )(x2, o_ret, o_lru, o_att, w_out_bf, ln2, wq_bf)


def _sorting_network(n):
    pairs = []
    p = 1
    while p < n:
        k = p
        while k >= 1:
            for j in range(k % p, n - k, 2 * k):
                for i in range(min(k, n - j - k)):
                    if (i + j) // (2 * p) == (i + j + k) // (2 * p):
                        pairs.append((i + j, i + j + k))
            k //= 2
        p *= 2
    return pairs


def _pair_counts(n):
    return [n // (i + 1) for i in range(n)]


def _peer_route_body(q_ref, keys_ref, cnt_ref, w0_ref, rk_ref, w1_ref, a_s, b_s, f_s,
                     *, half, topk):
    s0 = _nt_dot(keys_ref[0], q_ref[:, :half])
    s1 = _nt_dot(keys_ref[1], q_ref[:, half:])

    def top_sorted(tiles, n, out_ref):
        rows = list(tiles) + [None] * (pl.next_power_of_2(len(tiles)) - len(tiles))
        for lo_i, hi_i in _sorting_network(len(rows)):
            x, y = rows[lo_i], rows[hi_i]
            if y is None:
                continue
            if x is None:
                rows[lo_i], rows[hi_i] = y, None
            else:
                rows[lo_i], rows[hi_i] = jnp.maximum(x, y), jnp.minimum(x, y)
        rows = [t for t in rows if t is not None]
        for r in range(n):
            m = jnp.max(rows[0], axis=0, keepdims=True)
            out_ref[r:r + 1, :] = m
            hit = rows[0] == m
            for v in range(min(len(rows), n - r - 1)):
                nxt = rows[v + 1] if v + 1 < len(rows) else NEG_INF
                rows[v] = jnp.where(hit, nxt, rows[v])

    def sublane_tiles(x):
        sl = F32_SUBLANES
        return [x[sl * v:sl * (v + 1), :] for v in range(x.shape[0] // sl)]

    n = topk + 1
    top_sorted(sublane_tiles(s0), n, a_s)
    top_sorted(sublane_tiles(s1), n, b_s)
    a = a_s[0:topk, :]
    b = b_s[0:topk, :]
    counts = _pair_counts(n)
    sl = F32_SUBLANES
    row8 = lax.broadcasted_iota(jnp.int32, (sl, 1), 0)
    parts = [a[0:1, :] + b[0:sl, :], a[0:1, :] + b[sl:2 * sl, :]]
    i = 1
    while counts[i] > 1:
        parts.append(jnp.where(row8 < counts[i], a[i:i + 1, :] + b[0:sl, :], NEG_INF))
        i += 1
    parts.append(a[i:, :] + b[0:1, :])
    last = jnp.where(row8 == 0, a[0:1, :] + b_s[topk:n, :],
                     jnp.where(row8 == 1, a_s[topk:n, :] + b[0:1, :], NEG_INF))
    parts.append(last)
    top_sorted(parts, n, f_s)
    f = f_s[0:topk, :]
    z = jnp.sum(jnp.exp(f - f[0:1, :]), axis=0, keepdims=True)
    tau = 0.5 * (f[topk - 1:topk, :] + f_s[topk:n, :])
    thr = tau - s0
    cnt = jnp.zeros_like(thr)
    for r in range(topk):
        cnt = jnp.where(b[r:r + 1, :] >= thr, float(r + 1), cnt)
    cnt_ref[...] = cnt
    rank1 = jnp.zeros_like(s1)
    for r in range(topk):
        rank1 = jnp.where(b[r:r + 1, :] > s1, float(r + 1), rank1)
    w0_ref[...] = jnp.exp(s0 - a[0:1, :]) * (0.5 / z)
    rk_ref[...] = rank1.astype(BF16)
    w1_ref[...] = jnp.exp(s1 - b[0:1, :]).astype(BF16)


def _peer_route(q, keys, tm):
    T = q.shape[0]
    H, _, NK, half = keys.shape
    sl = F32_SUBLANES
    counts = _pair_counts(PEER_TOPK + 1)
    assert PEER_TOPK == 2 * sl and counts[1] <= sl and counts[sl - 1] > 1 >= counts[sl]
    assert NK > PEER_TOPK and NK % sl == 0
    per_key = jax.ShapeDtypeStruct((H, NK, T), F32)
    per_col = jax.ShapeDtypeStruct((H, NK, T), BF16)
    ospec = pl.BlockSpec((None, NK, tm), lambda i, h: (h, 0, i))
    return pl.pallas_call(
        functools.partial(_peer_route_body, half=half, topk=PEER_TOPK),
        grid=(T // tm, H),
        in_specs=[pl.BlockSpec((tm, 2 * half), lambda i, h: (i, h)),
                  pl.BlockSpec((None, 2, NK, half), lambda i, h: (h, 0, 0, 0))],
        out_specs=[ospec, ospec, ospec, ospec],
        out_shape=[per_key, per_key, per_col, per_col],
        scratch_shapes=[pltpu.VMEM((PEER_TOPK + sl, tm), F32)] * 3,
        compiler_params=_params("parallel", "parallel"),
        name="peer_route",
    )(q, keys)


def _peer_mix_body(xnt_ref, xres_ref, u_ref, vt_ref, cnt_ref, w0_ref, rk_ref, w1_ref, o_ref,
                   h_s, gh_s, acc_s, *, tl):
    H, ti, tm = cnt_ref.shape
    NK = rk_ref.shape[1]
    j = pl.program_id(1)

    last = pl.num_programs(1) - 1
    cur = j % 2

    def fold():
        acc_s[...] += jnp.dot(vt_ref[...], gh_s[1 - cur], preferred_element_type=F32)

    def row_tile(row):
        tile = jnp.broadcast_to(row, (BF16_SUBLANES, row.shape[1])).astype(BF16)
        return pltpu.repeat(tile, NK // BF16_SUBLANES, axis=0)

    def build():
        for ii in range(ti):
            rs = slice(ii * NK, (ii + 1) * NK)
            for lc in range(tm // tl):
                ls = slice(lc * tl, (lc + 1) * tl)
                g = None
                for h in range(H):
                    c = row_tile(cnt_ref[h, ii:ii + 1, ls])
                    a = row_tile(w0_ref[h, ii:ii + 1, ls])
                    term = a * jnp.where(rk_ref[h, :, ls] < c, w1_ref[h, :, ls], 0)
                    g = term if g is None else g + term
                gh_s[cur, rs, ls] = g
        h_s[...] = jnp.dot(u_ref[...], xnt_ref[...], preferred_element_type=F32)
        for ii in range(ti):
            rs = slice(ii * NK, (ii + 1) * NK)
            gh_s[cur, rs, :] = gh_s[cur, rs, :] * _gelu_tanh_x2(h_s[rs, :].astype(BF16))

    @pl.when(j == 0)
    def _():
        acc_s[...] = jnp.zeros_like(acc_s)
        build()

    @pl.when(jnp.logical_and(j > 0, j < last))
    def _():
        fold()
        build()

    @pl.when(j == last)
    def _():
        fold()
        o_ref[...] = xres_ref[...] + acc_s[...].T


def _peer_mix(xnt, xres, u_bf, vt_bf, cnt, w0, rk, w1, tm, ti, tl):
    D, T = xnt.shape
    H, NK, _ = cnt.shape
    E = u_bf.shape[0]
    te = ti * NK
    n_e = E // te
    build = lambda j: jnp.minimum(j, n_e - 1)
    fold = lambda j: jnp.maximum(j - 1, 0)
    keys = pl.BlockSpec((H, ti, tm), lambda i, j: (0, build(j), i))
    full = pl.BlockSpec((H, NK, tm), lambda i, j: (0, 0, i))
    return pl.pallas_call(
        functools.partial(_peer_mix_body, tl=tl),
        grid=(T // tm, n_e + 1),
        in_specs=[pl.BlockSpec((D, tm), lambda i, j: (0, i)),
                  pl.BlockSpec((tm, D), lambda i, j: (i, 0)),
                  pl.BlockSpec((te, D), lambda i, j: (build(j), 0)),
                  pl.BlockSpec((D, te), lambda i, j: (0, fold(j))),
                  keys, keys, full, full],
        out_specs=pl.BlockSpec((tm, D), lambda i, j: (i, 0)),
        out_shape=jax.ShapeDtypeStruct((T, D), F32),
        scratch_shapes=[pltpu.VMEM((te, tm), F32), pltpu.VMEM((2, te, tm), BF16),
                        pltpu.VMEM((D, tm), F32)],
        compiler_params=_params("parallel", "arbitrary"),
        name="peer_mix",
    )(xnt, xres, u_bf, vt_bf, cnt, w0, rk, w1)


def _final_norm_body(x_ref, w_ref, o_ref):
    o_ref[...] = _rms(x_ref[...], w_ref[...])


def _final_norm(x2, w, tm):
    T, D = x2.shape
    return pl.pallas_call(
        _final_norm_body,
        grid=(T // tm,),
        in_specs=[pl.BlockSpec((tm, D), lambda i: (i, 0)), pl.BlockSpec((1, D), lambda i: (0, 0))],
        out_specs=pl.BlockSpec((tm, D), lambda i: (i, 0)),
        out_shape=jax.ShapeDtypeStruct((T, D), F32),
        compiler_params=_params("parallel"),
        name="final_norm",
    )(x2, w)


def _rope_table(pos, inv_freq):
    ang = pos[:, None] * inv_freq[None, :]
    ang = jnp.concatenate([ang, ang], axis=-1)
    return jnp.cos(ang), jnp.sin(ang)


def _position_tables(S):
    t = jnp.arange(S, dtype=F32)
    n_rows = S // GRID_W
    rows = jnp.repeat(jnp.arange(n_rows, dtype=F32), GRID_W)
    cols = jnp.tile(jnp.arange(GRID_W, dtype=F32), n_rows)
    ret_inv = 1.0 / (10000.0 ** jnp.linspace(0.0, 1.0, HEAD_DIM // 2, dtype=F32))
    ret_cos, ret_sin = _rope_table(t, ret_inv)
    ax_n = HEAD_DIM // 4
    ax_inv = ROPE_BASE ** (-jnp.arange(ax_n, dtype=F32) / ax_n)
    cr, sr = _rope_table(rows, ax_inv)
    cc, sc = _rope_table(cols, ax_inv)
    ax_cos = jnp.concatenate([cr, cc], axis=-1)
    ax_sin = jnp.concatenate([sr, sc], axis=-1)
    return (jnp.tile(ret_cos, (1, RET_HEADS)), jnp.tile(ret_sin, (1, RET_HEADS)),
            jnp.tile(ax_cos, (1, ATT_HEADS)), jnp.tile(ax_sin, (1, ATT_HEADS)))


def _tiles(B, S, n_keys):
    T = B * S
    return dict(
        tm=min(512, T),
        tb=min(256, S),
        tp=min(512, S),
        tq=min(256, S),
        tk=min(256, S),
        tr=min(512, T),
        tmix=min(512, T),
        ti=min(8, n_keys),
        tl=min(256, T),
    )


def kernel(x, ln1_w, w_in, ret_log_decay, ret_gn_w, lru_conv_w, lru_conv_b, lru_gate_w, lru_gate_b,
           lru_lambda, attn_q_norm, attn_k_norm, w_out, ln2_w, peer_wq, peer_keys, peer_u, peer_v,
           lnf_w):
    B, S, D = x.shape
    T = B * S
    depth = w_in.shape[0]
    n_keys = peer_keys.shape[3]
    tl = _tiles(B, S, n_keys)
    widths = (RET_W,) * 4 + (LRU_W,) * 2 + (ATT_W, KV_W, KV_W)
    ret_cos, ret_sin, ax_cos, ax_sin = _position_tables(S)
    x2 = x.reshape(T, D)
    for l in range(depth):
        rq, rk, rv, rg, lx, lgt, aq, ak, av = _inproj(
            x2, ln1_w[l].reshape(1, D), w_in[l].astype(BF16), widths, tl["tm"])
        o_ret = _retention(rq, rk, rv, rg, ret_cos, ret_sin, ret_log_decay[l],
                           ret_gn_w[l].reshape(1, RET_W), B, S)
        o_lru = _lru(lx, lgt, lru_conv_w[l], lru_conv_b[l], lru_gate_w[l], lru_gate_b[l],
                     lru_lambda[l], B, S, tl["tb"])
        qp, kt, vp = _attn_prep(aq, ak, av, ax_cos, ax_sin, attn_q_norm[l], attn_k_norm[l],
                                B, S, tl["tp"])
        o_att = _attention(qp, kt, vp, B, S, tl["tq"], tl["tk"])
        x2, xn, q = _outproj(x2, o_ret, o_lru, o_att, w_out[l].astype(BF16),
                             ln2_w[l].reshape(1, D), peer_wq[l].astype(BF16), tl["tm"])
        cnt, w0, rk, w1 = _peer_route(q, peer_keys[l], tl["tr"])
        x2 = _peer_mix(xn, x2, peer_u[l].astype(BF16), peer_v[l].T.astype(BF16),
                       cnt, w0, rk, w1, tl["tmix"], tl["ti"], tl["tl"])
    return _final_norm(x2, lnf_w.reshape(1, D), tl["tm"]).reshape(B, S, D)
```

```python
import functools

import jax
import jax.numpy as jnp
from jax import lax
from jax.experimental import pallas as pl
from jax.experimental.pallas import tpu as pltpu

F32 = jnp.float32
BF16 = jnp.bfloat16
EPS = 1e-6
LOG2_E = 1.4426950408889634

GRID_W = 64
CHUNK = 128
HEAD_DIM = 64
RET_HEADS = 4
RET_W = RET_HEADS * HEAD_DIM
LRU_W = 256
LRU_CONV = 4
LRU_C = 8.0
ATT_HEADS = 8
ATT_KV_HEADS = 2
ATT_GROUP = ATT_HEADS // ATT_KV_HEADS
ATT_W = ATT_HEADS * HEAD_DIM
KV_W = ATT_KV_HEADS * HEAD_DIM
ROPE_BASE = 10000.0
PEER_TOPK = 16
F32_SUBLANES = 8
BF16_SUBLANES = 16

VMEM_LIMIT_BYTES = 56 * 1024 * 1024
NEG_INF = float("-inf")


def _params(*sem):
    return pltpu.CompilerParams(dimension_semantics=sem, vmem_limit_bytes=VMEM_LIMIT_BYTES)


def _rms(x, w):
    return x * lax.rsqrt(jnp.mean(x * x, axis=-1, keepdims=True) + EPS) * w


def _gelu_tanh(x):
    return 0.5 * x * (1.0 + jnp.tanh(0.7978845608028654 * (x + 0.044715 * (x * x * x))))


def _gelu_tanh_x2(x):
    return x + x * jnp.tanh(x * (0.7978845608028654 + (0.7978845608028654 * 0.044715) * (x * x)))


def _sigmoid(x):
    return 1.0 / (1.0 + jnp.exp(-x))


def _nt_dot(a, b):
    return lax.dot_general(a, b, (((1,), (1,)), ((), ())), preferred_element_type=F32)


def _group_sum(x, member):
    hi = x.astype(BF16)
    lo = (x - hi.astype(F32)).astype(BF16)
    return (jnp.dot(hi, member, preferred_element_type=F32)
            + jnp.dot(lo, member, preferred_element_type=F32))


def _tn_dot(a, b):
    return lax.dot_general(a, b, (((0,), (0,)), ((), ())), preferred_element_type=F32)


def _inproj_body(x_ref, lnw_ref, w_ref, *out_refs, widths):
    hb = _rms(x_ref[...], lnw_ref[...]).astype(BF16)
    off = 0
    for o_ref, wd in zip(out_refs, widths):
        o_ref[...] = jnp.dot(hb, w_ref[:, off:off + wd],
                             preferred_element_type=F32).astype(o_ref.dtype)
        off += wd


def _inproj(x2, lnw, w_bf, widths, tm):
    T, D = x2.shape
    n_in = w_bf.shape[1]
    return pl.pallas_call(
        functools.partial(_inproj_body, widths=widths),
        grid=(T // tm,),
        in_specs=[pl.BlockSpec((tm, D), lambda i: (i, 0)),
                  pl.BlockSpec((1, D), lambda i: (0, 0)),
                  pl.BlockSpec((D, n_in), lambda i: (0, 0))],
        out_specs=[pl.BlockSpec((tm, wd), lambda i: (i, 0)) for wd in widths],
        out_shape=[jax.ShapeDtypeStruct((T, wd), F32) for wd in widths],
        compiler_params=_params("parallel"),
        name="inproj",
    )(x2, lnw, w_bf)


def _ret_body(ld_ref, q_ref, k_ref, v_ref, g_ref, cos_ref, sin_ref, lgl_ref, gnw_ref, o_ref,
              qr_s, kr_s, st_s, d_s, *, n_chunks):
    C, W = CHUNK, RET_W
    lane = lax.broadcasted_iota(jnp.int32, (1, W), 1)
    first_half = (lane % HEAD_DIM) < (HEAD_DIM // 2)
    head_masks = [(lane // HEAD_DIM) == h for h in range(RET_HEADS)]
    same_head = (lax.broadcasted_iota(jnp.int32, (W, W), 0) // HEAD_DIM
                 == lax.broadcasted_iota(jnp.int32, (W, W), 1) // HEAD_DIM)
    bd = same_head.astype(F32)
    lgf = lgl_ref[0:1, :]
    lgb = lgl_ref[1:2, :]
    idx = lax.broadcasted_iota(jnp.int32, (C, 1), 0).astype(F32)
    decf = jnp.exp(lgf * float(C))
    decb = jnp.exp(lgb * float(C))

    diff = (lax.broadcasted_iota(jnp.int32, (C, C), 0)
            - lax.broadcasted_iota(jnp.int32, (C, C), 1)).astype(F32)
    for h in range(RET_HEADS):
        d_s[h] = jnp.where(diff >= 0.0,
                           jnp.exp(ld_ref[0, h] * jnp.maximum(diff, 0.0)),
                           jnp.exp(ld_ref[1, h] * jnp.maximum(-diff, 0.0)))

    def rot_half(x):
        return jnp.where(first_half, -pltpu.roll(x, W - HEAD_DIM // 2, 1),
                         pltpu.roll(x, HEAD_DIM // 2, 1))

    def prep(c, carry):
        r0 = pl.multiple_of(c * C, C)
        cs = cos_ref[pl.ds(r0, C), :]
        sn = sin_ref[pl.ds(r0, C), :]
        q = q_ref[pl.ds(r0, C), :]
        k = k_ref[pl.ds(r0, C), :]
        qr_s[pl.ds(r0, C), :] = q * cs + rot_half(q) * sn
        kr_s[pl.ds(r0, C), :] = (k * cs + rot_half(k) * sn) * (HEAD_DIM ** -0.5)
        return carry

    lax.fori_loop(0, n_chunks, prep, 0, unroll=4)

    st_s[...] = jnp.zeros((W, W), F32)

    def fwd(c, carry):
        r0 = pl.multiple_of(c * C, C)
        q = qr_s[pl.ds(r0, C), :]
        k = kr_s[pl.ds(r0, C), :]
        v = v_ref[pl.ds(r0, C), :]
        st = st_s[...]
        qw = q * jnp.exp(lgf * (idx + 1.0))
        o_ref[pl.ds(r0, C), :] = jnp.dot(qw.astype(BF16), (st * bd).astype(BF16),
                                         preferred_element_type=F32)
        kw = k * jnp.exp(lgf * (float(C) - 1.0 - idx))
        st_s[...] = st * decf + _tn_dot(kw.astype(BF16), v.astype(BF16))
        return carry

    lax.fori_loop(0, n_chunks, fwd, 0, unroll=4)

    st_s[...] = jnp.zeros((W, W), F32)

    def bwd(t, carry):
        c = n_chunks - 1 - t
        r0 = pl.multiple_of(c * C, C)
        q = qr_s[pl.ds(r0, C), :]
        k = kr_s[pl.ds(r0, C), :]
        v = v_ref[pl.ds(r0, C), :]
        st = st_s[...]
        qw = q * jnp.exp(lgb * (float(C) - idx))
        vb = v.astype(BF16)
        kb = k.astype(BF16)
        o = o_ref[pl.ds(r0, C), :] + jnp.dot(qw.astype(BF16), (st * bd).astype(BF16),
                                              preferred_element_type=F32)
        kw = k * jnp.exp(lgb * idx)
        st_s[...] = st * decb + _tn_dot(kw.astype(BF16), vb)
        for h in range(RET_HEADS):
            qm = jnp.where(head_masks[h], q, 0.0).astype(BF16)
            s = _nt_dot(qm, kb) * d_s[h]
            oh = jnp.dot(s.astype(BF16), vb, preferred_element_type=F32)
            o = o + jnp.where(head_masks[h], oh, 0.0)
        ms = _group_sum(o * o, bd.astype(BF16)) * (1.0 / HEAD_DIM)
        on = o * lax.rsqrt(ms + EPS) * gnw_ref[...]
        g = g_ref[pl.ds(r0, C), :]
        o_ref[pl.ds(r0, C), :] = (g * _sigmoid(g)) * on
        return carry

    lax.fori_loop(0, n_chunks, bwd, 0, unroll=4)


def _retention(rq, rk, rv, rg, cos, sin, log_decay, gn_w, B, S):
    W = RET_W
    lgl = jnp.repeat(log_decay, HEAD_DIM, axis=1)
    seq = pl.BlockSpec((S, W), lambda b: (b, 0), pipeline_mode=pl.Buffered(1))
    tab = pl.BlockSpec((S, W), lambda b: (0, 0), pipeline_mode=pl.Buffered(1))
    return pl.pallas_call(
        functools.partial(_ret_body, n_chunks=S // CHUNK),
        grid=(B,),
        in_specs=[pl.BlockSpec(memory_space=pltpu.SMEM), seq, seq, seq, seq, tab, tab,
                  pl.BlockSpec((2, W), lambda b: (0, 0)),
                  pl.BlockSpec((1, W), lambda b: (0, 0))],
        out_specs=pl.BlockSpec((S, W), lambda b: (b, 0)),
        out_shape=jax.ShapeDtypeStruct((B * S, W), F32),
        scratch_shapes=[pltpu.VMEM((S, W), F32), pltpu.VMEM((S, W), F32),
                        pltpu.VMEM((W, W), F32), pltpu.VMEM((RET_HEADS, CHUNK, CHUNK), F32)],
        compiler_params=_params("parallel"),
        name="retention",
    )(log_decay, rq, rk, rv, rg, cos, sin, lgl, gn_w)


def _lru_body(x_ref, gt_ref, cw_ref, cb_ref, gw_ref, gb_ref, lam_ref, o_ref, xc_s, h_s,
              *, n_blocks, tb):
    W = LRU_W
    S = n_blocks * tb
    z = -lam_ref[...]
    softplus = jnp.maximum(z, 0.0) + jnp.log(1.0 + jnp.exp(-jnp.abs(z)))
    cl = -LRU_C * softplus
    row = lax.broadcasted_iota(jnp.int32, (tb, 1), 0)
    left = LRU_CONV // 2

    def conv_block(kb):
        r0 = pl.multiple_of(kb * tb, tb)
        halo = F32_SUBLANES
        prev = x_ref[pl.ds(pl.multiple_of(jnp.maximum(r0 - halo, 0), halo), halo), :]
        nxt = x_ref[pl.ds(pl.multiple_of(jnp.minimum(r0 + tb, S - halo), halo), halo), :]
        ext = jnp.concatenate([prev, x_ref[pl.ds(r0, tb), :], nxt], axis=0)
        t = row + r0
        acc = cb_ref[...] + jnp.zeros((tb, W), F32)
        for j in range(LRU_CONV):
            off = j - left
            if off == 0:
                xs = ext[halo:halo + tb]
            else:
                xs = pltpu.roll(ext, (-off) % (tb + 2 * halo), 0)[halo:halo + tb]
                xs = jnp.where((t + off >= 0) & (t + off < S), xs, 0.0)
            acc = acc + xs * cw_ref[j:j + 1, :]
        return acc

    def gates(xc, d):
        g = jnp.dot(xc.astype(BF16), gw_ref[:, d * 2 * W:(d + 1) * 2 * W],
                    preferred_element_type=F32) + gb_ref[:, d * 2 * W:(d + 1) * 2 * W]
        r = _sigmoid(g[:, :W])
        i = _sigmoid(g[:, W:])
        log_a = cl[d:d + 1, :] * r
        a = jnp.exp(log_a)
        b = jnp.sqrt(-jnp.tanh(log_a) * (a * a + 1.0)) * (i * xc)
        return a, b

    def scan_block(a, b, reverse):
        d = 1
        while d < tb:
            if reverse:
                keep = row < tb - d
                a_sh = jnp.where(keep, pltpu.roll(a, tb - d, 0), 1.0)
                b_sh = jnp.where(keep, pltpu.roll(b, tb - d, 0), 0.0)
            else:
                keep = row >= d
                a_sh = jnp.where(keep, pltpu.roll(a, d, 0), 1.0)
                b_sh = jnp.where(keep, pltpu.roll(b, d, 0), 0.0)
            b = a * b_sh + b
            a = a * a_sh
            d *= 2
        return a, b

    def fwd(kb, carry):
        r0 = pl.multiple_of(kb * tb, tb)
        xc = conv_block(kb)
        xc_s[pl.ds(r0, tb), :] = xc
        a, b = gates(xc, 0)
        a, b = scan_block(a, b, False)
        h = a * carry + b
        h_s[pl.ds(r0, tb), :] = h
        return h[tb - 1:tb, :]

    lax.fori_loop(0, n_blocks, fwd, jnp.zeros((1, W), F32))

    def bwd(t, carry):
        kb = n_blocks - 1 - t
        r0 = pl.multiple_of(kb * tb, tb)
        xc = xc_s[pl.ds(r0, tb), :]
        a, b = gates(xc, 1)
        a, b = scan_block(a, b, True)
        h = a * carry + b
        o_ref[pl.ds(r0, tb), :] = (h_s[pl.ds(r0, tb), :] + h) * _gelu_tanh(gt_ref[pl.ds(r0, tb), :])
        return h[0:1, :]

    lax.fori_loop(0, n_blocks, bwd, jnp.zeros((1, W), F32))


def _lru(lx, lgt, conv_w, conv_b, gate_w, gate_b, lam, B, S, tb):
    W = LRU_W
    nb = W // HEAD_DIM
    eye = jnp.eye(nb, dtype=F32)
    gw = jnp.einsum('dgnkm,nj->nkdgjm', gate_w, eye).reshape(W, 4 * W).astype(BF16)
    gb = gate_b.reshape(1, 4 * W)
    seq = pl.BlockSpec((S, W), lambda b: (b, 0))
    const = lambda shape: pl.BlockSpec(shape, lambda b: (0,) * len(shape))
    return pl.pallas_call(
        functools.partial(_lru_body, n_blocks=S // tb, tb=tb),
        grid=(B,),
        in_specs=[seq, seq, const((LRU_CONV, W)), const((1, W)), const((W, 4 * W)),
                  const((1, 4 * W)), const((2, W))],
        out_specs=seq,
        out_shape=jax.ShapeDtypeStruct((B * S, W), F32),
        scratch_shapes=[pltpu.VMEM((S, W), F32), pltpu.VMEM((S, W), F32)],
        compiler_params=_params("parallel"),
        name="rglru",
    )(lx, lgt, conv_w, conv_b.reshape(1, W), gw, gb, lam)


def _attn_prep_body(q_ref, k_ref, v_ref, cos_ref, sin_ref, qnw_ref, knw_ref, rep_ref, rept_ref,
                    qo_ref, kt_ref, vo_ref):
    def head_norm_rot(x, w, cs, sn):
        W = x.shape[-1]
        lane = lax.broadcasted_iota(jnp.int32, (1, W), 1)
        same_head = (lax.broadcasted_iota(jnp.int32, (W, W), 0) // HEAD_DIM
                     == lax.broadcasted_iota(jnp.int32, (W, W), 1) // HEAD_DIM)
        ms = _group_sum(x * x, same_head.astype(BF16)) * (1.0 / HEAD_DIM)
        xn = x * lax.rsqrt(ms + EPS) * w
        quarter = HEAD_DIM // 4
        rot = jnp.where((lane % (2 * quarter)) < quarter, -pltpu.roll(xn, W - quarter, 1),
                        pltpu.roll(xn, quarter, 1))
        return xn * cs + rot * sn

    cs = cos_ref[...]
    sn = sin_ref[...]
    q = head_norm_rot(q_ref[...], qnw_ref[...], cs, sn)
    qo_ref[...] = (q * (HEAD_DIM ** -0.5 * LOG2_E)).astype(BF16)
    k = head_norm_rot(k_ref[...], knw_ref[...], cs[:, :KV_W], sn[:, :KV_W]).astype(BF16)
    kt_ref[...] = _nt_dot(rept_ref[...], k).astype(BF16)
    vo_ref[...] = jnp.dot(v_ref[...].astype(BF16), rep_ref[...],
                          preferred_element_type=F32).astype(BF16)


def _attn_prep(aq, ak, av, cos, sin, qn_w, kn_w, B, S, tm):
    T = B * S
    ns = S // tm
    GW = ATT_GROUP * HEAD_DIM
    src = jnp.arange(ATT_KV_HEADS * GW)
    src = (src // GW) * HEAD_DIM + src % HEAD_DIM
    rep = (jnp.arange(KV_W)[:, None] == src[None, :]).astype(BF16)
    return pl.pallas_call(
        _attn_prep_body,
        grid=(B, ns),
        in_specs=[pl.BlockSpec((tm, ATT_W), lambda b, s: (b * ns + s, 0)),
                  pl.BlockSpec((tm, KV_W), lambda b, s: (b * ns + s, 0)),
                  pl.BlockSpec((tm, KV_W), lambda b, s: (b * ns + s, 0)),
                  pl.BlockSpec((tm, ATT_W), lambda b, s: (s, 0)),
                  pl.BlockSpec((tm, ATT_W), lambda b, s: (s, 0)),
                  pl.BlockSpec((1, ATT_W), lambda b, s: (0, 0)),
                  pl.BlockSpec((1, KV_W), lambda b, s: (0, 0)),
                  pl.BlockSpec((KV_W, ATT_KV_HEADS * GW), lambda b, s: (0, 0)),
                  pl.BlockSpec((ATT_KV_HEADS * GW, KV_W), lambda b, s: (0, 0))],
        out_specs=[pl.BlockSpec((tm, ATT_W), lambda b, s: (b * ns + s, 0)),
                   pl.BlockSpec((None, ATT_KV_HEADS * GW, tm), lambda b, s: (b, 0, s)),
                   pl.BlockSpec((tm, ATT_KV_HEADS * GW), lambda b, s: (b * ns + s, 0))],
        out_shape=[jax.ShapeDtypeStruct((T, ATT_W), BF16),
                   jax.ShapeDtypeStruct((B, ATT_KV_HEADS * GW, S), BF16),
                   jax.ShapeDtypeStruct((T, ATT_KV_HEADS * GW), BF16)],
        compiler_params=_params("parallel", "parallel"),
        name="attn_prep",
    )(aq, ak, av, cos, sin, jnp.tile(qn_w, ATT_HEADS).reshape(1, ATT_W),
      jnp.tile(kn_w, ATT_KV_HEADS).reshape(1, KV_W), rep, rep.T)


def _attn_body(q_ref, kt_ref, v_ref, o_ref, *, tq, tk, nk):
    GW = ATT_GROUP * HEAD_DIM
    lane = lax.broadcasted_iota(jnp.int32, (1, GW), 1)
    masks = [(lane // HEAD_DIM) == g for g in range(ATT_GROUP)]
    q = q_ref[...]
    zero = jnp.zeros_like(q)
    qs = jnp.concatenate([jnp.where(m, q, zero) for m in masks], axis=0)
    rows = ATT_GROUP * tq
    m_run = jnp.full((rows, 1), NEG_INF, F32)
    l_run = jnp.zeros((rows, 1), F32)
    acc = jnp.zeros((rows, GW), F32)
    for c in range(nk):
        s = jnp.dot(qs, kt_ref[:, c * tk:(c + 1) * tk], preferred_element_type=F32)
        m_new = jnp.maximum(m_run, jnp.max(s, axis=-1, keepdims=True))
        alpha = jnp.exp2(m_run - m_new)
        p = jnp.exp2(s - m_new)
        l_run = alpha * l_run + jnp.sum(p, axis=-1, keepdims=True)
        acc = alpha * acc + jnp.dot(p.astype(BF16), v_ref[c * tk:(c + 1) * tk, :],
                                    preferred_element_type=F32)
        m_run = m_new
    on = acc * (1.0 / l_run)
    o = jnp.zeros((tq, GW), F32)
    for g in range(ATT_GROUP):
        o = o + jnp.where(masks[g], on[g * tq:(g + 1) * tq], 0.0)
    o_ref[...] = o


def _attention(qp, kt, vp, B, S, tq, tk):
    T = B * S
    nq = S // tq
    GW = ATT_GROUP * HEAD_DIM
    return pl.pallas_call(
        functools.partial(_attn_body, tq=tq, tk=tk, nk=S // tk),
        grid=(B, ATT_KV_HEADS, nq),
        in_specs=[pl.BlockSpec((tq, GW), lambda b, h, i: (b * nq + i, h)),
                  pl.BlockSpec((None, GW, S), lambda b, h, i: (b, h, 0)),
                  pl.BlockSpec((S, GW), lambda b, h, i: (b, h))],
        out_specs=pl.BlockSpec((tq, GW), lambda b, h, i: (b * nq + i, h)),
        out_shape=jax.ShapeDtypeStruct((T, ATT_W), F32),
        compiler_params=_params("parallel", "parallel", "parallel"),
        name="attention",
    )(qp, kt, vp)


def _outproj_body(x_ref, oret_ref, olru_ref, oatt_ref, w_ref, ln2_ref, wq_ref,
                  xo_ref, xn_ref, q_ref):
    y = jnp.dot(oret_ref[...].astype(BF16), w_ref[0:RET_W, :], preferred_element_type=F32)
    y = y + jnp.dot(olru_ref[...].astype(BF16), w_ref[RET_W:RET_W + LRU_W, :],
                    preferred_element_type=F32)
    y = y + jnp.dot(oatt_ref[...].astype(BF16), w_ref[RET_W + LRU_W:, :],
                    preferred_element_type=F32)
    x = x_ref[...] + y
    xo_ref[...] = x
    xn = _rms(x, ln2_ref[...]).astype(BF16)
    xn_ref[...] = xn.astype(F32).T.astype(BF16)
    q_ref[...] = jnp.dot(xn, wq_ref[...], preferred_element_type=F32)


def _outproj(x2, o_ret, o_lru, o_att, w_out_bf, ln2, wq_bf, tm):
    T, D = x2.shape
    QW = wq_bf.shape[1]
    MW = w_out_bf.shape[0]
    tok = lambda w: pl.BlockSpec((tm, w), lambda i: (i, 0))
    return pl.pallas_call(
        _outproj_body,
        grid=(T // tm,),
        in_specs=[tok(D), tok(RET_W), tok(LRU_W), tok(ATT_W),
                  pl.BlockSpec((MW, D), lambda i: (0, 0)),
                  pl.BlockSpec((1, D), lambda i: (0, 0)),
                  pl.BlockSpec((D, QW), lambda i: (0, 0))],
        out_specs=[tok(D), pl.BlockSpec((D, tm), lambda i: (0, i)), tok(QW)],
        out_shape=[jax.ShapeDtypeStruct((T, D), F32), jax.ShapeDtypeStruct((D, T), BF16),
                   jax.ShapeDtypeStruct((T, QW), F32)],
        compiler_params=_params("parallel"),
        name="outproj",
    )(x2, o_ret, o_lru, o_att, w_out_bf, ln2, wq_bf)


def _sorting_network(n):
    pairs = []
    p = 1
    while p < n:
        k = p
        while k >= 1:
            for j in range(k % p, n - k, 2 * k):
                for i in range(min(k, n - j - k)):
                    if (i + j) // (2 * p) == (i + j + k) // (2 * p):
                        pairs.append((i + j, i + j + k))
            k //= 2
        p *= 2
    return pairs


def _pair_counts(n):
    return [n // (i + 1) for i in range(n)]


def _peer_route_body(q_ref, keys_ref, cnt_ref, w0_ref, rk_ref, w1_ref, a_s, b_s, f_s,
                     *, half, topk):
    s0 = _nt_dot(keys_ref[0], q_ref[:, :half])
    s1 = _nt_dot(keys_ref[1], q_ref[:, half:])

    def top_sorted(tiles, n, out_ref):
        rows = list(tiles) + [None] * (pl.next_power_of_2(len(tiles)) - len(tiles))
        for lo_i, hi_i in _sorting_network(len(rows)):
            x, y = rows[lo_i], rows[hi_i]
            if y is None:
                continue
            if x is None:
                rows[lo_i], rows[hi_i] = y, None
            else:
                rows[lo_i], rows[hi_i] = jnp.maximum(x, y), jnp.minimum(x, y)
        rows = [t for t in rows if t is not None]
        for r in range(n):
            m = jnp.max(rows[0], axis=0, keepdims=True)
            out_ref[r:r + 1, :] = m
            hit = rows[0] == m
            for v in range(min(len(rows), n - r - 1)):
                nxt = rows[v + 1] if v + 1 < len(rows) else NEG_INF
                rows[v] = jnp.where(hit, nxt, rows[v])

    def sublane_tiles(x):
        sl = F32_SUBLANES
        return [x[sl * v:sl * (v + 1), :] for v in range(x.shape[0] // sl)]

    n = topk + 1
    top_sorted(sublane_tiles(s0), n, a_s)
    top_sorted(sublane_tiles(s1), n, b_s)
    a = a_s[0:topk, :]
    b = b_s[0:topk, :]
    counts = _pair_counts(n)
    sl = F32_SUBLANES
    row8 = lax.broadcasted_iota(jnp.int32, (sl, 1), 0)
    parts = [a[0:1, :] + b[0:sl, :], a[0:1, :] + b[sl:2 * sl, :]]
    i = 1
    while counts[i] > 1:
        parts.append(jnp.where(row8 < counts[i], a[i:i + 1, :] + b[0:sl, :], NEG_INF))
        i += 1
    parts.append(a[i:, :] + b[0:1, :])
    last = jnp.where(row8 == 0, a[0:1, :] + b_s[topk:n, :],
                     jnp.where(row8 == 1, a_s[topk:n, :] + b[0:1, :], NEG_INF))
    parts.append(last)
    top_sorted(parts, n, f_s)
    f = f_s[0:topk, :]
    z = jnp.sum(jnp.exp(f - f[0:1, :]), axis=0, keepdims=True)
    tau = 0.5 * (f[topk - 1:topk, :] + f_s[topk:n, :])
    thr = tau - s0
    cnt = jnp.zeros_like(thr)
    for r in range(topk):
        cnt = jnp.where(b[r:r + 1, :] >= thr, float(r + 1), cnt)
    cnt_ref[...] = cnt
    rank1 = jnp.zeros_like(s1)
    for r in range(topk):
        rank1 = jnp.where(b[r:r + 1, :] > s1, float(r + 1), rank1)
    w0_ref[...] = jnp.exp(s0 - a[0:1, :]) * (0.5 / z)
    rk_ref[...] = rank1.astype(BF16)
    w1_ref[...] = jnp.exp(s1 - b[0:1, :]).astype(BF16)


def _peer_route(q, keys, tm):
    T = q.shape[0]
    H, _, NK, half = keys.shape
    sl = F32_SUBLANES
    counts = _pair_counts(PEER_TOPK + 1)
    assert PEER_TOPK == 2 * sl and counts[1] <= sl and counts[sl - 1] > 1 >= counts[sl]
    assert NK > PEER_TOPK and NK % sl == 0
    per_key = jax.ShapeDtypeStruct((H, NK, T), F32)
    per_col = jax.ShapeDtypeStruct((H, NK, T), BF16)
    ospec = pl.BlockSpec((None, NK, tm), lambda i, h: (h, 0, i))
    return pl.pallas_call(
        functools.partial(_peer_route_body, half=half, topk=PEER_TOPK),
        grid=(T // tm, H),
        in_specs=[pl.BlockSpec((tm, 2 * half), lambda i, h: (i, h)),
                  pl.BlockSpec((None, 2, NK, half), lambda i, h: (h, 0, 0, 0))],
        out_specs=[ospec, ospec, ospec, ospec],
        out_shape=[per_key, per_key, per_col, per_col],
        scratch_shapes=[pltpu.VMEM((PEER_TOPK + sl, tm), F32)] * 3,
        compiler_params=_params("parallel", "parallel"),
        name="peer_route",
    )(q, keys)


def _peer_mix_body(xnt_ref, xres_ref, u_ref, vt_ref, cnt_ref, w0_ref, rk_ref, w1_ref, o_ref,
                   h_s, gh_s, acc_s, *, tl):
    H, ti, tm = cnt_ref.shape
    NK = rk_ref.shape[1]
    j = pl.program_id(1)

    last = pl.num_programs(1) - 1
    cur = j % 2

    def fold():
        acc_s[...] += jnp.dot(vt_ref[...], gh_s[1 - cur], preferred_element_type=F32)

    def row_tile(row):
        tile = jnp.broadcast_to(row, (BF16_SUBLANES, row.shape[1])).astype(BF16)
        return pltpu.repeat(tile, NK // BF16_SUBLANES, axis=0)

    def build():
        for ii in range(ti):
            rs = slice(ii * NK, (ii + 1) * NK)
            for lc in range(tm // tl):
                ls = slice(lc * tl, (lc + 1) * tl)
                g = None
                for h in range(H):
                    c = row_tile(cnt_ref[h, ii:ii + 1, ls])
                    a = row_tile(w0_ref[h, ii:ii + 1, ls])
                    term = a * jnp.where(rk_ref[h, :, ls] < c, w1_ref[h, :, ls], 0)
                    g = term if g is None else g + term
                gh_s[cur, rs, ls] = g
        h_s[...] = jnp.dot(u_ref[...], xnt_ref[...], preferred_element_type=F32)
        for ii in range(ti):
            rs = slice(ii * NK, (ii + 1) * NK)
            gh_s[cur, rs, :] = gh_s[cur, rs, :] * _gelu_tanh_x2(h_s[rs, :].astype(BF16))

    @pl.when(j == 0)
    def _():
        acc_s[...] = jnp.zeros_like(acc_s)
        build()

    @pl.when(jnp.logical_and(j > 0, j < last))
    def _():
        fold()
        build()

    @pl.when(j == last)
    def _():
        fold()
        o_ref[...] = xres_ref[...] + acc_s[...].T


def _peer_mix(xnt, xres, u_bf, vt_bf, cnt, w0, rk, w1, tm, ti, tl):
    D, T = xnt.shape
    H, NK, _ = cnt.shape
    E = u_bf.shape[0]
    te = ti * NK
    n_e = E // te
    build = lambda j: jnp.minimum(j, n_e - 1)
    fold = lambda j: jnp.maximum(j - 1, 0)
    keys = pl.BlockSpec((H, ti, tm), lambda i, j: (0, build(j), i))
    full = pl.BlockSpec((H, NK, tm), lambda i, j: (0, 0, i))
    return pl.pallas_call(
        functools.partial(_peer_mix_body, tl=tl),
        grid=(T // tm, n_e + 1),
        in_specs=[pl.BlockSpec((D, tm), lambda i, j: (0, i)),
                  pl.BlockSpec((tm, D), lambda i, j: (i, 0)),
                  pl.BlockSpec((te, D), lambda i, j: (build(j), 0)),
                  pl.BlockSpec((D, te), lambda i, j: (0, fold(j))),
                  keys, keys, full, full],
        out_specs=pl.BlockSpec((tm, D), lambda i, j: (i, 0)),
        out_shape=jax.ShapeDtypeStruct((T, D), F32),
        scratch_shapes=[pltpu.VMEM((te, tm), F32), pltpu.VMEM((2, te, tm), BF16),
                        pltpu.VMEM((D, tm), F32)],
        compiler_params=_params("parallel", "arbitrary"),
        name="peer_mix",
    )(xnt, xres, u_bf, vt_bf, cnt, w0, rk, w1)


def _final_norm_body(x_ref, w_ref, o_ref):
    o_ref[...] = _rms(x_ref[...], w_ref[...])


def _final_norm(x2, w, tm):
    T, D = x2.shape
    return pl.pallas_call(
        _final_norm_body,
        grid=(T // tm,),
        in_specs=[pl.BlockSpec((tm, D), lambda i: (i, 0)), pl.BlockSpec((1, D), lambda i: (0, 0))],
        out_specs=pl.BlockSpec((tm, D), lambda i: (i, 0)),
        out_shape=jax.ShapeDtypeStruct((T, D), F32),
        compiler_params=_params("parallel"),
        name="final_norm",
    )(x2, w)


def _rope_table(pos, inv_freq):
    ang = pos[:, None] * inv_freq[None, :]
    ang = jnp.concatenate([ang, ang], axis=-1)
    return jnp.cos(ang), jnp.sin(ang)


def _position_tables(S):
    t = jnp.arange(S, dtype=F32)
    n_rows = S // GRID_W
    rows = jnp.repeat(jnp.arange(n_rows, dtype=F32), GRID_W)
    cols = jnp.tile(jnp.arange(GRID_W, dtype=F32), n_rows)
    ret_inv = 1.0 / (10000.0 ** jnp.linspace(0.0, 1.0, HEAD_DIM // 2, dtype=F32))
    ret_cos, ret_sin = _rope_table(t, ret_inv)
    ax_n = HEAD_DIM // 4
    ax_inv = ROPE_BASE ** (-jnp.arange(ax_n, dtype=F32) / ax_n)
    cr, sr = _rope_table(rows, ax_inv)
    cc, sc = _rope_table(cols, ax_inv)
    ax_cos = jnp.concatenate([cr, cc], axis=-1)
    ax_sin = jnp.concatenate([sr, sc], axis=-1)
    return (jnp.tile(ret_cos, (1, RET_HEADS)), jnp.tile(ret_sin, (1, RET_HEADS)),
            jnp.tile(ax_cos, (1, ATT_HEADS)), jnp.tile(ax_sin, (1, ATT_HEADS)))


def _tiles(B, S, n_keys):
    T = B * S
    return dict(
        tm=min(512, T),
        tb=min(256, S),
        tp=min(512, S),
        tq=min(256, S),
        tk=min(256, S),
        tr=min(512, T),
        tmix=min(512, T),
        ti=min(8, n_keys),
        tl=min(256, T),
    )


def kernel(x, ln1_w, w_in, ret_log_decay, ret_gn_w, lru_conv_w, lru_conv_b, lru_gate_w, lru_gate_b,
           lru_lambda, attn_q_norm, attn_k_norm, w_out, ln2_w, peer_wq, peer_keys, peer_u, peer_v,
           lnf_w):
    B, S, D = x.shape
    T = B * S
    depth = w_in.shape[0]
    n_keys = peer_keys.shape[3]
    tl = _tiles(B, S, n_keys)
    widths = (RET_W,) * 4 + (LRU_W,) * 2 + (ATT_W, KV_W, KV_W)
    ret_cos, ret_sin, ax_cos, ax_sin = _position_tables(S)
    x2 = x.reshape(T, D)
    for l in range(depth):
        rq, rk, rv, rg, lx, lgt, aq, ak, av = _inproj(
            x2, ln1_w[l].reshape(1, D), w_in[l].astype(BF16), widths, tl["tm"])
        o_ret = _retention(rq, rk, rv, rg, ret_cos, ret_sin, ret_log_decay[l],
                           ret_gn_w[l].reshape(1, RET_W), B, S)
        o_lru = _lru(lx, lgt, lru_conv_w[l], lru_conv_b[l], lru_gate_w[l], lru_gate_b[l],
                     lru_lambda[l], B, S, tl["tb"])
        qp, kt, vp = _attn_prep(aq, ak, av, ax_cos, ax_sin, attn_q_norm[l], attn_k_norm[l],
                                B, S, tl["tp"])
        o_att = _attention(qp, kt, vp, B, S, tl["tq"], tl["tk"])
        x2, xn, q = _outproj(x2, o_ret, o_lru, o_att, w_out[l].astype(BF16),
                             ln2_w[l].reshape(1, D), peer_wq[l].astype(BF16), tl["tm"])
        cnt, w0, rk, w1 = _peer_route(q, peer_keys[l], tl["tr"])
        x2 = _peer_mix(xn, x2, peer_u[l].astype(BF16), peer_v[l].T.astype(BF16),
                       cnt, w0, rk, w1, tl["tmix"], tl["ti"], tl["tl"])
    return _final_norm(x2, lnf_w.reshape(1, D), tl["tm"]).reshape(B, S, D)
```

```python
import functools

import jax
import jax.numpy as jnp
from jax import lax
from jax.experimental import pallas as pl
from jax.experimental.pallas import tpu as pltpu

F32 = jnp.float32
BF16 = jnp.bfloat16
EPS = 1e-6
LOG2_E = 1.4426950408889634

GRID_W = 64
CHUNK = 128
HEAD_DIM = 64
RET_HEADS = 4
RET_W = RET_HEADS * HEAD_DIM
LRU_W = 256
LRU_CONV = 4
LRU_C = 8.0
ATT_HEADS = 8
ATT_KV_HEADS = 2
ATT_GROUP = ATT_HEADS // ATT_KV_HEADS
ATT_W = ATT_HEADS * HEAD_DIM
KV_W = ATT_KV_HEADS * HEAD_DIM
ROPE_BASE = 10000.0
PEER_TOPK = 16
F32_SUBLANES = 8
BF16_SUBLANES = 16

VMEM_LIMIT_BYTES = 56 * 1024 * 1024
NEG_INF = float("-inf")


def _params(*sem):
    return pltpu.CompilerParams(dimension_semantics=sem, vmem_limit_bytes=VMEM_LIMIT_BYTES)


def _rms(x, w):
    return x * lax.rsqrt(jnp.mean(x * x, axis=-1, keepdims=True) + EPS) * w


def _gelu_tanh(x):
    return 0.5 * x * (1.0 + jnp.tanh(0.7978845608028654 * (x + 0.044715 * (x * x * x))))


def _gelu_tanh_x2(x):
    return x + x * jnp.tanh(x * (0.7978845608028654 + (0.7978845608028654 * 0.044715) * (x * x)))


def _sigmoid(x):
    return 1.0 / (1.0 + jnp.exp(-x))


def _nt_dot(a, b):
    return lax.dot_general(a, b, (((1,), (1,)), ((), ())), preferred_element_type=F32)


def _group_sum(x, member):
    hi = x.astype(BF16)
    lo = (x - hi.astype(F32)).astype(BF16)
    return (jnp.dot(hi, member, preferred_element_type=F32)
            + jnp.dot(lo, member, preferred_element_type=F32))


def _tn_dot(a, b):
    return lax.dot_general(a, b, (((0,), (0,)), ((), ())), preferred_element_type=F32)


def _inproj_body(x_ref, lnw_ref, w_ref, *out_refs, widths):
    hb = _rms(x_ref[...], lnw_ref[...]).astype(BF16)
    off = 0
    for o_ref, wd in zip(out_refs, widths):
        o_ref[...] = jnp.dot(hb, w_ref[:, off:off + wd],
                             preferred_element_type=F32).astype(o_ref.dtype)
        off += wd


def _inproj(x2, lnw, w_bf, widths, tm):
    T, D = x2.shape
    n_in = w_bf.shape[1]
    return pl.pallas_call(
        functools.partial(_inproj_body, widths=widths),
        grid=(T // tm,),
        in_specs=[pl.BlockSpec((tm, D), lambda i: (i, 0)),
                  pl.BlockSpec((1, D), lambda i: (0, 0)),
                  pl.BlockSpec((D, n_in), lambda i: (0, 0))],
        out_specs=[pl.BlockSpec((tm, wd), lambda i: (i, 0)) for wd in widths],
        out_shape=[jax.ShapeDtypeStruct((T, wd), F32) for wd in widths],
        compiler_params=_params("parallel"),
        name="inproj",
    )(x2, lnw, w_bf)


def _ret_body(ld_ref, q_ref, k_ref, v_ref, g_ref, cos_ref, sin_ref, lgl_ref, gnw_ref, o_ref,
              qr_s, kr_s, st_s, d_s, *, n_chunks):
    C, W = CHUNK, RET_W
    lane = lax.broadcasted_iota(jnp.int32, (1, W), 1)
    first_half = (lane % HEAD_DIM) < (HEAD_DIM // 2)
    head_masks = [(lane // HEAD_DIM) == h for h in range(RET_HEADS)]
    same_head = (lax.broadcasted_iota(jnp.int32, (W, W), 0) // HEAD_DIM
                 == lax.broadcasted_iota(jnp.int32, (W, W), 1) // HEAD_DIM)
    bd = same_head.astype(F32)
    lgf = lgl_ref[0:1, :]
    lgb = lgl_ref[1:2, :]
    idx = lax.broadcasted_iota(jnp.int32, (C, 1), 0).astype(F32)
    decf = jnp.exp(lgf * float(C))
    decb = jnp.exp(lgb * float(C))

    diff = (lax.broadcasted_iota(jnp.int32, (C, C), 0)
            - lax.broadcasted_iota(jnp.int32, (C, C), 1)).astype(F32)
    for h in range(RET_HEADS):
        d_s[h] = jnp.where(diff >= 0.0,
                           jnp.exp(ld_ref[0, h] * jnp.maximum(diff, 0.0)),
                           jnp.exp(ld_ref[1, h] * jnp.maximum(-diff, 0.0)))

    def rot_half(x):
        return jnp.where(first_half, -pltpu.roll(x, W - HEAD_DIM // 2, 1),
                         pltpu.roll(x, HEAD_DIM // 2, 1))

    def prep(c, carry):
        r0 = pl.multiple_of(c * C, C)
        cs = cos_ref[pl.ds(r0, C), :]
        sn = sin_ref[pl.ds(r0, C), :]
        q = q_ref[pl.ds(r0, C), :]
        k = k_ref[pl.ds(r0, C), :]
        qr_s[pl.ds(r0, C), :] = q * cs + rot_half(q) * sn
        kr_s[pl.ds(r0, C), :] = (k * cs + rot_half(k) * sn) * (HEAD_DIM ** -0.5)
        return carry

    lax.fori_loop(0, n_chunks, prep, 0, unroll=4)

    st_s[...] = jnp.zeros((W, W), F32)

    def fwd(c, carry):
        r0 = pl.multiple_of(c * C, C)
        q = qr_s[pl.ds(r0, C), :]
        k = kr_s[pl.ds(r0, C), :]
        v = v_ref[pl.ds(r0, C), :]
        st = st_s[...]
        qw = q * jnp.exp(lgf * (idx + 1.0))
        o_ref[pl.ds(r0, C), :] = jnp.dot(qw.astype(BF16), (st * bd).astype(BF16),
                                         preferred_element_type=F32)
        kw = k * jnp.exp(lgf * (float(C) - 1.0 - idx))
        st_s[...] = st * decf + _tn_dot(kw.astype(BF16), v.astype(BF16))
        return carry

    lax.fori_loop(0, n_chunks, fwd, 0, unroll=4)

    st_s[...] = jnp.zeros((W, W), F32)

    def bwd(t, carry):
        c = n_chunks - 1 - t
        r0 = pl.multiple_of(c * C, C)
        q = qr_s[pl.ds(r0, C), :]
        k = kr_s[pl.ds(r0, C), :]
        v = v_ref[pl.ds(r0, C), :]
        st = st_s[...]
        qw = q * jnp.exp(lgb * (float(C) - idx))
        vb = v.astype(BF16)
        kb = k.astype(BF16)
        o = o_ref[pl.ds(r0, C), :] + jnp.dot(qw.astype(BF16), (st * bd).astype(BF16),
                                              preferred_element_type=F32)
        kw = k * jnp.exp(lgb * idx)
        st_s[...] = st * decb + _tn_dot(kw.astype(BF16), vb)
        for h in range(RET_HEADS):
            qm = jnp.where(head_masks[h], q, 0.0).astype(BF16)
            s = _nt_dot(qm, kb) * d_s[h]
            oh = jnp.dot(s.astype(BF16), vb, preferred_element_type=F32)
            o = o + jnp.where(head_masks[h], oh, 0.0)
        ms = _group_sum(o * o, bd.astype(BF16)) * (1.0 / HEAD_DIM)
        on = o * lax.rsqrt(ms + EPS) * gnw_ref[...]
        g = g_ref[pl.ds(r0, C), :]
        o_ref[pl.ds(r0, C), :] = (g * _sigmoid(g)) * on
        return carry

    lax.fori_loop(0, n_chunks, bwd, 0, unroll=4)


def _retention(rq, rk, rv, rg, cos, sin, log_decay, gn_w, B, S):
    W = RET_W
    lgl = jnp.repeat(log_decay, HEAD_DIM, axis=1)
    seq = pl.BlockSpec((S, W), lambda b: (b, 0), pipeline_mode=pl.Buffered(1))
    tab = pl.BlockSpec((S, W), lambda b: (0, 0), pipeline_mode=pl.Buffered(1))
    return pl.pallas_call(
        functools.partial(_ret_body, n_chunks=S // CHUNK),
        grid=(B,),
        in_specs=[pl.BlockSpec(memory_space=pltpu.SMEM), seq, seq, seq, seq, tab, tab,
                  pl.BlockSpec((2, W), lambda b: (0, 0)),
                  pl.BlockSpec((1, W), lambda b: (0, 0))],
        out_specs=pl.BlockSpec((S, W), lambda b: (b, 0)),
        out_shape=jax.ShapeDtypeStruct((B * S, W), F32),
        scratch_shapes=[pltpu.VMEM((S, W), F32), pltpu.VMEM((S, W), F32),
                        pltpu.VMEM((W, W), F32), pltpu.VMEM((RET_HEADS, CHUNK, CHUNK), F32)],
        compiler_params=_params("parallel"),
        name="retention",
    )(log_decay, rq, rk, rv, rg, cos, sin, lgl, gn_w)


def _lru_body(x_ref, gt_ref, cw_ref, cb_ref, gw_ref, gb_ref, lam_ref, o_ref, xc_s, h_s,
              *, n_blocks, tb):
    W = LRU_W
    S = n_blocks * tb
    z = -lam_ref[...]
    softplus = jnp.maximum(z, 0.0) + jnp.log(1.0 + jnp.exp(-jnp.abs(z)))
    cl = -LRU_C * softplus
    row = lax.broadcasted_iota(jnp.int32, (tb, 1), 0)
    left = LRU_CONV // 2

    def conv_block(kb):
        r0 = pl.multiple_of(kb * tb, tb)
        halo = F32_SUBLANES
        prev = x_ref[pl.ds(pl.multiple_of(jnp.maximum(r0 - halo, 0), halo), halo), :]
        nxt = x_ref[pl.ds(pl.multiple_of(jnp.minimum(r0 + tb, S - halo), halo), halo), :]
        ext = jnp.concatenate([prev, x_ref[pl.ds(r0, tb), :], nxt], axis=0)
        t = row + r0
        acc = cb_ref[...] + jnp.zeros((tb, W), F32)
        for j in range(LRU_CONV):
            off = j - left
            if off == 0:
                xs = ext[halo:halo + tb]
            else:
                xs = pltpu.roll(ext, (-off) % (tb + 2 * halo), 0)[halo:halo + tb]
                xs = jnp.where((t + off >= 0) & (t + off < S), xs, 0.0)
            acc = acc + xs * cw_ref[j:j + 1, :]
        return acc

    def gates(xc, d):
        g = jnp.dot(xc.astype(BF16), gw_ref[:, d * 2 * W:(d + 1) * 2 * W],
                    preferred_element_type=F32) + gb_ref[:, d * 2 * W:(d + 1) * 2 * W]
        r = _sigmoid(g[:, :W])
        i = _sigmoid(g[:, W:])
        log_a = cl[d:d + 1, :] * r
        a = jnp.exp(log_a)
        b = jnp.sqrt(-jnp.tanh(log_a) * (a * a + 1.0)) * (i * xc)
        return a, b

    def scan_block(a, b, reverse):
        d = 1
        while d < tb:
            if reverse:
                keep = row < tb - d
                a_sh = jnp.where(keep, pltpu.roll(a, tb - d, 0), 1.0)
                b_sh = jnp.where(keep, pltpu.roll(b, tb - d, 0), 0.0)
            else:
                keep = row >= d
                a_sh = jnp.where(keep, pltpu.roll(a, d, 0), 1.0)
                b_sh = jnp.where(keep, pltpu.roll(b, d, 0), 0.0)
            b = a * b_sh + b
            a = a * a_sh
            d *= 2
        return a, b

    def fwd(kb, carry):
        r0 = pl.multiple_of(kb * tb, tb)
        xc = conv_block(kb)
        xc_s[pl.ds(r0, tb), :] = xc
        a, b = gates(xc, 0)
        a, b = scan_block(a, b, False)
        h = a * carry + b
        h_s[pl.ds(r0, tb), :] = h
        return h[tb - 1:tb, :]

    lax.fori_loop(0, n_blocks, fwd, jnp.zeros((1, W), F32))

    def bwd(t, carry):
        kb = n_blocks - 1 - t
        r0 = pl.multiple_of(kb * tb, tb)
        xc = xc_s[pl.ds(r0, tb), :]
        a, b = gates(xc, 1)
        a, b = scan_block(a, b, True)
        h = a * carry + b
        o_ref[pl.ds(r0, tb), :] = (h_s[pl.ds(r0, tb), :] + h) * _gelu_tanh(gt_ref[pl.ds(r0, tb), :])
        return h[0:1, :]

    lax.fori_loop(0, n_blocks, bwd, jnp.zeros((1, W), F32))


def _lru(lx, lgt, conv_w, conv_b, gate_w, gate_b, lam, B, S, tb):
    W = LRU_W
    nb = W // HEAD_DIM
    eye = jnp.eye(nb, dtype=F32)
    gw = jnp.einsum('dgnkm,nj->nkdgjm', gate_w, eye).reshape(W, 4 * W).astype(BF16)
    gb = gate_b.reshape(1, 4 * W)
    seq = pl.BlockSpec((S, W), lambda b: (b, 0))
    const = lambda shape: pl.BlockSpec(shape, lambda b: (0,) * len(shape))
    return pl.pallas_call(
        functools.partial(_lru_body, n_blocks=S // tb, tb=tb),
        grid=(B,),
        in_specs=[seq, seq, const((LRU_CONV, W)), const((1, W)), const((W, 4 * W)),
                  const((1, 4 * W)), const((2, W))],
        out_specs=seq,
        out_shape=jax.ShapeDtypeStruct((B * S, W), F32),
        scratch_shapes=[pltpu.VMEM((S, W), F32), pltpu.VMEM((S, W), F32)],
        compiler_params=_params("parallel"),
        name="rglru",
    )(lx, lgt, conv_w, conv_b.reshape(1, W), gw, gb, lam)


def _attn_prep_body(q_ref, k_ref, v_ref, cos_ref, sin_ref, qnw_ref, knw_ref, vrep_ref, rept_ref,
                    qo_ref, kt_ref, vo_ref):
    def head_norm_rot(x, w, cs, sn):
        W = x.shape[-1]
        lane = lax.broadcasted_iota(jnp.int32, (1, W), 1)
        same_head = (lax.broadcasted_iota(jnp.int32, (W, W), 0) // HEAD_DIM
                     == lax.broadcasted_iota(jnp.int32, (W, W), 1) // HEAD_DIM)
        ms = _group_sum(x * x, same_head.astype(BF16)) * (1.0 / HEAD_DIM)
        xn = x * lax.rsqrt(ms + EPS) * w
        quarter = HEAD_DIM // 4
        rot = jnp.where((lane % (2 * quarter)) < quarter, -pltpu.roll(xn, W - quarter, 1),
                        pltpu.roll(xn, quarter, 1))
        return xn * cs + rot * sn

    cs = cos_ref[...]
    sn = sin_ref[...]
    q = head_norm_rot(q_ref[...], qnw_ref[...], cs, sn)
    qo_ref[...] = (q * (HEAD_DIM ** -0.5 * LOG2_E)).astype(BF16)
    k = head_norm_rot(k_ref[...], knw_ref[...], cs[:, :KV_W], sn[:, :KV_W]).astype(BF16)
    kt_ref[...] = _nt_dot(rept_ref[...], k).astype(BF16)
    VW = 2 * HEAD_DIM
    vlane = lax.broadcasted_iota(jnp.int32, (1, ATT_KV_HEADS * VW), 1)
    ones = jnp.where(vlane % VW >= HEAD_DIM, 1.0, 0.0)
    vo_ref[...] = (jnp.dot(v_ref[...].astype(BF16), vrep_ref[...], preferred_element_type=F32)
                   + ones).astype(BF16)


def _attn_prep(aq, ak, av, cos, sin, qn_w, kn_w, B, S, tm):
    T = B * S
    ns = S // tm
    GW = ATT_GROUP * HEAD_DIM
    src = jnp.arange(ATT_KV_HEADS * GW)
    src = (src // GW) * HEAD_DIM + src % HEAD_DIM
    rep = (jnp.arange(KV_W)[:, None] == src[None, :]).astype(BF16)
    VW = 2 * HEAD_DIM
    vsrc = jnp.arange(ATT_KV_HEADS * VW)
    vsrc = jnp.where(vsrc % VW < HEAD_DIM, (vsrc // VW) * HEAD_DIM + vsrc % VW, -1)
    vrep = (jnp.arange(KV_W)[:, None] == vsrc[None, :]).astype(BF16)
    return pl.pallas_call(
        _attn_prep_body,
        grid=(B, ns),
        in_specs=[pl.BlockSpec((tm, ATT_W), lambda b, s: (b * ns + s, 0)),
                  pl.BlockSpec((tm, KV_W), lambda b, s: (b * ns + s, 0)),
                  pl.BlockSpec((tm, KV_W), lambda b, s: (b * ns + s, 0)),
                  pl.BlockSpec((tm, ATT_W), lambda b, s: (s, 0)),
                  pl.BlockSpec((tm, ATT_W), lambda b, s: (s, 0)),
                  pl.BlockSpec((1, ATT_W), lambda b, s: (0, 0)),
                  pl.BlockSpec((1, KV_W), lambda b, s: (0, 0)),
                  pl.BlockSpec((KV_W, ATT_KV_HEADS * VW), lambda b, s: (0, 0)),
                  pl.BlockSpec((ATT_KV_HEADS * GW, KV_W), lambda b, s: (0, 0))],
        out_specs=[pl.BlockSpec((tm, ATT_W), lambda b, s: (b * ns + s, 0)),
                   pl.BlockSpec((None, ATT_KV_HEADS * GW, tm), lambda b, s: (b, 0, s)),
                   pl.BlockSpec((tm, ATT_KV_HEADS * VW), lambda b, s: (b * ns + s, 0))],
        out_shape=[jax.ShapeDtypeStruct((T, ATT_W), BF16),
                   jax.ShapeDtypeStruct((B, ATT_KV_HEADS * GW, S), BF16),
                   jax.ShapeDtypeStruct((T, ATT_KV_HEADS * VW), BF16)],
        compiler_params=_params("parallel", "parallel"),
        name="attn_prep",
    )(aq, ak, av, cos, sin, jnp.tile(qn_w, ATT_HEADS).reshape(1, ATT_W),
      jnp.tile(kn_w, ATT_KV_HEADS).reshape(1, KV_W), vrep, rep.T)


def _attn_body(q_ref, kt_ref, v_ref, o_ref, *, tq, tk, nk):
    GW = ATT_GROUP * HEAD_DIM
    lane = lax.broadcasted_iota(jnp.int32, (1, GW), 1)
    masks = [(lane // HEAD_DIM) == g for g in range(ATT_GROUP)]
    q = q_ref[...]
    zero = jnp.zeros_like(q)
    qs = jnp.concatenate([jnp.where(m, q, zero) for m in masks], axis=0)
    rows = ATT_GROUP * tq
    VW = 2 * HEAD_DIM
    m_run = jnp.full((rows, 1), NEG_INF, F32)
    acc = jnp.zeros((rows, VW), F32)
    for c in range(nk):
        s = jnp.dot(qs, kt_ref[:, c * tk:(c + 1) * tk], preferred_element_type=F32)
        m_new = jnp.maximum(m_run, jnp.max(s, axis=-1, keepdims=True))
        alpha = jnp.exp2(m_run - m_new)
        p = jnp.exp2(s - m_new)
        acc = alpha * acc + jnp.dot(p.astype(BF16), v_ref[c * tk:(c + 1) * tk, :],
                                    preferred_element_type=F32)
        m_run = m_new
    on = acc * pltpu.roll(1.0 / acc, HEAD_DIM, 1)
    low = lax.broadcasted_iota(jnp.int32, (1, VW), 1) < HEAD_DIM
    heads = [on[g * tq:(g + 1) * tq] for g in range(ATT_GROUP)]
    for pair in range(ATT_GROUP // 2):
        o_ref[:, pair * VW:(pair + 1) * VW] = jnp.where(
            low, heads[2 * pair], pltpu.roll(heads[2 * pair + 1], HEAD_DIM, 1))


def _attention(qp, kt, vp, B, S, tq, tk):
    T = B * S
    nq = S // tq
    GW = ATT_GROUP * HEAD_DIM
    return pl.pallas_call(
        functools.partial(_attn_body, tq=tq, tk=tk, nk=S // tk),
        grid=(B, ATT_KV_HEADS, nq),
        in_specs=[pl.BlockSpec((tq, GW), lambda b, h, i: (b * nq + i, h)),
                  pl.BlockSpec((None, GW, S), lambda b, h, i: (b, h, 0)),
                  pl.BlockSpec((S, 2 * HEAD_DIM), lambda b, h, i: (b, h))],
        out_specs=pl.BlockSpec((tq, GW), lambda b, h, i: (b * nq + i, h)),
        out_shape=jax.ShapeDtypeStruct((T, ATT_W), F32),
        compiler_params=_params("parallel", "parallel", "parallel"),
        name="attention",
    )(qp, kt, vp)


def _outproj_body(x_ref, oret_ref, olru_ref, oatt_ref, w_ref, ln2_ref, wq_ref,
                  xo_ref, xn_ref, q_ref):
    y = jnp.dot(oret_ref[...].astype(BF16), w_ref[0:RET_W, :], preferred_element_type=F32)
    y = y + jnp.dot(olru_ref[...].astype(BF16), w_ref[RET_W:RET_W + LRU_W, :],
                    preferred_element_type=F32)
    y = y + jnp.dot(oatt_ref[...].astype(BF16), w_ref[RET_W + LRU_W:, :],
                    preferred_element_type=F32)
    x = x_ref[...] + y
    xo_ref[...] = x
    xn = _rms(x, ln2_ref[...]).astype(BF16)
    xn_ref[...] = xn.astype(F32).T.astype(BF16)
    q_ref[...] = jnp.dot(xn, wq_ref[...], preferred_element_type=F32)


def _outproj(x2, o_ret, o_lru, o_att, w_out_bf, ln2, wq_bf, tm):
    T, D = x2.shape
    QW = wq_bf.shape[1]
    MW = w_out_bf.shape[0]
    tok = lambda w: pl.BlockSpec((tm, w), lambda i: (i, 0))
    return pl.pallas_call(
        _outproj_body,
        grid=(T // tm,),
        in_specs=[tok(D), tok(RET_W), tok(LRU_W), tok(ATT_W),
                  pl.BlockSpec((MW, D), lambda i: (0, 0)),
                  pl.BlockSpec((1, D), lambda i: (0, 0)),
                  pl.BlockSpec((D, QW), lambda i: (0, 0))],
        out_specs=[tok(D), pl.BlockSpec((D, tm), lambda i: (0, i)), tok(QW)],
        out_shape=[jax.ShapeDtypeStruct((T, D), F32), jax.ShapeDtypeStruct((D, T), BF16),
                   jax.ShapeDtypeStruct((T, QW), F32)],
        compiler_params=_params("parallel"),
        name="outproj",
    )(x2, o_ret, o_lru, o_att, w_out_bf, ln2, wq_bf)


def _sorting_network(n):
    pairs = []
    p = 1
    while p < n:
        k = p
        while k >= 1:
            for j in range(k % p, n - k, 2 * k):
                for i in range(min(k, n - j - k)):
                    if (i + j) // (2 * p) == (i + j + k) // (2 * p):
                        pairs.append((i + j, i + j + k))
            k //= 2
        p *= 2
    return pairs


def _pair_counts(n):
    return [n // (i + 1) for i in range(n)]


def _peer_route_body(q_ref, keys_ref, cnt_ref, w0_ref, rk_ref, w1_ref, a_s, b_s, f_s,
                     *, half, topk):
    s0 = _nt_dot(keys_ref[0], q_ref[:, :half])
    s1 = _nt_dot(keys_ref[1], q_ref[:, half:])

    def top_sorted(tiles, n, out_ref):
        rows = list(tiles) + [None] * (pl.next_power_of_2(len(tiles)) - len(tiles))
        for lo_i, hi_i in _sorting_network(len(rows)):
            x, y = rows[lo_i], rows[hi_i]
            if y is None:
                continue
            if x is None:
                rows[lo_i], rows[hi_i] = y, None
            else:
                rows[lo_i], rows[hi_i] = jnp.maximum(x, y), jnp.minimum(x, y)
        rows = [t for t in rows if t is not None]
        for r in range(n):
            m = jnp.max(rows[0], axis=0, keepdims=True)
            out_ref[r:r + 1, :] = m
            hit = rows[0] == m
            for v in range(min(len(rows), n - r - 1)):
                nxt = rows[v + 1] if v + 1 < len(rows) else NEG_INF
                rows[v] = jnp.where(hit, nxt, rows[v])

    def sublane_tiles(x):
        sl = F32_SUBLANES
        return [x[sl * v:sl * (v + 1), :] for v in range(x.shape[0] // sl)]

    n = topk + 1
    top_sorted(sublane_tiles(s0), n, a_s)
    top_sorted(sublane_tiles(s1), n, b_s)
    a = a_s[0:topk, :]
    b = b_s[0:topk, :]
    counts = _pair_counts(n)
    sl = F32_SUBLANES
    row8 = lax.broadcasted_iota(jnp.int32, (sl, 1), 0)
    parts = [a[0:1, :] + b[0:sl, :], a[0:1, :] + b[sl:2 * sl, :]]
    i = 1
    while counts[i] > 1:
        parts.append(jnp.where(row8 < counts[i], a[i:i + 1, :] + b[0:sl, :], NEG_INF))
        i += 1
    parts.append(a[i:, :] + b[0:1, :])
    last = jnp.where(row8 == 0, a[0:1, :] + b_s[topk:n, :],
                     jnp.where(row8 == 1, a_s[topk:n, :] + b[0:1, :], NEG_INF))
    parts.append(last)
    top_sorted(parts, n, f_s)
    f = f_s[0:topk, :]
    z = jnp.sum(jnp.exp(f - f[0:1, :]), axis=0, keepdims=True)
    tau = 0.5 * (f[topk - 1:topk, :] + f_s[topk:n, :])
    thr = tau - s0
    cnt = jnp.zeros_like(thr)
    for r in range(topk):
        cnt = jnp.where(b[r:r + 1, :] >= thr, float(r + 1), cnt)
    cnt_ref[...] = cnt
    rank1 = jnp.zeros_like(s1)
    for r in range(topk):
        rank1 = jnp.where(b[r:r + 1, :] > s1, float(r + 1), rank1)
    w0_ref[...] = jnp.exp(s0 - a[0:1, :]) * (0.5 / z)
    rk_ref[...] = rank1.astype(BF16)
    w1_ref[...] = jnp.exp(s1 - b[0:1, :]).astype(BF16)


def _peer_route(q, keys, tm):
    T = q.shape[0]
    H, _, NK, half = keys.shape
    sl = F32_SUBLANES
    counts = _pair_counts(PEER_TOPK + 1)
    assert PEER_TOPK == 2 * sl and counts[1] <= sl and counts[sl - 1] > 1 >= counts[sl]
    assert NK > PEER_TOPK and NK % sl == 0
    per_key = jax.ShapeDtypeStruct((H, NK, T), F32)
    per_col = jax.ShapeDtypeStruct((H, NK, T), BF16)
    ospec = pl.BlockSpec((None, NK, tm), lambda i, h: (h, 0, i))
    return pl.pallas_call(
        functools.partial(_peer_route_body, half=half, topk=PEER_TOPK),
        grid=(T // tm, H),
        in_specs=[pl.BlockSpec((tm, 2 * half), lambda i, h: (i, h)),
                  pl.BlockSpec((None, 2, NK, half), lambda i, h: (h, 0, 0, 0))],
        out_specs=[ospec, ospec, ospec, ospec],
        out_shape=[per_key, per_key, per_col, per_col],
        scratch_shapes=[pltpu.VMEM((PEER_TOPK + sl, tm), F32)] * 3,
        compiler_params=_params("parallel", "parallel"),
        name="peer_route",
    )(q, keys)


def _peer_mix_body(xnt_ref, xres_ref, u_ref, vt_ref, cnt_ref, w0_ref, rk_ref, w1_ref, o_ref,
                   h_s, gh_s, acc_s, *, tl):
    H, ti, tm = cnt_ref.shape
    NK = rk_ref.shape[1]
    j = pl.program_id(1)

    last = pl.num_programs(1) - 1
    cur = j % 2

    def fold():
        acc_s[...] += jnp.dot(vt_ref[...], gh_s[1 - cur], preferred_element_type=F32)

    def row_tile(row):
        tile = jnp.broadcast_to(row, (BF16_SUBLANES, row.shape[1])).astype(BF16)
        return pltpu.repeat(tile, NK // BF16_SUBLANES, axis=0)

    def build():
        for ii in range(ti):
            rs = slice(ii * NK, (ii + 1) * NK)
            for lc in range(tm // tl):
                ls = slice(lc * tl, (lc + 1) * tl)
                g = None
                for h in range(H):
                    c = row_tile(cnt_ref[h, ii:ii + 1, ls])
                    a = row_tile(w0_ref[h, ii:ii + 1, ls])
                    term = a * jnp.where(rk_ref[h, :, ls] < c, w1_ref[h, :, ls], 0)
                    g = term if g is None else g + term
                gh_s[cur, rs, ls] = g
        h_s[...] = jnp.dot(u_ref[...], xnt_ref[...], preferred_element_type=F32)
        for ii in range(ti):
            rs = slice(ii * NK, (ii + 1) * NK)
            gh_s[cur, rs, :] = gh_s[cur, rs, :] * _gelu_tanh_x2(h_s[rs, :].astype(BF16))

    @pl.when(j == 0)
    def _():
        acc_s[...] = jnp.zeros_like(acc_s)
        build()

    @pl.when(jnp.logical_and(j > 0, j < last))
    def _():
        fold()
        build()

    @pl.when(j == last)
    def _():
        fold()
        o_ref[...] = xres_ref[...] + acc_s[...].T


def _peer_mix(xnt, xres, u_bf, vt_bf, cnt, w0, rk, w1, tm, ti, tl):
    D, T = xnt.shape
    H, NK, _ = cnt.shape
    E = u_bf.shape[0]
    te = ti * NK
    n_e = E // te
    build = lambda j: jnp.minimum(j, n_e - 1)
    fold = lambda j: jnp.maximum(j - 1, 0)
    keys = pl.BlockSpec((H, ti, tm), lambda i, j: (0, build(j), i))
    full = pl.BlockSpec((H, NK, tm), lambda i, j: (0, 0, i))
    return pl.pallas_call(
        functools.partial(_peer_mix_body, tl=tl),
        grid=(T // tm, n_e + 1),
        in_specs=[pl.BlockSpec((D, tm), lambda i, j: (0, i)),
                  pl.BlockSpec((tm, D), lambda i, j: (i, 0)),
                  pl.BlockSpec((te, D), lambda i, j: (build(j), 0)),
                  pl.BlockSpec((D, te), lambda i, j: (0, fold(j))),
                  keys, keys, full, full],
        out_specs=pl.BlockSpec((tm, D), lambda i, j: (i, 0)),
        out_shape=jax.ShapeDtypeStruct((T, D), F32),
        scratch_shapes=[pltpu.VMEM((te, tm), F32), pltpu.VMEM((2, te, tm), BF16),
                        pltpu.VMEM((D, tm), F32)],
        compiler_params=_params("parallel", "arbitrary"),
        name="peer_mix",
    )(xnt, xres, u_bf, vt_bf, cnt, w0, rk, w1)


def _final_norm_body(x_ref, w_ref, o_ref):
    o_ref[...] = _rms(x_ref[...], w_ref[...])


def _final_norm(x2, w, tm):
    T, D = x2.shape
    return pl.pallas_call(
        _final_norm_body,
        grid=(T // tm,),
        in_specs=[pl.BlockSpec((tm, D), lambda i: (i, 0)), pl.BlockSpec((1, D), lambda i: (0, 0))],
        out_specs=pl.BlockSpec((tm, D), lambda i: (i, 0)),
        out_shape=jax.ShapeDtypeStruct((T, D), F32),
        compiler_params=_params("parallel"),
        name="final_norm",
    )(x2, w)


def _rope_table(pos, inv_freq):
    ang = pos[:, None] * inv_freq[None, :]
    ang = jnp.concatenate([ang, ang], axis=-1)
    return jnp.cos(ang), jnp.sin(ang)


def _position_tables(S):
    t = jnp.arange(S, dtype=F32)
    n_rows = S // GRID_W
    rows = jnp.repeat(jnp.arange(n_rows, dtype=F32), GRID_W)
    cols = jnp.tile(jnp.arange(GRID_W, dtype=F32), n_rows)
    ret_inv = 1.0 / (10000.0 ** jnp.linspace(0.0, 1.0, HEAD_DIM // 2, dtype=F32))
    ret_cos, ret_sin = _rope_table(t, ret_inv)
    ax_n = HEAD_DIM // 4
    ax_inv = ROPE_BASE ** (-jnp.arange(ax_n, dtype=F32) / ax_n)
    cr, sr = _rope_table(rows, ax_inv)
    cc, sc = _rope_table(cols, ax_inv)
    ax_cos = jnp.concatenate([cr, cc], axis=-1)
    ax_sin = jnp.concatenate([sr, sc], axis=-1)
    return (jnp.tile(ret_cos, (1, RET_HEADS)), jnp.tile(ret_sin, (1, RET_HEADS)),
            jnp.tile(ax_cos, (1, ATT_HEADS)), jnp.tile(ax_sin, (1, ATT_HEADS)))


def _tiles(B, S, n_keys):
    T = B * S
    return dict(
        tm=min(512, T),
        tb=min(256, S),
        tp=min(512, S),
        tq=min(256, S),
        tk=min(256, S),
        tr=min(512, T),
        tmix=min(512, T),
        ti=min(8, n_keys),
        tl=min(256, T),
    )


def kernel(x, ln1_w, w_in, ret_log_decay, ret_gn_w, lru_conv_w, lru_conv_b, lru_gate_w, lru_gate_b,
           lru_lambda, attn_q_norm, attn_k_norm, w_out, ln2_w, peer_wq, peer_keys, peer_u, peer_v,
           lnf_w):
    B, S, D = x.shape
    T = B * S
    depth = w_in.shape[0]
    n_keys = peer_keys.shape[3]
    tl = _tiles(B, S, n_keys)
    widths = (RET_W,) * 4 + (LRU_W,) * 2 + (ATT_W, KV_W, KV_W)
    ret_cos, ret_sin, ax_cos, ax_sin = _position_tables(S)
    x2 = x.reshape(T, D)
    for l in range(depth):
        rq, rk, rv, rg, lx, lgt, aq, ak, av = _inproj(
            x2, ln1_w[l].reshape(1, D), w_in[l].astype(BF16), widths, tl["tm"])
        o_ret = _retention(rq, rk, rv, rg, ret_cos, ret_sin, ret_log_decay[l],
                           ret_gn_w[l].reshape(1, RET_W), B, S)
        o_lru = _lru(lx, lgt, lru_conv_w[l], lru_conv_b[l], lru_gate_w[l], lru_gate_b[l],
                     lru_lambda[l], B, S, tl["tb"])
        qp, kt, vp = _attn_prep(aq, ak, av, ax_cos, ax_sin, attn_q_norm[l], attn_k_norm[l],
                                B, S, tl["tp"])
        o_att = _attention(qp, kt, vp, B, S, tl["tq"], tl["tk"])
        x2, xn, q = _outproj(x2, o_ret, o_lru, o_att, w_out[l].astype(BF16),
                             ln2_w[l].reshape(1, D), peer_wq[l].astype(BF16), tl["tm"])
        cnt, w0, rk, w1 = _peer_route(q, peer_keys[l], tl["tr"])
        x2 = _peer_mix(xn, x2, peer_u[l].astype(BF16), peer_v[l].T.astype(BF16),
                       cnt, w0, rk, w1, tl["tmix"], tl["ti"], tl["tl"])
    return _final_norm(x2, lnf_w.reshape(1, D), tl["tm"]).reshape(B, S, D)
```

```python
import functools

import jax
import jax.numpy as jnp
from jax import lax
from jax.experimental import pallas as pl
from jax.experimental.pallas import tpu as pltpu

F32 = jnp.float32
BF16 = jnp.bfloat16
EPS = 1e-6
LOG2_E = 1.4426950408889634

GRID_W = 64
CHUNK = 128
HEAD_DIM = 64
RET_HEADS = 4
RET_W = RET_HEADS * HEAD_DIM
LRU_W = 256
LRU_CONV = 4
LRU_C = 8.0
ATT_HEADS = 8
ATT_KV_HEADS = 2
ATT_GROUP = ATT_HEADS // ATT_KV_HEADS
ATT_W = ATT_HEADS * HEAD_DIM
KV_W = ATT_KV_HEADS * HEAD_DIM
ROPE_BASE = 10000.0
PEER_TOPK = 16
F32_SUBLANES = 8
BF16_SUBLANES = 16

VMEM_LIMIT_BYTES = 56 * 1024 * 1024
NEG_INF = float("-inf")


def _params(*sem):
    return pltpu.CompilerParams(dimension_semantics=sem, vmem_limit_bytes=VMEM_LIMIT_BYTES)


def _rms(x, w):
    return x * lax.rsqrt(jnp.mean(x * x, axis=-1, keepdims=True) + EPS) * w


def _gelu_tanh(x):
    return 0.5 * x * (1.0 + jnp.tanh(0.7978845608028654 * (x + 0.044715 * (x * x * x))))


def _gelu_tanh_x2(x):
    return x + x * jnp.tanh(x * (0.7978845608028654 + (0.7978845608028654 * 0.044715) * (x * x)))


def _sigmoid(x):
    return 1.0 / (1.0 + jnp.exp(-x))


def _nt_dot(a, b):
    return lax.dot_general(a, b, (((1,), (1,)), ((), ())), preferred_element_type=F32)


def _group_sum(x, member):
    hi = x.astype(BF16)
    lo = (x - hi.astype(F32)).astype(BF16)
    return (jnp.dot(hi, member, preferred_element_type=F32)
            + jnp.dot(lo, member, preferred_element_type=F32))


def _tn_dot(a, b):
    return lax.dot_general(a, b, (((0,), (0,)), ((), ())), preferred_element_type=F32)


def _inproj_body(x_ref, lnw_ref, w_ref, *out_refs, widths):
    hb = _rms(x_ref[...], lnw_ref[...]).astype(BF16)
    off = 0
    for o_ref, wd in zip(out_refs, widths):
        o_ref[...] = jnp.dot(hb, w_ref[:, off:off + wd],
                             preferred_element_type=F32).astype(o_ref.dtype)
        off += wd


def _inproj(x2, lnw, w_bf, widths, tm):
    T, D = x2.shape
    n_in = w_bf.shape[1]
    return pl.pallas_call(
        functools.partial(_inproj_body, widths=widths),
        grid=(T // tm,),
        in_specs=[pl.BlockSpec((tm, D), lambda i: (i, 0)),
                  pl.BlockSpec((1, D), lambda i: (0, 0)),
                  pl.BlockSpec((D, n_in), lambda i: (0, 0))],
        out_specs=[pl.BlockSpec((tm, wd), lambda i: (i, 0)) for wd in widths],
        out_shape=[jax.ShapeDtypeStruct((T, wd), F32) for wd in widths],
        compiler_params=_params("parallel"),
        name="inproj",
    )(x2, lnw, w_bf)


def _ret_body(ld_ref, q_ref, k_ref, v_ref, g_ref, cos_ref, sin_ref, lgl_ref, gnw_ref, o_ref,
              qr_s, kr_s, st_s, d_s, *, n_chunks):
    C, W = CHUNK, RET_W
    lane = lax.broadcasted_iota(jnp.int32, (1, W), 1)
    first_half = (lane % HEAD_DIM) < (HEAD_DIM // 2)
    head_masks = [(lane // HEAD_DIM) == h for h in range(RET_HEADS)]
    same_head = (lax.broadcasted_iota(jnp.int32, (W, W), 0) // HEAD_DIM
                 == lax.broadcasted_iota(jnp.int32, (W, W), 1) // HEAD_DIM)
    bd = same_head.astype(F32)
    lgf = lgl_ref[0:1, :]
    lgb = lgl_ref[1:2, :]
    idx = lax.broadcasted_iota(jnp.int32, (C, 1), 0).astype(F32)
    decf = jnp.exp(lgf * float(C))
    decb = jnp.exp(lgb * float(C))

    diff = (lax.broadcasted_iota(jnp.int32, (C, C), 0)
            - lax.broadcasted_iota(jnp.int32, (C, C), 1)).astype(F32)
    for h in range(RET_HEADS):
        d_s[h] = jnp.where(diff >= 0.0,
                           jnp.exp(ld_ref[0, h] * jnp.maximum(diff, 0.0)),
                           jnp.exp(ld_ref[1, h] * jnp.maximum(-diff, 0.0)))

    def rot_half(x):
        return jnp.where(first_half, -pltpu.roll(x, W - HEAD_DIM // 2, 1),
                         pltpu.roll(x, HEAD_DIM // 2, 1))

    def prep(c, carry):
        r0 = pl.multiple_of(c * C, C)
        cs = cos_ref[pl.ds(r0, C), :]
        sn = sin_ref[pl.ds(r0, C), :]
        q = q_ref[pl.ds(r0, C), :]
        k = k_ref[pl.ds(r0, C), :]
        qr_s[pl.ds(r0, C), :] = q * cs + rot_half(q) * sn
        kr_s[pl.ds(r0, C), :] = (k * cs + rot_half(k) * sn) * (HEAD_DIM ** -0.5)
        return carry

    lax.fori_loop(0, n_chunks, prep, 0, unroll=4)

    st_s[...] = jnp.zeros((W, W), F32)

    def fwd(c, carry):
        r0 = pl.multiple_of(c * C, C)
        q = qr_s[pl.ds(r0, C), :]
        k = kr_s[pl.ds(r0, C), :]
        v = v_ref[pl.ds(r0, C), :]
        st = st_s[...]
        qw = q * jnp.exp(lgf * (idx + 1.0))
        o_ref[pl.ds(r0, C), :] = jnp.dot(qw.astype(BF16), (st * bd).astype(BF16),
                                         preferred_element_type=F32)
        kw = k * jnp.exp(lgf * (float(C) - 1.0 - idx))
        st_s[...] = st * decf + _tn_dot(kw.astype(BF16), v.astype(BF16))
        return carry

    lax.fori_loop(0, n_chunks, fwd, 0, unroll=4)

    st_s[...] = jnp.zeros((W, W), F32)

    def bwd(t, carry):
        c = n_chunks - 1 - t
        r0 = pl.multiple_of(c * C, C)
        q = qr_s[pl.ds(r0, C), :]
        k = kr_s[pl.ds(r0, C), :]
        v = v_ref[pl.ds(r0, C), :]
        st = st_s[...]
        qw = q * jnp.exp(lgb * (float(C) - idx))
        vb = v.astype(BF16)
        kb = k.astype(BF16)
        o = o_ref[pl.ds(r0, C), :] + jnp.dot(qw.astype(BF16), (st * bd).astype(BF16),
                                              preferred_element_type=F32)
        kw = k * jnp.exp(lgb * idx)
        st_s[...] = st * decb + _tn_dot(kw.astype(BF16), vb)
        for h in range(RET_HEADS):
            qm = jnp.where(head_masks[h], q, 0.0).astype(BF16)
            s = _nt_dot(qm, kb) * d_s[h]
            oh = jnp.dot(s.astype(BF16), vb, preferred_element_type=F32)
            o = o + jnp.where(head_masks[h], oh, 0.0)
        ms = _group_sum(o * o, bd.astype(BF16)) * (1.0 / HEAD_DIM)
        on = o * lax.rsqrt(ms + EPS) * gnw_ref[...]
        g = g_ref[pl.ds(r0, C), :]
        o_ref[pl.ds(r0, C), :] = (g * _sigmoid(g)) * on
        return carry

    lax.fori_loop(0, n_chunks, bwd, 0, unroll=4)


def _retention(rq, rk, rv, rg, cos, sin, log_decay, gn_w, B, S):
    W = RET_W
    lgl = jnp.repeat(log_decay, HEAD_DIM, axis=1)
    seq = pl.BlockSpec((S, W), lambda b: (b, 0), pipeline_mode=pl.Buffered(1))
    tab = pl.BlockSpec((S, W), lambda b: (0, 0), pipeline_mode=pl.Buffered(1))
    return pl.pallas_call(
        functools.partial(_ret_body, n_chunks=S // CHUNK),
        grid=(B,),
        in_specs=[pl.BlockSpec(memory_space=pltpu.SMEM), seq, seq, seq, seq, tab, tab,
                  pl.BlockSpec((2, W), lambda b: (0, 0)),
                  pl.BlockSpec((1, W), lambda b: (0, 0))],
        out_specs=pl.BlockSpec((S, W), lambda b: (b, 0)),
        out_shape=jax.ShapeDtypeStruct((B * S, W), F32),
        scratch_shapes=[pltpu.VMEM((S, W), F32), pltpu.VMEM((S, W), F32),
                        pltpu.VMEM((W, W), F32), pltpu.VMEM((RET_HEADS, CHUNK, CHUNK), F32)],
        compiler_params=_params("parallel"),
        name="retention",
    )(log_decay, rq, rk, rv, rg, cos, sin, lgl, gn_w)


def _lru_body(x_ref, gt_ref, cw_ref, cb_ref, gw_ref, gb_ref, lam_ref, o_ref, xc_s, h_s,
              *, n_blocks, tb):
    W = LRU_W
    S = n_blocks * tb
    z = -lam_ref[...]
    softplus = jnp.maximum(z, 0.0) + jnp.log(1.0 + jnp.exp(-jnp.abs(z)))
    cl = -LRU_C * softplus
    row = lax.broadcasted_iota(jnp.int32, (tb, 1), 0)
    left = LRU_CONV // 2

    def conv_block(kb):
        r0 = pl.multiple_of(kb * tb, tb)
        halo = F32_SUBLANES
        prev = x_ref[pl.ds(pl.multiple_of(jnp.maximum(r0 - halo, 0), halo), halo), :]
        nxt = x_ref[pl.ds(pl.multiple_of(jnp.minimum(r0 + tb, S - halo), halo), halo), :]
        ext = jnp.concatenate([prev, x_ref[pl.ds(r0, tb), :], nxt], axis=0)
        t = row + r0
        acc = cb_ref[...] + jnp.zeros((tb, W), F32)
        for j in range(LRU_CONV):
            off = j - left
            if off == 0:
                xs = ext[halo:halo + tb]
            else:
                xs = pltpu.roll(ext, (-off) % (tb + 2 * halo), 0)[halo:halo + tb]
                xs = jnp.where((t + off >= 0) & (t + off < S), xs, 0.0)
            acc = acc + xs * cw_ref[j:j + 1, :]
        return acc

    def gates(xc, d):
        g = jnp.dot(xc.astype(BF16), gw_ref[:, d * 2 * W:(d + 1) * 2 * W],
                    preferred_element_type=F32) + gb_ref[:, d * 2 * W:(d + 1) * 2 * W]
        r = _sigmoid(g[:, :W])
        i = _sigmoid(g[:, W:])
        log_a = cl[d:d + 1, :] * r
        a = jnp.exp(log_a)
        b = jnp.sqrt(-jnp.tanh(log_a) * (a * a + 1.0)) * (i * xc)
        return a, b

    def scan_block(a, b, reverse):
        d = 1
        while d < tb:
            if reverse:
                keep = row < tb - d
                a_sh = jnp.where(keep, pltpu.roll(a, tb - d, 0), 1.0)
                b_sh = jnp.where(keep, pltpu.roll(b, tb - d, 0), 0.0)
            else:
                keep = row >= d
                a_sh = jnp.where(keep, pltpu.roll(a, d, 0), 1.0)
                b_sh = jnp.where(keep, pltpu.roll(b, d, 0), 0.0)
            b = a * b_sh + b
            a = a * a_sh
            d *= 2
        return a, b

    def fwd(kb, carry):
        r0 = pl.multiple_of(kb * tb, tb)
        xc = conv_block(kb)
        xc_s[pl.ds(r0, tb), :] = xc
        a, b = gates(xc, 0)
        a, b = scan_block(a, b, False)
        h = a * carry + b
        h_s[pl.ds(r0, tb), :] = h
        return h[tb - 1:tb, :]

    lax.fori_loop(0, n_blocks, fwd, jnp.zeros((1, W), F32))

    def bwd(t, carry):
        kb = n_blocks - 1 - t
        r0 = pl.multiple_of(kb * tb, tb)
        xc = xc_s[pl.ds(r0, tb), :]
        a, b = gates(xc, 1)
        a, b = scan_block(a, b, True)
        h = a * carry + b
        o_ref[pl.ds(r0, tb), :] = (h_s[pl.ds(r0, tb), :] + h) * _gelu_tanh(gt_ref[pl.ds(r0, tb), :])
        return h[0:1, :]

    lax.fori_loop(0, n_blocks, bwd, jnp.zeros((1, W), F32))


def _lru(lx, lgt, conv_w, conv_b, gate_w, gate_b, lam, B, S, tb):
    W = LRU_W
    nb = W // HEAD_DIM
    eye = jnp.eye(nb, dtype=F32)
    gw = jnp.einsum('dgnkm,nj->nkdgjm', gate_w, eye).reshape(W, 4 * W).astype(BF16)
    gb = gate_b.reshape(1, 4 * W)
    seq = pl.BlockSpec((S, W), lambda b: (b, 0))
    const = lambda shape: pl.BlockSpec(shape, lambda b: (0,) * len(shape))
    return pl.pallas_call(
        functools.partial(_lru_body, n_blocks=S // tb, tb=tb),
        grid=(B,),
        in_specs=[seq, seq, const((LRU_CONV, W)), const((1, W)), const((W, 4 * W)),
                  const((1, 4 * W)), const((2, W))],
        out_specs=seq,
        out_shape=jax.ShapeDtypeStruct((B * S, W), F32),
        scratch_shapes=[pltpu.VMEM((S, W), F32), pltpu.VMEM((S, W), F32)],
        compiler_params=_params("parallel"),
        name="rglru",
    )(lx, lgt, conv_w, conv_b.reshape(1, W), gw, gb, lam)


def _attn_prep_body(q_ref, k_ref, v_ref, cos_ref, sin_ref, qnw_ref, knw_ref, vrep_ref, rept_ref,
                    qo_ref, kt_ref, vo_ref):
    def head_norm_rot(x, w, cs, sn):
        W = x.shape[-1]
        lane = lax.broadcasted_iota(jnp.int32, (1, W), 1)
        same_head = (lax.broadcasted_iota(jnp.int32, (W, W), 0) // HEAD_DIM
                     == lax.broadcasted_iota(jnp.int32, (W, W), 1) // HEAD_DIM)
        ms = _group_sum(x * x, same_head.astype(BF16)) * (1.0 / HEAD_DIM)
        xn = x * lax.rsqrt(ms + EPS) * w
        quarter = HEAD_DIM // 4
        rot = jnp.where((lane % (2 * quarter)) < quarter, -pltpu.roll(xn, W - quarter, 1),
                        pltpu.roll(xn, quarter, 1))
        return xn * cs + rot * sn

    cs = cos_ref[...]
    sn = sin_ref[...]
    q = head_norm_rot(q_ref[...], qnw_ref[...], cs, sn)
    qo_ref[...] = (q * (HEAD_DIM ** -0.5 * LOG2_E)).astype(BF16)
    k = head_norm_rot(k_ref[...], knw_ref[...], cs[:, :KV_W], sn[:, :KV_W]).astype(BF16)
    kt_ref[...] = _nt_dot(rept_ref[...], k).astype(BF16)
    VW = 2 * HEAD_DIM
    vlane = lax.broadcasted_iota(jnp.int32, (1, ATT_KV_HEADS * VW), 1)
    ones = jnp.where(vlane % VW >= HEAD_DIM, 1.0, 0.0)
    vo_ref[...] = (jnp.dot(v_ref[...].astype(BF16), vrep_ref[...], preferred_element_type=F32)
                   + ones).astype(BF16)


def _attn_prep(aq, ak, av, cos, sin, qn_w, kn_w, B, S, tm):
    T = B * S
    ns = S // tm
    GW = ATT_GROUP * HEAD_DIM
    src = jnp.arange(ATT_KV_HEADS * GW)
    src = (src // GW) * HEAD_DIM + src % HEAD_DIM
    rep = (jnp.arange(KV_W)[:, None] == src[None, :]).astype(BF16)
    VW = 2 * HEAD_DIM
    vsrc = jnp.arange(ATT_KV_HEADS * VW)
    vsrc = jnp.where(vsrc % VW < HEAD_DIM, (vsrc // VW) * HEAD_DIM + vsrc % VW, -1)
    vrep = (jnp.arange(KV_W)[:, None] == vsrc[None, :]).astype(BF16)
    return pl.pallas_call(
        _attn_prep_body,
        grid=(B, ns),
        in_specs=[pl.BlockSpec((tm, ATT_W), lambda b, s: (b * ns + s, 0)),
                  pl.BlockSpec((tm, KV_W), lambda b, s: (b * ns + s, 0)),
                  pl.BlockSpec((tm, KV_W), lambda b, s: (b * ns + s, 0)),
                  pl.BlockSpec((tm, ATT_W), lambda b, s: (s, 0)),
                  pl.BlockSpec((tm, ATT_W), lambda b, s: (s, 0)),
                  pl.BlockSpec((1, ATT_W), lambda b, s: (0, 0)),
                  pl.BlockSpec((1, KV_W), lambda b, s: (0, 0)),
                  pl.BlockSpec((KV_W, ATT_KV_HEADS * VW), lambda b, s: (0, 0)),
                  pl.BlockSpec((ATT_KV_HEADS * GW, KV_W), lambda b, s: (0, 0))],
        out_specs=[pl.BlockSpec((tm, ATT_W), lambda b, s: (b * ns + s, 0)),
                   pl.BlockSpec((None, ATT_KV_HEADS * GW, tm), lambda b, s: (b, 0, s)),
                   pl.BlockSpec((tm, ATT_KV_HEADS * VW), lambda b, s: (b * ns + s, 0))],
        out_shape=[jax.ShapeDtypeStruct((T, ATT_W), BF16),
                   jax.ShapeDtypeStruct((B, ATT_KV_HEADS * GW, S), BF16),
                   jax.ShapeDtypeStruct((T, ATT_KV_HEADS * VW), BF16)],
        compiler_params=_params("parallel", "parallel"),
        name="attn_prep",
    )(aq, ak, av, cos, sin, jnp.tile(qn_w, ATT_HEADS).reshape(1, ATT_W),
      jnp.tile(kn_w, ATT_KV_HEADS).reshape(1, KV_W), vrep, rep.T)


def _attn_body(q_ref, kt_ref, v_ref, o_ref, *, tq, tk, nk):
    GW = ATT_GROUP * HEAD_DIM
    lane = lax.broadcasted_iota(jnp.int32, (1, GW), 1)
    masks = [(lane // HEAD_DIM) == g for g in range(ATT_GROUP)]
    q = q_ref[...]
    zero = jnp.zeros_like(q)
    qs = jnp.concatenate([jnp.where(m, q, zero) for m in masks], axis=0)
    rows = ATT_GROUP * tq
    VW = 2 * HEAD_DIM
    m_run = jnp.full((rows, 1), NEG_INF, F32)
    acc = jnp.zeros((rows, VW), F32)
    for c in range(nk):
        s = jnp.dot(qs, kt_ref[:, c * tk:(c + 1) * tk], preferred_element_type=F32)
        m_new = jnp.maximum(m_run, jnp.max(s, axis=-1, keepdims=True))
        alpha = jnp.exp2(m_run - m_new)
        p = jnp.exp2(s - m_new)
        acc = alpha * acc + jnp.dot(p.astype(BF16), v_ref[c * tk:(c + 1) * tk, :],
                                    preferred_element_type=F32)
        m_run = m_new
    on = acc * pltpu.roll(1.0 / acc, HEAD_DIM, 1)
    low = lax.broadcasted_iota(jnp.int32, (1, VW), 1) < HEAD_DIM
    heads = [on[g * tq:(g + 1) * tq] for g in range(ATT_GROUP)]
    for pair in range(ATT_GROUP // 2):
        o_ref[:, pair * VW:(pair + 1) * VW] = jnp.where(
            low, heads[2 * pair], pltpu.roll(heads[2 * pair + 1], HEAD_DIM, 1))


def _attention(qp, kt, vp, B, S, tq, tk):
    T = B * S
    nq = S // tq
    GW = ATT_GROUP * HEAD_DIM
    return pl.pallas_call(
        functools.partial(_attn_body, tq=tq, tk=tk, nk=S // tk),
        grid=(B, ATT_KV_HEADS, nq),
        in_specs=[pl.BlockSpec((tq, GW), lambda b, h, i: (b * nq + i, h)),
                  pl.BlockSpec((None, GW, S), lambda b, h, i: (b, h, 0)),
                  pl.BlockSpec((S, 2 * HEAD_DIM), lambda b, h, i: (b, h))],
        out_specs=pl.BlockSpec((tq, GW), lambda b, h, i: (b * nq + i, h)),
        out_shape=jax.ShapeDtypeStruct((T, ATT_W), F32),
        compiler_params=_params("parallel", "parallel", "parallel"),
        name="attention",
    )(qp, kt, vp)


def _outproj_body(x_ref, oret_ref, olru_ref, oatt_ref, w_ref, ln2_ref, wq_ref,
                  xo_ref, xn_ref, q_ref):
    y = jnp.dot(oret_ref[...].astype(BF16), w_ref[0:RET_W, :], preferred_element_type=F32)
    y = y + jnp.dot(olru_ref[...].astype(BF16), w_ref[RET_W:RET_W + LRU_W, :],
                    preferred_element_type=F32)
    y = y + jnp.dot(oatt_ref[...].astype(BF16), w_ref[RET_W + LRU_W:, :],
                    preferred_element_type=F32)
    x = x_ref[...] + y
    xo_ref[...] = x
    xn = _rms(x, ln2_ref[...]).astype(BF16)
    xn_ref[...] = xn.astype(F32).T.astype(BF16)
    q_ref[...] = jnp.dot(xn, wq_ref[...], preferred_element_type=F32)


def _outproj(x2, o_ret, o_lru, o_att, w_out_bf, ln2, wq_bf, tm):
    T, D = x2.shape
    QW = wq_bf.shape[1]
    MW = w_out_bf.shape[0]
    tok = lambda w: pl.BlockSpec((tm, w), lambda i: (i, 0))
    return pl.pallas_call(
        _outproj_body,
        grid=(T // tm,),
        in_specs=[tok(D), tok(RET_W), tok(LRU_W), tok(ATT_W),
                  pl.BlockSpec((MW, D), lambda i: (0, 0)),
                  pl.BlockSpec((1, D), lambda i: (0, 0)),
                  pl.BlockSpec((D, QW), lambda i: (0, 0))],
        out_specs=[tok(D), pl.BlockSpec((D, tm), lambda i: (0, i)), tok(QW)],
        out_shape=[jax.ShapeDtypeStruct((T, D), F32), jax.ShapeDtypeStruct((D, T), BF16),
                   jax.ShapeDtypeStruct((T, QW), F32)],
        compiler_params=_params("parallel"),
        name="outproj",
    )(x2, o_ret, o_lru, o_att, w_out_bf, ln2, wq_bf)


def _sorting_network(n):
    pairs = []
    p = 1
    while p < n:
        k = p
        while k >= 1:
            for j in range(k % p, n - k, 2 * k):
                for i in range(min(k, n - j - k)):
                    if (i + j) // (2 * p) == (i + j + k) // (2 * p):
                        pairs.append((i + j, i + j + k))
            k //= 2
        p *= 2
    return pairs


def _pair_counts(n):
    return [n // (i + 1) for i in range(n)]


def _peer_route_body(q_ref, keys_ref, cnt_ref, w0_ref, rk_ref, w1_ref, a_s, b_s, f_s,
                     *, half, topk):
    s0 = _nt_dot(keys_ref[0], q_ref[:, :half])
    s1 = _nt_dot(keys_ref[1], q_ref[:, half:])

    def top_sorted(tiles, n, out_ref):
        rows = list(tiles) + [None] * (pl.next_power_of_2(len(tiles)) - len(tiles))
        for lo_i, hi_i in _sorting_network(len(rows)):
            x, y = rows[lo_i], rows[hi_i]
            if y is None:
                continue
            if x is None:
                rows[lo_i], rows[hi_i] = y, None
            else:
                rows[lo_i], rows[hi_i] = jnp.maximum(x, y), jnp.minimum(x, y)
        rows = [t for t in rows if t is not None]
        for r in range(n):
            m = jnp.max(rows[0], axis=0, keepdims=True)
            out_ref[r:r + 1, :] = m
            hit = rows[0] == m
            for v in range(min(len(rows), n - r - 1)):
                nxt = rows[v + 1] if v + 1 < len(rows) else NEG_INF
                rows[v] = jnp.where(hit, nxt, rows[v])

    def sublane_tiles(x):
        sl = F32_SUBLANES
        return [x[sl * v:sl * (v + 1), :] for v in range(x.shape[0] // sl)]

    n = topk + 1
    top_sorted(sublane_tiles(s0), n, a_s)
    top_sorted(sublane_tiles(s1), n, b_s)
    a = a_s[0:topk, :]
    b = b_s[0:topk, :]
    counts = _pair_counts(n)
    sl = F32_SUBLANES
    row8 = lax.broadcasted_iota(jnp.int32, (sl, 1), 0)
    parts = [a[0:1, :] + b[0:sl, :], a[0:1, :] + b[sl:2 * sl, :]]
    i = 1
    while counts[i] > 1:
        parts.append(jnp.where(row8 < counts[i], a[i:i + 1, :] + b[0:sl, :], NEG_INF))
        i += 1
    parts.append(a[i:, :] + b[0:1, :])
    last = jnp.where(row8 == 0, a[0:1, :] + b_s[topk:n, :],
                     jnp.where(row8 == 1, a_s[topk:n, :] + b[0:1, :], NEG_INF))
    parts.append(last)
    top_sorted(parts, n, f_s)
    f = f_s[0:topk, :]
    z = jnp.sum(jnp.exp(f - f[0:1, :]), axis=0, keepdims=True)
    tau = 0.5 * (f[topk - 1:topk, :] + f_s[topk:n, :])
    thr = tau - s0
    cnt = jnp.zeros_like(thr)
    for r in range(topk):
        cnt = jnp.where(b[r:r + 1, :] >= thr, float(r + 1), cnt)
    cnt_ref[...] = cnt
    rank1 = jnp.zeros_like(s1)
    for r in range(topk):
        rank1 = jnp.where(b[r:r + 1, :] > s1, float(r + 1), rank1)
    w0_ref[...] = jnp.exp(s0 - a[0:1, :]) * (0.5 / z)
    rk_ref[...] = rank1.astype(BF16)
    w1_ref[...] = jnp.exp(s1 - b[0:1, :]).astype(BF16)


def _peer_route(q, keys, tm):
    T = q.shape[0]
    H, _, NK, half = keys.shape
    sl = F32_SUBLANES
    counts = _pair_counts(PEER_TOPK + 1)
    assert PEER_TOPK == 2 * sl and counts[1] <= sl and counts[sl - 1] > 1 >= counts[sl]
    assert NK > PEER_TOPK and NK % sl == 0
    per_key = jax.ShapeDtypeStruct((H, NK, T), F32)
    per_col = jax.ShapeDtypeStruct((H, NK, T), BF16)
    ospec = pl.BlockSpec((None, NK, tm), lambda i, h: (h, 0, i))
    return pl.pallas_call(
        functools.partial(_peer_route_body, half=half, topk=PEER_TOPK),
        grid=(T // tm, H),
        in_specs=[pl.BlockSpec((tm, 2 * half), lambda i, h: (i, h)),
                  pl.BlockSpec((None, 2, NK, half), lambda i, h: (h, 0, 0, 0))],
        out_specs=[ospec, ospec, ospec, ospec],
        out_shape=[per_key, per_key, per_col, per_col],
        scratch_shapes=[pltpu.VMEM((PEER_TOPK + sl, tm), F32)] * 3,
        compiler_params=_params("parallel", "parallel"),
        name="peer_route",
    )(q, keys)


def _peer_mix_body(xnt_ref, xres_ref, u_ref, vt_ref, cnt_ref, w0_ref, rk_ref, w1_ref, o_ref,
                   h_s, gh_s, acc_s, *, tl):
    H, ti, tm = cnt_ref.shape
    NK = rk_ref.shape[1]
    j = pl.program_id(1)

    last = pl.num_programs(1) - 1
    cur = j % 2

    def fold():
        acc_s[...] += jnp.dot(vt_ref[...], gh_s[1 - cur], preferred_element_type=F32)

    def row_tile(row):
        tile = jnp.broadcast_to(row, (BF16_SUBLANES, row.shape[1])).astype(BF16)
        return pltpu.repeat(tile, NK // BF16_SUBLANES, axis=0)

    def build():
        for ii in range(ti):
            rs = slice(ii * NK, (ii + 1) * NK)
            for lc in range(tm // tl):
                ls = slice(lc * tl, (lc + 1) * tl)
                g = None
                for h in range(H):
                    c = row_tile(cnt_ref[h, ii:ii + 1, ls])
                    a = row_tile(w0_ref[h, ii:ii + 1, ls])
                    term = a * jnp.where(rk_ref[h, :, ls] < c, w1_ref[h, :, ls], 0)
                    g = term if g is None else g + term
                gh_s[cur, rs, ls] = g
        h_s[...] = jnp.dot(u_ref[...], xnt_ref[...], preferred_element_type=F32)
        for ii in range(ti):
            rs = slice(ii * NK, (ii + 1) * NK)
            gh_s[cur, rs, :] = gh_s[cur, rs, :] * _gelu_tanh_x2(h_s[rs, :].astype(BF16))

    @pl.when(j == 0)
    def _():
        acc_s[...] = jnp.zeros_like(acc_s)
        build()

    @pl.when(jnp.logical_and(j > 0, j < last))
    def _():
        fold()
        build()

    @pl.when(j == last)
    def _():
        fold()
        o_ref[...] = xres_ref[...] + acc_s[...].T


def _peer_mix(xnt, xres, u_bf, vt_bf, cnt, w0, rk, w1, tm, ti, tl):
    D, T = xnt.shape
    H, NK, _ = cnt.shape
    E = u_bf.shape[0]
    te = ti * NK
    n_e = E // te
    build = lambda j: jnp.minimum(j, n_e - 1)
    fold = lambda j: jnp.maximum(j - 1, 0)
    keys = pl.BlockSpec((H, ti, tm), lambda i, j: (0, build(j), i))
    full = pl.BlockSpec((H, NK, tm), lambda i, j: (0, 0, i))
    return pl.pallas_call(
        functools.partial(_peer_mix_body, tl=tl),
        grid=(T // tm, n_e + 1),
        in_specs=[pl.BlockSpec((D, tm), lambda i, j: (0, i)),
                  pl.BlockSpec((tm, D), lambda i, j: (i, 0)),
                  pl.BlockSpec((te, D), lambda i, j: (build(j), 0)),
                  pl.BlockSpec((D, te), lambda i, j: (0, fold(j))),
                  keys, keys, full, full],
        out_specs=pl.BlockSpec((tm, D), lambda i, j: (i, 0)),
        out_shape=jax.ShapeDtypeStruct((T, D), F32),
        scratch_shapes=[pltpu.VMEM((te, tm), F32), pltpu.VMEM((2, te, tm), BF16),
                        pltpu.VMEM((D, tm), F32)],
        compiler_params=_params("parallel", "arbitrary"),
        name="peer_mix",
    )(xnt, xres, u_bf, vt_bf, cnt, w0, rk, w1)


def _final_norm_body(x_ref, w_ref, o_ref):
    o_ref[...] = _rms(x_ref[...], w_ref[...])


def _final_norm(x2, w, tm):
    T, D = x2.shape
    return pl.pallas_call(
        _final_norm_body,
        grid=(T // tm,),
        in_specs=[pl.BlockSpec((tm, D), lambda i: (i, 0)), pl.BlockSpec((1, D), lambda i: (0, 0))],
        out_specs=pl.BlockSpec((tm, D), lambda i: (i, 0)),
        out_shape=jax.ShapeDtypeStruct((T, D), F32),
        compiler_params=_params("parallel"),
        name="final_norm",
    )(x2, w)


def _rope_table(pos, inv_freq):
    ang = pos[:, None] * inv_freq[None, :]
    ang = jnp.concatenate([ang, ang], axis=-1)
    return jnp.cos(ang), jnp.sin(ang)


def _position_tables(S):
    t = jnp.arange(S, dtype=F32)
    n_rows = S // GRID_W
    rows = jnp.repeat(jnp.arange(n_rows, dtype=F32), GRID_W)
    cols = jnp.tile(jnp.arange(GRID_W, dtype=F32), n_rows)
    ret_inv = 1.0 / (10000.0 ** jnp.linspace(0.0, 1.0, HEAD_DIM // 2, dtype=F32))
    ret_cos, ret_sin = _rope_table(t, ret_inv)
    ax_n = HEAD_DIM // 4
    ax_inv = ROPE_BASE ** (-jnp.arange(ax_n, dtype=F32) / ax_n)
    cr, sr = _rope_table(rows, ax_inv)
    cc, sc = _rope_table(cols, ax_inv)
    ax_cos = jnp.concatenate([cr, cc], axis=-1)
    ax_sin = jnp.concatenate([sr, sc], axis=-1)
    return (jnp.tile(ret_cos, (1, RET_HEADS)), jnp.tile(ret_sin, (1, RET_HEADS)),
            jnp.tile(ax_cos, (1, ATT_HEADS)), jnp.tile(ax_sin, (1, ATT_HEADS)))


def _tiles(B, S, n_keys):
    T = B * S
    return dict(
        tm=min(512, T),
        tb=min(256, S),
        tp=min(512, S),
        tq=min(256, S),
        tk=min(1024, S),
        tr=min(512, T),
        tmix=min(512, T),
        ti=min(8, n_keys),
        tl=min(256, T),
    )


def kernel(x, ln1_w, w_in, ret_log_decay, ret_gn_w, lru_conv_w, lru_conv_b, lru_gate_w, lru_gate_b,
           lru_lambda, attn_q_norm, attn_k_norm, w_out, ln2_w, peer_wq, peer_keys, peer_u, peer_v,
           lnf_w):
    B, S, D = x.shape
    T = B * S
    depth = w_in.shape[0]
    n_keys = peer_keys.shape[3]
    tl = _tiles(B, S, n_keys)
    widths = (RET_W,) * 4 + (LRU_W,) * 2 + (ATT_W, KV_W, KV_W)
    ret_cos, ret_sin, ax_cos, ax_sin = _position_tables(S)
    x2 = x.reshape(T, D)
    for l in range(depth):
        rq, rk, rv, rg, lx, lgt, aq, ak, av = _inproj(
            x2, ln1_w[l].reshape(1, D), w_in[l].astype(BF16), widths, tl["tm"])
        o_ret = _retention(rq, rk, rv, rg, ret_cos, ret_sin, ret_log_decay[l],
                           ret_gn_w[l].reshape(1, RET_W), B, S)
        o_lru = _lru(lx, lgt, lru_conv_w[l], lru_conv_b[l], lru_gate_w[l], lru_gate_b[l],
                     lru_lambda[l], B, S, tl["tb"])
        qp, kt, vp = _attn_prep(aq, ak, av, ax_cos, ax_sin, attn_q_norm[l], attn_k_norm[l],
                                B, S, tl["tp"])
        o_att = _attention(qp, kt, vp, B, S, tl["tq"], tl["tk"])
        x2, xn, q = _outproj(x2, o_ret, o_lru, o_att, w_out[l].astype(BF16),
                             ln2_w[l].reshape(1, D), peer_wq[l].astype(BF16), tl["tm"])
        cnt, w0, rk, w1 = _peer_route(q, peer_keys[l], tl["tr"])
        x2 = _peer_mix(xn, x2, peer_u[l].astype(BF16), peer_v[l].T.astype(BF16),
                       cnt, w0, rk, w1, tl["tmix"], tl["ti"], tl["tl"])
    return _final_norm(x2, lnf_w.reshape(1, D), tl["tm"]).reshape(B, S, D)
```

```python
import functools

import jax
import jax.numpy as jnp
from jax import lax
from jax.experimental import pallas as pl
from jax.experimental.pallas import tpu as pltpu

F32 = jnp.float32
BF16 = jnp.bfloat16
EPS = 1e-6
LOG2_E = 1.4426950408889634

GRID_W = 64
CHUNK = 128
HEAD_DIM = 64
RET_HEADS = 4
RET_W = RET_HEADS * HEAD_DIM
LRU_W = 256
LRU_CONV = 4
LRU_C = 8.0
ATT_HEADS = 8
ATT_KV_HEADS = 2
ATT_GROUP = ATT_HEADS // ATT_KV_HEADS
ATT_W = ATT_HEADS * HEAD_DIM
KV_W = ATT_KV_HEADS * HEAD_DIM
ROPE_BASE = 10000.0
PEER_TOPK = 16
F32_SUBLANES = 8
BF16_SUBLANES = 16

VMEM_LIMIT_BYTES = 56 * 1024 * 1024
NEG_INF = float("-inf")


def _params(*sem):
    return pltpu.CompilerParams(dimension_semantics=sem, vmem_limit_bytes=VMEM_LIMIT_BYTES)


def _rms(x, w):
    return x * lax.rsqrt(jnp.mean(x * x, axis=-1, keepdims=True) + EPS) * w


def _gelu_tanh(x):
    return 0.5 * x * (1.0 + jnp.tanh(0.7978845608028654 * (x + 0.044715 * (x * x * x))))


def _gelu_tanh_x2(x):
    return x + x * jnp.tanh(x * (0.7978845608028654 + (0.7978845608028654 * 0.044715) * (x * x)))


def _sigmoid(x):
    return 1.0 / (1.0 + jnp.exp(-x))


def _nt_dot(a, b):
    return lax.dot_general(a, b, (((1,), (1,)), ((), ())), preferred_element_type=F32)


def _group_sum(x, member):
    hi = x.astype(BF16)
    lo = (x - hi.astype(F32)).astype(BF16)
    return (jnp.dot(hi, member, preferred_element_type=F32)
            + jnp.dot(lo, member, preferred_element_type=F32))


def _tn_dot(a, b):
    return lax.dot_general(a, b, (((0,), (0,)), ((), ())), preferred_element_type=F32)


def _inproj_body(x_ref, lnw_ref, w_ref, *out_refs, widths):
    hb = _rms(x_ref[...], lnw_ref[...]).astype(BF16)
    off = 0
    for o_ref, wd in zip(out_refs, widths):
        o_ref[...] = jnp.dot(hb, w_ref[:, off:off + wd],
                             preferred_element_type=F32).astype(o_ref.dtype)
        off += wd


def _inproj(x2, lnw, w_bf, widths, tm):
    T, D = x2.shape
    n_in = w_bf.shape[1]
    return pl.pallas_call(
        functools.partial(_inproj_body, widths=widths),
        grid=(T // tm,),
        in_specs=[pl.BlockSpec((tm, D), lambda i: (i, 0)),
                  pl.BlockSpec((1, D), lambda i: (0, 0)),
                  pl.BlockSpec((D, n_in), lambda i: (0, 0))],
        out_specs=[pl.BlockSpec((tm, wd), lambda i: (i, 0)) for wd in widths],
        out_shape=[jax.ShapeDtypeStruct((T, wd), F32) for wd in widths],
        compiler_params=_params("parallel"),
        name="inproj",
    )(x2, lnw, w_bf)


def _ret_body(ld_ref, q_ref, k_ref, v_ref, g_ref, cos_ref, sin_ref, lgl_ref, gnw_ref, o_ref,
              qr_s, kr_s, st_s, d_s, *, n_chunks):
    C, W = CHUNK, RET_W
    lane = lax.broadcasted_iota(jnp.int32, (1, W), 1)
    first_half = (lane % HEAD_DIM) < (HEAD_DIM // 2)
    head_masks = [(lane // HEAD_DIM) == h for h in range(RET_HEADS)]
    same_head = (lax.broadcasted_iota(jnp.int32, (W, W), 0) // HEAD_DIM
                 == lax.broadcasted_iota(jnp.int32, (W, W), 1) // HEAD_DIM)
    bd = same_head.astype(F32)
    lgf = lgl_ref[0:1, :]
    lgb = lgl_ref[1:2, :]
    idx = lax.broadcasted_iota(jnp.int32, (C, 1), 0).astype(F32)
    decf = jnp.exp(lgf * float(C))
    decb = jnp.exp(lgb * float(C))

    diff = (lax.broadcasted_iota(jnp.int32, (C, C), 0)
            - lax.broadcasted_iota(jnp.int32, (C, C), 1)).astype(F32)
    for h in range(RET_HEADS):
        d_s[h] = jnp.where(diff >= 0.0,
                           jnp.exp(ld_ref[0, h] * jnp.maximum(diff, 0.0)),
                           jnp.exp(ld_ref[1, h] * jnp.maximum(-diff, 0.0)))

    def rot_half(x):
        return jnp.where(first_half, -pltpu.roll(x, W - HEAD_DIM // 2, 1),
                         pltpu.roll(x, HEAD_DIM // 2, 1))

    def prep(c, carry):
        r0 = pl.multiple_of(c * C, C)
        cs = cos_ref[pl.ds(r0, C), :]
        sn = sin_ref[pl.ds(r0, C), :]
        q = q_ref[pl.ds(r0, C), :]
        k = k_ref[pl.ds(r0, C), :]
        qr_s[pl.ds(r0, C), :] = q * cs + rot_half(q) * sn
        kr_s[pl.ds(r0, C), :] = (k * cs + rot_half(k) * sn) * (HEAD_DIM ** -0.5)
        return carry

    lax.fori_loop(0, n_chunks, prep, 0, unroll=4)

    st_s[...] = jnp.zeros((W, W), F32)

    def fwd(c, carry):
        r0 = pl.multiple_of(c * C, C)
        q = qr_s[pl.ds(r0, C), :]
        k = kr_s[pl.ds(r0, C), :]
        v = v_ref[pl.ds(r0, C), :]
        st = st_s[...]
        qw = q * jnp.exp(lgf * (idx + 1.0))
        o_ref[pl.ds(r0, C), :] = jnp.dot(qw.astype(BF16), (st * bd).astype(BF16),
                                         preferred_element_type=F32)
        kw = k * jnp.exp(lgf * (float(C) - 1.0 - idx))
        st_s[...] = st * decf + _tn_dot(kw.astype(BF16), v.astype(BF16))
        return carry

    lax.fori_loop(0, n_chunks, fwd, 0, unroll=4)

    st_s[...] = jnp.zeros((W, W), F32)

    def bwd(t, carry):
        c = n_chunks - 1 - t
        r0 = pl.multiple_of(c * C, C)
        q = qr_s[pl.ds(r0, C), :]
        k = kr_s[pl.ds(r0, C), :]
        v = v_ref[pl.ds(r0, C), :]
        st = st_s[...]
        qw = q * jnp.exp(lgb * (float(C) - idx))
        vb = v.astype(BF16)
        kb = k.astype(BF16)
        o = o_ref[pl.ds(r0, C), :] + jnp.dot(qw.astype(BF16), (st * bd).astype(BF16),
                                              preferred_element_type=F32)
        kw = k * jnp.exp(lgb * idx)
        st_s[...] = st * decb + _tn_dot(kw.astype(BF16), vb)
        for h in range(RET_HEADS):
            qm = jnp.where(head_masks[h], q, 0.0).astype(BF16)
            s = _nt_dot(qm, kb) * d_s[h]
            oh = jnp.dot(s.astype(BF16), vb, preferred_element_type=F32)
            o = o + jnp.where(head_masks[h], oh, 0.0)
        ms = _group_sum(o * o, bd.astype(BF16)) * (1.0 / HEAD_DIM)
        on = o * lax.rsqrt(ms + EPS) * gnw_ref[...]
        g = g_ref[pl.ds(r0, C), :]
        o_ref[pl.ds(r0, C), :] = (g * _sigmoid(g)) * on
        return carry

    lax.fori_loop(0, n_chunks, bwd, 0, unroll=4)


def _retention(rq, rk, rv, rg, cos, sin, log_decay, gn_w, B, S):
    W = RET_W
    lgl = jnp.repeat(log_decay, HEAD_DIM, axis=1)
    seq = pl.BlockSpec((S, W), lambda b: (b, 0), pipeline_mode=pl.Buffered(1))
    tab = pl.BlockSpec((S, W), lambda b: (0, 0), pipeline_mode=pl.Buffered(1))
    return pl.pallas_call(
        functools.partial(_ret_body, n_chunks=S // CHUNK),
        grid=(B,),
        in_specs=[pl.BlockSpec(memory_space=pltpu.SMEM), seq, seq, seq, seq, tab, tab,
                  pl.BlockSpec((2, W), lambda b: (0, 0)),
                  pl.BlockSpec((1, W), lambda b: (0, 0))],
        out_specs=pl.BlockSpec((S, W), lambda b: (b, 0)),
        out_shape=jax.ShapeDtypeStruct((B * S, W), F32),
        scratch_shapes=[pltpu.VMEM((S, W), F32), pltpu.VMEM((S, W), F32),
                        pltpu.VMEM((W, W), F32), pltpu.VMEM((RET_HEADS, CHUNK, CHUNK), F32)],
        compiler_params=_params("parallel"),
        name="retention",
    )(log_decay, rq, rk, rv, rg, cos, sin, lgl, gn_w)


def _lru_body(x_ref, gt_ref, cw_ref, cb_ref, gw_ref, gb_ref, lam_ref, o_ref, xc_s, h_s,
              *, n_blocks, tb):
    W = LRU_W
    S = n_blocks * tb
    z = -lam_ref[...]
    softplus = jnp.maximum(z, 0.0) + jnp.log(1.0 + jnp.exp(-jnp.abs(z)))
    cl = -LRU_C * softplus
    row = lax.broadcasted_iota(jnp.int32, (tb, 1), 0)
    left = LRU_CONV // 2

    def conv_block(kb):
        r0 = pl.multiple_of(kb * tb, tb)
        halo = F32_SUBLANES
        prev = x_ref[pl.ds(pl.multiple_of(jnp.maximum(r0 - halo, 0), halo), halo), :]
        nxt = x_ref[pl.ds(pl.multiple_of(jnp.minimum(r0 + tb, S - halo), halo), halo), :]
        ext = jnp.concatenate([prev, x_ref[pl.ds(r0, tb), :], nxt], axis=0)
        t = row + r0
        acc = cb_ref[...] + jnp.zeros((tb, W), F32)
        for j in range(LRU_CONV):
            off = j - left
            if off == 0:
                xs = ext[halo:halo + tb]
            else:
                xs = pltpu.roll(ext, (-off) % (tb + 2 * halo), 0)[halo:halo + tb]
                xs = jnp.where((t + off >= 0) & (t + off < S), xs, 0.0)
            acc = acc + xs * cw_ref[j:j + 1, :]
        return acc

    def gates(xc, d):
        g = jnp.dot(xc.astype(BF16), gw_ref[:, d * 2 * W:(d + 1) * 2 * W],
                    preferred_element_type=F32) + gb_ref[:, d * 2 * W:(d + 1) * 2 * W]
        r = _sigmoid(g[:, :W])
        i = _sigmoid(g[:, W:])
        log_a = cl[d:d + 1, :] * r
        a = jnp.exp(log_a)
        b = jnp.sqrt(-jnp.tanh(log_a) * (a * a + 1.0)) * (i * xc)
        return a, b

    def scan_block(a, b, reverse):
        d = 1
        while d < tb:
            if reverse:
                keep = row < tb - d
                a_sh = jnp.where(keep, pltpu.roll(a, tb - d, 0), 1.0)
                b_sh = jnp.where(keep, pltpu.roll(b, tb - d, 0), 0.0)
            else:
                keep = row >= d
                a_sh = jnp.where(keep, pltpu.roll(a, d, 0), 1.0)
                b_sh = jnp.where(keep, pltpu.roll(b, d, 0), 0.0)
            b = a * b_sh + b
            a = a * a_sh
            d *= 2
        return a, b

    def fwd(kb, carry):
        r0 = pl.multiple_of(kb * tb, tb)
        xc = conv_block(kb)
        xc_s[pl.ds(r0, tb), :] = xc
        a, b = gates(xc, 0)
        a, b = scan_block(a, b, False)
        h = a * carry + b
        h_s[pl.ds(r0, tb), :] = h
        return h[tb - 1:tb, :]

    lax.fori_loop(0, n_blocks, fwd, jnp.zeros((1, W), F32))

    def bwd(t, carry):
        kb = n_blocks - 1 - t
        r0 = pl.multiple_of(kb * tb, tb)
        xc = xc_s[pl.ds(r0, tb), :]
        a, b = gates(xc, 1)
        a, b = scan_block(a, b, True)
        h = a * carry + b
        o_ref[pl.ds(r0, tb), :] = (h_s[pl.ds(r0, tb), :] + h) * _gelu_tanh(gt_ref[pl.ds(r0, tb), :])
        return h[0:1, :]

    lax.fori_loop(0, n_blocks, bwd, jnp.zeros((1, W), F32))


def _lru(lx, lgt, conv_w, conv_b, gate_w, gate_b, lam, B, S, tb):
    W = LRU_W
    nb = W // HEAD_DIM
    eye = jnp.eye(nb, dtype=F32)
    gw = jnp.einsum('dgnkm,nj->nkdgjm', gate_w, eye).reshape(W, 4 * W).astype(BF16)
    gb = gate_b.reshape(1, 4 * W)
    seq = pl.BlockSpec((S, W), lambda b: (b, 0))
    const = lambda shape: pl.BlockSpec(shape, lambda b: (0,) * len(shape))
    return pl.pallas_call(
        functools.partial(_lru_body, n_blocks=S // tb, tb=tb),
        grid=(B,),
        in_specs=[seq, seq, const((LRU_CONV, W)), const((1, W)), const((W, 4 * W)),
                  const((1, 4 * W)), const((2, W))],
        out_specs=seq,
        out_shape=jax.ShapeDtypeStruct((B * S, W), F32),
        scratch_shapes=[pltpu.VMEM((S, W), F32), pltpu.VMEM((S, W), F32)],
        compiler_params=_params("parallel"),
        name="rglru",
    )(lx, lgt, conv_w, conv_b.reshape(1, W), gw, gb, lam)


def _attn_prep_body(q_ref, k_ref, v_ref, cos_ref, sin_ref, qnw_ref, knw_ref, vrep_ref, rept_ref,
                    qo_ref, kt_ref, vo_ref):
    def head_norm_rot(x, w, cs, sn):
        W = x.shape[-1]
        lane = lax.broadcasted_iota(jnp.int32, (1, W), 1)
        same_head = (lax.broadcasted_iota(jnp.int32, (W, W), 0) // HEAD_DIM
                     == lax.broadcasted_iota(jnp.int32, (W, W), 1) // HEAD_DIM)
        ms = _group_sum(x * x, same_head.astype(BF16)) * (1.0 / HEAD_DIM)
        xn = x * lax.rsqrt(ms + EPS) * w
        quarter = HEAD_DIM // 4
        rot = jnp.where((lane % (2 * quarter)) < quarter, -pltpu.roll(xn, W - quarter, 1),
                        pltpu.roll(xn, quarter, 1))
        return xn * cs + rot * sn

    cs = cos_ref[...]
    sn = sin_ref[...]
    q = head_norm_rot(q_ref[...], qnw_ref[...], cs, sn)
    qo_ref[...] = (q * (HEAD_DIM ** -0.5 * LOG2_E)).astype(BF16)
    k = head_norm_rot(k_ref[...], knw_ref[...], cs[:, :KV_W], sn[:, :KV_W]).astype(BF16)
    kt_ref[...] = _nt_dot(rept_ref[...], k).astype(BF16)
    VW = 2 * HEAD_DIM
    vlane = lax.broadcasted_iota(jnp.int32, (1, ATT_KV_HEADS * VW), 1)
    ones = jnp.where(vlane % VW >= HEAD_DIM, 1.0, 0.0)
    vo_ref[...] = (jnp.dot(v_ref[...].astype(BF16), vrep_ref[...], preferred_element_type=F32)
                   + ones).astype(BF16)


def _attn_prep(aq, ak, av, cos, sin, qn_w, kn_w, B, S, tm):
    T = B * S
    ns = S // tm
    GW = ATT_GROUP * HEAD_DIM
    src = jnp.arange(ATT_KV_HEADS * GW)
    src = (src // GW) * HEAD_DIM + src % HEAD_DIM
    rep = (jnp.arange(KV_W)[:, None] == src[None, :]).astype(BF16)
    VW = 2 * HEAD_DIM
    vsrc = jnp.arange(ATT_KV_HEADS * VW)
    vsrc = jnp.where(vsrc % VW < HEAD_DIM, (vsrc // VW) * HEAD_DIM + vsrc % VW, -1)
    vrep = (jnp.arange(KV_W)[:, None] == vsrc[None, :]).astype(BF16)
    return pl.pallas_call(
        _attn_prep_body,
        grid=(B, ns),
        in_specs=[pl.BlockSpec((tm, ATT_W), lambda b, s: (b * ns + s, 0)),
                  pl.BlockSpec((tm, KV_W), lambda b, s: (b * ns + s, 0)),
                  pl.BlockSpec((tm, KV_W), lambda b, s: (b * ns + s, 0)),
                  pl.BlockSpec((tm, ATT_W), lambda b, s: (s, 0)),
                  pl.BlockSpec((tm, ATT_W), lambda b, s: (s, 0)),
                  pl.BlockSpec((1, ATT_W), lambda b, s: (0, 0)),
                  pl.BlockSpec((1, KV_W), lambda b, s: (0, 0)),
                  pl.BlockSpec((KV_W, ATT_KV_HEADS * VW), lambda b, s: (0, 0)),
                  pl.BlockSpec((ATT_KV_HEADS * GW, KV_W), lambda b, s: (0, 0))],
        out_specs=[pl.BlockSpec((tm, ATT_W), lambda b, s: (b * ns + s, 0)),
                   pl.BlockSpec((None, ATT_KV_HEADS * GW, tm), lambda b, s: (b, 0, s)),
                   pl.BlockSpec((tm, ATT_KV_HEADS * VW), lambda b, s: (b * ns + s, 0))],
        out_shape=[jax.ShapeDtypeStruct((T, ATT_W), BF16),
                   jax.ShapeDtypeStruct((B, ATT_KV_HEADS * GW, S), BF16),
                   jax.ShapeDtypeStruct((T, ATT_KV_HEADS * VW), BF16)],
        compiler_params=_params("parallel", "parallel"),
        name="attn_prep",
    )(aq, ak, av, cos, sin, jnp.tile(qn_w, ATT_HEADS).reshape(1, ATT_W),
      jnp.tile(kn_w, ATT_KV_HEADS).reshape(1, KV_W), vrep, rep.T)


def _attn_body(q_ref, kt_ref, v_ref, o_ref, *, tq, tk, nk):
    GW = ATT_GROUP * HEAD_DIM
    lane = lax.broadcasted_iota(jnp.int32, (1, GW), 1)
    masks = [(lane // HEAD_DIM) == g for g in range(ATT_GROUP)]
    q = q_ref[...]
    zero = jnp.zeros_like(q)
    qs = jnp.concatenate([jnp.where(m, q, zero) for m in masks], axis=0)
    rows = ATT_GROUP * tq
    VW = 2 * HEAD_DIM
    m_run = jnp.full((rows, 1), NEG_INF, F32)
    acc = jnp.zeros((rows, VW), F32)
    for c in range(nk):
        s = jnp.dot(qs, kt_ref[:, c * tk:(c + 1) * tk], preferred_element_type=F32)
        m_new = jnp.maximum(m_run, jnp.max(s, axis=-1, keepdims=True))
        alpha = jnp.exp2(m_run - m_new)
        p = jnp.exp2(s - m_new)
        acc = alpha * acc + jnp.dot(p.astype(BF16), v_ref[c * tk:(c + 1) * tk, :],
                                    preferred_element_type=F32)
        m_run = m_new
    on = acc * pltpu.roll(1.0 / acc, HEAD_DIM, 1)
    low = lax.broadcasted_iota(jnp.int32, (1, VW), 1) < HEAD_DIM
    heads = [on[g * tq:(g + 1) * tq] for g in range(ATT_GROUP)]
    for pair in range(ATT_GROUP // 2):
        o_ref[:, pair * VW:(pair + 1) * VW] = jnp.where(
            low, heads[2 * pair], pltpu.roll(heads[2 * pair + 1], HEAD_DIM, 1))


def _attention(qp, kt, vp, B, S, tq, tk):
    T = B * S
    nq = S // tq
    GW = ATT_GROUP * HEAD_DIM
    return pl.pallas_call(
        functools.partial(_attn_body, tq=tq, tk=tk, nk=S // tk),
        grid=(B, ATT_KV_HEADS, nq),
        in_specs=[pl.BlockSpec((tq, GW), lambda b, h, i: (b * nq + i, h)),
                  pl.BlockSpec((None, GW, S), lambda b, h, i: (b, h, 0)),
                  pl.BlockSpec((S, 2 * HEAD_DIM), lambda b, h, i: (b, h))],
        out_specs=pl.BlockSpec((tq, GW), lambda b, h, i: (b * nq + i, h)),
        out_shape=jax.ShapeDtypeStruct((T, ATT_W), F32),
        compiler_params=_params("parallel", "parallel", "parallel"),
        name="attention",
    )(qp, kt, vp)


def _outproj_body(x_ref, oret_ref, olru_ref, oatt_ref, w_ref, ln2_ref, wq_ref,
                  xo_ref, xn_ref, q_ref):
    y = jnp.dot(oret_ref[...].astype(BF16), w_ref[0:RET_W, :], preferred_element_type=F32)
    y = y + jnp.dot(olru_ref[...].astype(BF16), w_ref[RET_W:RET_W + LRU_W, :],
                    preferred_element_type=F32)
    y = y + jnp.dot(oatt_ref[...].astype(BF16), w_ref[RET_W + LRU_W:, :],
                    preferred_element_type=F32)
    x = x_ref[...] + y
    xo_ref[...] = x
    xn = _rms(x, ln2_ref[...]).astype(BF16)
    xn_ref[...] = xn.astype(F32).T.astype(BF16)
    q_ref[...] = jnp.dot(xn, wq_ref[...], preferred_element_type=F32)


def _outproj(x2, o_ret, o_lru, o_att, w_out_bf, ln2, wq_bf, tm):
    T, D = x2.shape
    QW = wq_bf.shape[1]
    MW = w_out_bf.shape[0]
    tok = lambda w: pl.BlockSpec((tm, w), lambda i: (i, 0))
    return pl.pallas_call(
        _outproj_body,
        grid=(T // tm,),
        in_specs=[tok(D), tok(RET_W), tok(LRU_W), tok(ATT_W),
                  pl.BlockSpec((MW, D), lambda i: (0, 0)),
                  pl.BlockSpec((1, D), lambda i: (0, 0)),
                  pl.BlockSpec((D, QW), lambda i: (0, 0))],
        out_specs=[tok(D), pl.BlockSpec((D, tm), lambda i: (0, i)), tok(QW)],
        out_shape=[jax.ShapeDtypeStruct((T, D), F32), jax.ShapeDtypeStruct((D, T), BF16),
                   jax.ShapeDtypeStruct((T, QW), F32)],
        compiler_params=_params("parallel"),
        name="outproj",
    )(x2, o_ret, o_lru, o_att, w_out_bf, ln2, wq_bf)


def _sorting_network(n):
    pairs = []
    p = 1
    while p < n:
        k = p
        while k >= 1:
            for j in range(k % p, n - k, 2 * k):
                for i in range(min(k, n - j - k)):
                    if (i + j) // (2 * p) == (i + j + k) // (2 * p):
                        pairs.append((i + j, i + j + k))
            k //= 2
        p *= 2
    return pairs


def _pair_counts(n):
    return [n // (i + 1) for i in range(n)]


def _peer_route_body(q_ref, keys_ref, cnt_ref, w0_ref, rk_ref, w1_ref, a_s, b_s, f_s,
                     *, half, topk):
    s0 = _nt_dot(keys_ref[0], q_ref[:, :half])
    s1 = _nt_dot(keys_ref[1], q_ref[:, half:])

    def top_sorted(tiles, n, out_ref):
        rows = list(tiles) + [None] * (pl.next_power_of_2(len(tiles)) - len(tiles))
        for lo_i, hi_i in _sorting_network(len(rows)):
            x, y = rows[lo_i], rows[hi_i]
            if y is None:
                continue
            if x is None:
                rows[lo_i], rows[hi_i] = y, None
            else:
                rows[lo_i], rows[hi_i] = jnp.maximum(x, y), jnp.minimum(x, y)
        rows = [t for t in rows if t is not None]
        for r in range(n):
            m = jnp.max(rows[0], axis=0, keepdims=True)
            out_ref[r:r + 1, :] = m
            hit = rows[0] == m
            for v in range(min(len(rows), n - r - 1)):
                nxt = rows[v + 1] if v + 1 < len(rows) else NEG_INF
                rows[v] = jnp.where(hit, nxt, rows[v])

    def sublane_tiles(x):
        sl = F32_SUBLANES
        return [x[sl * v:sl * (v + 1), :] for v in range(x.shape[0] // sl)]

    n = topk + 1
    top_sorted(sublane_tiles(s0), n, a_s)
    top_sorted(sublane_tiles(s1), n, b_s)
    a = a_s[0:topk, :]
    b = b_s[0:topk, :]
    counts = _pair_counts(n)
    sl = F32_SUBLANES
    row8 = lax.broadcasted_iota(jnp.int32, (sl, 1), 0)
    parts = [a[0:1, :] + b[0:sl, :], a[0:1, :] + b[sl:2 * sl, :]]
    i = 1
    while counts[i] > 1:
        parts.append(jnp.where(row8 < counts[i], a[i:i + 1, :] + b[0:sl, :], NEG_INF))
        i += 1
    parts.append(a[i:, :] + b[0:1, :])
    last = jnp.where(row8 == 0, a[0:1, :] + b_s[topk:n, :],
                     jnp.where(row8 == 1, a_s[topk:n, :] + b[0:1, :], NEG_INF))
    parts.append(last)
    top_sorted(parts, n, f_s)
    f = f_s[0:topk, :]
    z = jnp.sum(jnp.exp(f - f[0:1, :]), axis=0, keepdims=True)
    tau = 0.5 * (f[topk - 1:topk, :] + f_s[topk:n, :])
    thr = tau - s0
    cnt = jnp.zeros_like(thr)
    for r in range(topk):
        cnt = jnp.where(b[r:r + 1, :] >= thr, float(r + 1), cnt)
    cnt_ref[...] = cnt
    rank1 = jnp.zeros_like(s1)
    for r in range(topk):
        rank1 = jnp.where(b[r:r + 1, :] > s1, float(r + 1), rank1)
    w0_ref[...] = jnp.exp(s0 - a[0:1, :]) * (0.5 / z)
    rk_ref[...] = rank1.astype(BF16)
    w1_ref[...] = jnp.exp(s1 - b[0:1, :]).astype(BF16)


def _peer_route_heads(q_ref, keys_ref, cnt_ref, w0_ref, rk_ref, w1_ref, a_s, b_s, f_s,
                      *, half, topk, hp):
    for k in range(hp):
        _peer_route_body(q_ref.at[:, k * 2 * half:(k + 1) * 2 * half], keys_ref.at[k],
                         cnt_ref.at[k], w0_ref.at[k], rk_ref.at[k], w1_ref.at[k],
                         a_s.at[k], b_s.at[k], f_s.at[k], half=half, topk=topk)


def _peer_route(q, keys, tm, hp):
    T = q.shape[0]
    H, _, NK, half = keys.shape
    sl = F32_SUBLANES
    counts = _pair_counts(PEER_TOPK + 1)
    assert PEER_TOPK == 2 * sl and counts[1] <= sl and counts[sl - 1] > 1 >= counts[sl]
    assert NK > PEER_TOPK and NK % sl == 0
    per_key = jax.ShapeDtypeStruct((H, NK, T), F32)
    per_col = jax.ShapeDtypeStruct((H, NK, T), BF16)
    ospec = pl.BlockSpec((hp, NK, tm), lambda i, h: (h, 0, i))
    return pl.pallas_call(
        functools.partial(_peer_route_heads, half=half, topk=PEER_TOPK, hp=hp),
        grid=(T // tm, H // hp),
        in_specs=[pl.BlockSpec((tm, hp * 2 * half), lambda i, h: (i, h)),
                  pl.BlockSpec((hp, 2, NK, half), lambda i, h: (h, 0, 0, 0))],
        out_specs=[ospec, ospec, ospec, ospec],
        out_shape=[per_key, per_key, per_col, per_col],
        scratch_shapes=[pltpu.VMEM((hp, PEER_TOPK + sl, tm), F32)] * 3,
        compiler_params=_params("parallel", "parallel"),
        name="peer_route",
    )(q, keys)


def _peer_mix_body(xnt_ref, xres_ref, u_ref, vt_ref, cnt_ref, w0_ref, rk_ref, w1_ref, o_ref,
                   h_s, gh_s, acc_s, *, tl):
    H, ti, tm = cnt_ref.shape
    NK = rk_ref.shape[1]
    j = pl.program_id(1)

    last = pl.num_programs(1) - 1
    cur = j % 2

    def fold():
        acc_s[...] += jnp.dot(vt_ref[...], gh_s[1 - cur], preferred_element_type=F32)

    def row_tile(row):
        tile = jnp.broadcast_to(row, (BF16_SUBLANES, row.shape[1])).astype(BF16)
        return pltpu.repeat(tile, NK // BF16_SUBLANES, axis=0)

    def build():
        for ii in range(ti):
            rs = slice(ii * NK, (ii + 1) * NK)
            for lc in range(tm // tl):
                ls = slice(lc * tl, (lc + 1) * tl)
                g = None
                for h in range(H):
                    c = row_tile(cnt_ref[h, ii:ii + 1, ls])
                    a = row_tile(w0_ref[h, ii:ii + 1, ls])
                    term = a * jnp.where(rk_ref[h, :, ls] < c, w1_ref[h, :, ls], 0)
                    g = term if g is None else g + term
                gh_s[cur, rs, ls] = g
        h_s[...] = jnp.dot(u_ref[...], xnt_ref[...], preferred_element_type=F32)
        for ii in range(ti):
            rs = slice(ii * NK, (ii + 1) * NK)
            gh_s[cur, rs, :] = gh_s[cur, rs, :] * _gelu_tanh_x2(h_s[rs, :].astype(BF16))

    @pl.when(j == 0)
    def _():
        acc_s[...] = jnp.zeros_like(acc_s)
        build()

    @pl.when(jnp.logical_and(j > 0, j < last))
    def _():
        fold()
        build()

    @pl.when(j == last)
    def _():
        fold()
        o_ref[...] = xres_ref[...] + acc_s[...].T


def _peer_mix(xnt, xres, u_bf, vt_bf, cnt, w0, rk, w1, tm, ti, tl):
    D, T = xnt.shape
    H, NK, _ = cnt.shape
    E = u_bf.shape[0]
    te = ti * NK
    n_e = E // te
    build = lambda j: jnp.minimum(j, n_e - 1)
    fold = lambda j: jnp.maximum(j - 1, 0)
    keys = pl.BlockSpec((H, ti, tm), lambda i, j: (0, build(j), i))
    full = pl.BlockSpec((H, NK, tm), lambda i, j: (0, 0, i))
    return pl.pallas_call(
        functools.partial(_peer_mix_body, tl=tl),
        grid=(T // tm, n_e + 1),
        in_specs=[pl.BlockSpec((D, tm), lambda i, j: (0, i)),
                  pl.BlockSpec((tm, D), lambda i, j: (i, 0)),
                  pl.BlockSpec((te, D), lambda i, j: (build(j), 0)),
                  pl.BlockSpec((D, te), lambda i, j: (0, fold(j))),
                  keys, keys, full, full],
        out_specs=pl.BlockSpec((tm, D), lambda i, j: (i, 0)),
        out_shape=jax.ShapeDtypeStruct((T, D), F32),
        scratch_shapes=[pltpu.VMEM((te, tm), F32), pltpu.VMEM((2, te, tm), BF16),
                        pltpu.VMEM((D, tm), F32)],
        compiler_params=_params("parallel", "arbitrary"),
        name="peer_mix",
    )(xnt, xres, u_bf, vt_bf, cnt, w0, rk, w1)


def _final_norm_body(x_ref, w_ref, o_ref):
    o_ref[...] = _rms(x_ref[...], w_ref[...])


def _final_norm(x2, w, tm):
    T, D = x2.shape
    return pl.pallas_call(
        _final_norm_body,
        grid=(T // tm,),
        in_specs=[pl.BlockSpec((tm, D), lambda i: (i, 0)), pl.BlockSpec((1, D), lambda i: (0, 0))],
        out_specs=pl.BlockSpec((tm, D), lambda i: (i, 0)),
        out_shape=jax.ShapeDtypeStruct((T, D), F32),
        compiler_params=_params("parallel"),
        name="final_norm",
    )(x2, w)


def _rope_table(pos, inv_freq):
    ang = pos[:, None] * inv_freq[None, :]
    ang = jnp.concatenate([ang, ang], axis=-1)
    return jnp.cos(ang), jnp.sin(ang)


def _position_tables(S):
    t = jnp.arange(S, dtype=F32)
    n_rows = S // GRID_W
    rows = jnp.repeat(jnp.arange(n_rows, dtype=F32), GRID_W)
    cols = jnp.tile(jnp.arange(GRID_W, dtype=F32), n_rows)
    ret_inv = 1.0 / (10000.0 ** jnp.linspace(0.0, 1.0, HEAD_DIM // 2, dtype=F32))
    ret_cos, ret_sin = _rope_table(t, ret_inv)
    ax_n = HEAD_DIM // 4
    ax_inv = ROPE_BASE ** (-jnp.arange(ax_n, dtype=F32) / ax_n)
    cr, sr = _rope_table(rows, ax_inv)
    cc, sc = _rope_table(cols, ax_inv)
    ax_cos = jnp.concatenate([cr, cc], axis=-1)
    ax_sin = jnp.concatenate([sr, sc], axis=-1)
    return (jnp.tile(ret_cos, (1, RET_HEADS)), jnp.tile(ret_sin, (1, RET_HEADS)),
            jnp.tile(ax_cos, (1, ATT_HEADS)), jnp.tile(ax_sin, (1, ATT_HEADS)))


def _tiles(B, S, n_keys):
    T = B * S
    return dict(
        tm=min(512, T),
        tb=min(256, S),
        tp=min(512, S),
        tq=min(256, S),
        tk=min(1024, S),
        tr=min(256, T),
        hp=2,
        tmix=min(512, T),
        ti=min(8, n_keys),
        tl=min(256, T),
    )


def kernel(x, ln1_w, w_in, ret_log_decay, ret_gn_w, lru_conv_w, lru_conv_b, lru_gate_w, lru_gate_b,
           lru_lambda, attn_q_norm, attn_k_norm, w_out, ln2_w, peer_wq, peer_keys, peer_u, peer_v,
           lnf_w):
    B, S, D = x.shape
    T = B * S
    depth = w_in.shape[0]
    n_keys = peer_keys.shape[3]
    tl = _tiles(B, S, n_keys)
    widths = (RET_W,) * 4 + (LRU_W,) * 2 + (ATT_W, KV_W, KV_W)
    ret_cos, ret_sin, ax_cos, ax_sin = _position_tables(S)
    x2 = x.reshape(T, D)
    for l in range(depth):
        rq, rk, rv, rg, lx, lgt, aq, ak, av = _inproj(
            x2, ln1_w[l].reshape(1, D), w_in[l].astype(BF16), widths, tl["tm"])
        o_ret = _retention(rq, rk, rv, rg, ret_cos, ret_sin, ret_log_decay[l],
                           ret_gn_w[l].reshape(1, RET_W), B, S)
        o_lru = _lru(lx, lgt, lru_conv_w[l], lru_conv_b[l], lru_gate_w[l], lru_gate_b[l],
                     lru_lambda[l], B, S, tl["tb"])
        qp, kt, vp = _attn_prep(aq, ak, av, ax_cos, ax_sin, attn_q_norm[l], attn_k_norm[l],
                                B, S, tl["tp"])
        o_att = _attention(qp, kt, vp, B, S, tl["tq"], tl["tk"])
        x2, xn, q = _outproj(x2, o_ret, o_lru, o_att, w_out[l].astype(BF16),
                             ln2_w[l].reshape(1, D), peer_wq[l].astype(BF16), tl["tm"])
        cnt, w0, rk, w1 = _peer_route(q, peer_keys[l], tl["tr"], tl["hp"])
        x2 = _peer_mix(xn, x2, peer_u[l].astype(BF16), peer_v[l].T.astype(BF16),
                       cnt, w0, rk, w1, tl["tmix"], tl["ti"], tl["tl"])
    return _final_norm(x2, lnf_w.reshape(1, D), tl["tm"]).reshape(B, S, D)
```

```python
import functools

import jax
import jax.numpy as jnp
from jax import lax
from jax.experimental import pallas as pl
from jax.experimental.pallas import tpu as pltpu

F32 = jnp.float32
BF16 = jnp.bfloat16
EPS = 1e-6
LOG2_E = 1.4426950408889634

GRID_W = 64
CHUNK = 128
HEAD_DIM = 64
RET_HEADS = 4
RET_W = RET_HEADS * HEAD_DIM
LRU_W = 256
LRU_CONV = 4
LRU_C = 8.0
ATT_HEADS = 8
ATT_KV_HEADS = 2
ATT_GROUP = ATT_HEADS // ATT_KV_HEADS
ATT_W = ATT_HEADS * HEAD_DIM
KV_W = ATT_KV_HEADS * HEAD_DIM
ROPE_BASE = 10000.0
PEER_TOPK = 16
F32_SUBLANES = 8
BF16_SUBLANES = 16

VMEM_LIMIT_BYTES = 56 * 1024 * 1024
NEG_INF = float("-inf")


def _params(*sem):
    return pltpu.CompilerParams(dimension_semantics=sem, vmem_limit_bytes=VMEM_LIMIT_BYTES)


def _rms(x, w):
    return x * lax.rsqrt(jnp.mean(x * x, axis=-1, keepdims=True) + EPS) * w


def _gelu_tanh(x):
    return 0.5 * x * (1.0 + jnp.tanh(0.7978845608028654 * (x + 0.044715 * (x * x * x))))


def _gelu_tanh_x2(x):
    return x + x * jnp.tanh(x * (0.7978845608028654 + (0.7978845608028654 * 0.044715) * (x * x)))


def _sigmoid(x):
    return 1.0 / (1.0 + jnp.exp(-x))


def _nt_dot(a, b):
    return lax.dot_general(a, b, (((1,), (1,)), ((), ())), preferred_element_type=F32)


def _group_sum(x, member):
    hi = x.astype(BF16)
    lo = (x - hi.astype(F32)).astype(BF16)
    return (jnp.dot(hi, member, preferred_element_type=F32)
            + jnp.dot(lo, member, preferred_element_type=F32))


def _tn_dot(a, b):
    return lax.dot_general(a, b, (((0,), (0,)), ((), ())), preferred_element_type=F32)


def _inproj_body(x_ref, lnw_ref, w_ref, *out_refs, widths):
    hb = _rms(x_ref[...], lnw_ref[...]).astype(BF16)
    off = 0
    for o_ref, wd in zip(out_refs, widths):
        o_ref[...] = jnp.dot(hb, w_ref[:, off:off + wd],
                             preferred_element_type=F32).astype(o_ref.dtype)
        off += wd


def _inproj(x2, lnw, w_bf, widths, tm):
    T, D = x2.shape
    n_in = w_bf.shape[1]
    return pl.pallas_call(
        functools.partial(_inproj_body, widths=widths),
        grid=(T // tm,),
        in_specs=[pl.BlockSpec((tm, D), lambda i: (i, 0)),
                  pl.BlockSpec((1, D), lambda i: (0, 0)),
                  pl.BlockSpec((D, n_in), lambda i: (0, 0))],
        out_specs=[pl.BlockSpec((tm, wd), lambda i: (i, 0)) for wd in widths],
        out_shape=[jax.ShapeDtypeStruct((T, wd), F32) for wd in widths],
        compiler_params=_params("parallel"),
        name="inproj",
    )(x2, lnw, w_bf)


def _ret_body(ld_ref, q_ref, k_ref, v_ref, g_ref, cos_ref, sin_ref, lgl_ref, gnw_ref, o_ref,
              qr_s, kr_s, st_s, d_s, *, n_chunks):
    C, W = CHUNK, RET_W
    lane = lax.broadcasted_iota(jnp.int32, (1, W), 1)
    first_half = (lane % HEAD_DIM) < (HEAD_DIM // 2)
    head_masks = [(lane // HEAD_DIM) == h for h in range(RET_HEADS)]
    same_head = (lax.broadcasted_iota(jnp.int32, (W, W), 0) // HEAD_DIM
                 == lax.broadcasted_iota(jnp.int32, (W, W), 1) // HEAD_DIM)
    bd = same_head.astype(F32)
    lgf = lgl_ref[0:1, :]
    lgb = lgl_ref[1:2, :]
    idx = lax.broadcasted_iota(jnp.int32, (C, 1), 0).astype(F32)
    decf = jnp.exp(lgf * float(C))
    decb = jnp.exp(lgb * float(C))

    diff = (lax.broadcasted_iota(jnp.int32, (C, C), 0)
            - lax.broadcasted_iota(jnp.int32, (C, C), 1)).astype(F32)
    for h in range(RET_HEADS):
        d_s[h] = jnp.where(diff >= 0.0,
                           jnp.exp(ld_ref[0, h] * jnp.maximum(diff, 0.0)),
                           jnp.exp(ld_ref[1, h] * jnp.maximum(-diff, 0.0)))

    def rot_half(x):
        return jnp.where(first_half, -pltpu.roll(x, W - HEAD_DIM // 2, 1),
                         pltpu.roll(x, HEAD_DIM // 2, 1))

    def prep(c, carry):
        r0 = pl.multiple_of(c * C, C)
        cs = cos_ref[pl.ds(r0, C), :]
        sn = sin_ref[pl.ds(r0, C), :]
        q = q_ref[pl.ds(r0, C), :]
        k = k_ref[pl.ds(r0, C), :]
        qr_s[pl.ds(r0, C), :] = q * cs + rot_half(q) * sn
        kr_s[pl.ds(r0, C), :] = (k * cs + rot_half(k) * sn) * (HEAD_DIM ** -0.5)
        return carry

    lax.fori_loop(0, n_chunks, prep, 0, unroll=4)

    st_s[...] = jnp.zeros((W, W), F32)

    def fwd(c, carry):
        r0 = pl.multiple_of(c * C, C)
        q = qr_s[pl.ds(r0, C), :]
        k = kr_s[pl.ds(r0, C), :]
        v = v_ref[pl.ds(r0, C), :]
        st = st_s[...]
        qw = q * jnp.exp(lgf * (idx + 1.0))
        o_ref[pl.ds(r0, C), :] = jnp.dot(qw.astype(BF16), (st * bd).astype(BF16),
                                         preferred_element_type=F32)
        kw = k * jnp.exp(lgf * (float(C) - 1.0 - idx))
        st_s[...] = st * decf + _tn_dot(kw.astype(BF16), v.astype(BF16))
        return carry

    lax.fori_loop(0, n_chunks, fwd, 0, unroll=4)

    st_s[...] = jnp.zeros((W, W), F32)

    def bwd(t, carry):
        c = n_chunks - 1 - t
        r0 = pl.multiple_of(c * C, C)
        q = qr_s[pl.ds(r0, C), :]
        k = kr_s[pl.ds(r0, C), :]
        v = v_ref[pl.ds(r0, C), :]
        st = st_s[...]
        qw = q * jnp.exp(lgb * (float(C) - idx))
        vb = v.astype(BF16)
        kb = k.astype(BF16)
        o = o_ref[pl.ds(r0, C), :] + jnp.dot(qw.astype(BF16), (st * bd).astype(BF16),
                                              preferred_element_type=F32)
        kw = k * jnp.exp(lgb * idx)
        st_s[...] = st * decb + _tn_dot(kw.astype(BF16), vb)
        for h in range(RET_HEADS):
            qm = jnp.where(head_masks[h], q, 0.0).astype(BF16)
            s = _nt_dot(qm, kb) * d_s[h]
            oh = jnp.dot(s.astype(BF16), vb, preferred_element_type=F32)
            o = o + jnp.where(head_masks[h], oh, 0.0)
        ms = _group_sum(o * o, bd.astype(BF16)) * (1.0 / HEAD_DIM)
        on = o * lax.rsqrt(ms + EPS) * gnw_ref[...]
        g = g_ref[pl.ds(r0, C), :]
        o_ref[pl.ds(r0, C), :] = (g * _sigmoid(g)) * on
        return carry

    lax.fori_loop(0, n_chunks, bwd, 0, unroll=4)


def _retention(rq, rk, rv, rg, cos, sin, log_decay, gn_w, B, S):
    W = RET_W
    lgl = jnp.repeat(log_decay, HEAD_DIM, axis=1)
    seq = pl.BlockSpec((S, W), lambda b: (b, 0), pipeline_mode=pl.Buffered(1))
    tab = pl.BlockSpec((S, W), lambda b: (0, 0), pipeline_mode=pl.Buffered(1))
    return pl.pallas_call(
        functools.partial(_ret_body, n_chunks=S // CHUNK),
        grid=(B,),
        in_specs=[pl.BlockSpec(memory_space=pltpu.SMEM), seq, seq, seq, seq, tab, tab,
                  pl.BlockSpec((2, W), lambda b: (0, 0)),
                  pl.BlockSpec((1, W), lambda b: (0, 0))],
        out_specs=pl.BlockSpec((S, W), lambda b: (b, 0)),
        out_shape=jax.ShapeDtypeStruct((B * S, W), F32),
        scratch_shapes=[pltpu.VMEM((S, W), F32), pltpu.VMEM((S, W), F32),
                        pltpu.VMEM((W, W), F32), pltpu.VMEM((RET_HEADS, CHUNK, CHUNK), F32)],
        compiler_params=_params("parallel"),
        name="retention",
    )(log_decay, rq, rk, rv, rg, cos, sin, lgl, gn_w)


def _lru_body(x_ref, gt_ref, cw_ref, cb_ref, gw_ref, gb_ref, lam_ref, o_ref, xc_s, h_s,
              *, n_blocks, tb):
    W = LRU_W
    S = n_blocks * tb
    z = -lam_ref[...]
    softplus = jnp.maximum(z, 0.0) + jnp.log(1.0 + jnp.exp(-jnp.abs(z)))
    cl = -LRU_C * softplus
    row = lax.broadcasted_iota(jnp.int32, (tb, 1), 0)
    left = LRU_CONV // 2

    def conv_block(kb):
        r0 = pl.multiple_of(kb * tb, tb)
        halo = F32_SUBLANES
        prev = x_ref[pl.ds(pl.multiple_of(jnp.maximum(r0 - halo, 0), halo), halo), :]
        nxt = x_ref[pl.ds(pl.multiple_of(jnp.minimum(r0 + tb, S - halo), halo), halo), :]
        ext = jnp.concatenate([prev, x_ref[pl.ds(r0, tb), :], nxt], axis=0)
        t = row + r0
        acc = cb_ref[...] + jnp.zeros((tb, W), F32)
        for j in range(LRU_CONV):
            off = j - left
            if off == 0:
                xs = ext[halo:halo + tb]
            else:
                xs = pltpu.roll(ext, (-off) % (tb + 2 * halo), 0)[halo:halo + tb]
                xs = jnp.where((t + off >= 0) & (t + off < S), xs, 0.0)
            acc = acc + xs * cw_ref[j:j + 1, :]
        return acc

    def gates(xc, d):
        g = jnp.dot(xc.astype(BF16), gw_ref[:, d * 2 * W:(d + 1) * 2 * W],
                    preferred_element_type=F32) + gb_ref[:, d * 2 * W:(d + 1) * 2 * W]
        r = _sigmoid(g[:, :W])
        i = _sigmoid(g[:, W:])
        log_a = cl[d:d + 1, :] * r
        a = jnp.exp(log_a)
        b = jnp.sqrt(-jnp.tanh(log_a) * (a * a + 1.0)) * (i * xc)
        return a, b

    def scan_block(a, b, reverse):
        d = 1
        while d < tb:
            if reverse:
                keep = row < tb - d
                a_sh = jnp.where(keep, pltpu.roll(a, tb - d, 0), 1.0)
                b_sh = jnp.where(keep, pltpu.roll(b, tb - d, 0), 0.0)
            else:
                keep = row >= d
                a_sh = jnp.where(keep, pltpu.roll(a, d, 0), 1.0)
                b_sh = jnp.where(keep, pltpu.roll(b, d, 0), 0.0)
            b = a * b_sh + b
            a = a * a_sh
            d *= 2
        return a, b

    def fwd(kb, carry):
        r0 = pl.multiple_of(kb * tb, tb)
        xc = conv_block(kb)
        xc_s[pl.ds(r0, tb), :] = xc
        a, b = gates(xc, 0)
        a, b = scan_block(a, b, False)
        h = a * carry + b
        h_s[pl.ds(r0, tb), :] = h
        return h[tb - 1:tb, :]

    lax.fori_loop(0, n_blocks, fwd, jnp.zeros((1, W), F32))

    def bwd(t, carry):
        kb = n_blocks - 1 - t
        r0 = pl.multiple_of(kb * tb, tb)
        xc = xc_s[pl.ds(r0, tb), :]
        a, b = gates(xc, 1)
        a, b = scan_block(a, b, True)
        h = a * carry + b
        o_ref[pl.ds(r0, tb), :] = (h_s[pl.ds(r0, tb), :] + h) * _gelu_tanh(gt_ref[pl.ds(r0, tb), :])
        return h[0:1, :]

    lax.fori_loop(0, n_blocks, bwd, jnp.zeros((1, W), F32))


def _lru(lx, lgt, conv_w, conv_b, gate_w, gate_b, lam, B, S, tb):
    W = LRU_W
    nb = W // HEAD_DIM
    eye = jnp.eye(nb, dtype=F32)
    gw = jnp.einsum('dgnkm,nj->nkdgjm', gate_w, eye).reshape(W, 4 * W).astype(BF16)
    gb = gate_b.reshape(1, 4 * W)
    seq = pl.BlockSpec((S, W), lambda b: (b, 0))
    const = lambda shape: pl.BlockSpec(shape, lambda b: (0,) * len(shape))
    return pl.pallas_call(
        functools.partial(_lru_body, n_blocks=S // tb, tb=tb),
        grid=(B,),
        in_specs=[seq, seq, const((LRU_CONV, W)), const((1, W)), const((W, 4 * W)),
                  const((1, 4 * W)), const((2, W))],
        out_specs=seq,
        out_shape=jax.ShapeDtypeStruct((B * S, W), F32),
        scratch_shapes=[pltpu.VMEM((S, W), F32), pltpu.VMEM((S, W), F32)],
        compiler_params=_params("parallel"),
        name="rglru",
    )(lx, lgt, conv_w, conv_b.reshape(1, W), gw, gb, lam)


def _attn_prep_body(q_ref, k_ref, v_ref, cos_ref, sin_ref, qnw_ref, knw_ref, vrep_ref, rept_ref,
                    qo_ref, kt_ref, vo_ref):
    def head_norm_rot(x, w, cs, sn):
        W = x.shape[-1]
        lane = lax.broadcasted_iota(jnp.int32, (1, W), 1)
        same_head = (lax.broadcasted_iota(jnp.int32, (W, W), 0) // HEAD_DIM
                     == lax.broadcasted_iota(jnp.int32, (W, W), 1) // HEAD_DIM)
        ms = _group_sum(x * x, same_head.astype(BF16)) * (1.0 / HEAD_DIM)
        xn = x * lax.rsqrt(ms + EPS) * w
        quarter = HEAD_DIM // 4
        rot = jnp.where((lane % (2 * quarter)) < quarter, -pltpu.roll(xn, W - quarter, 1),
                        pltpu.roll(xn, quarter, 1))
        return xn * cs + rot * sn

    cs = cos_ref[...]
    sn = sin_ref[...]
    q = head_norm_rot(q_ref[...], qnw_ref[...], cs, sn)
    qo_ref[...] = (q * (HEAD_DIM ** -0.5 * LOG2_E)).astype(BF16)
    k = head_norm_rot(k_ref[...], knw_ref[...], cs[:, :KV_W], sn[:, :KV_W]).astype(BF16)
    kt_ref[...] = _nt_dot(rept_ref[...], k).astype(BF16)
    VW = 2 * HEAD_DIM
    vlane = lax.broadcasted_iota(jnp.int32, (1, ATT_KV_HEADS * VW), 1)
    ones = jnp.where(vlane % VW >= HEAD_DIM, 1.0, 0.0)
    vo_ref[...] = (jnp.dot(v_ref[...].astype(BF16), vrep_ref[...], preferred_element_type=F32)
                   + ones).astype(BF16)


def _attn_prep(aq, ak, av, cos, sin, qn_w, kn_w, B, S, tm):
    T = B * S
    ns = S // tm
    GW = ATT_GROUP * HEAD_DIM
    src = jnp.arange(ATT_KV_HEADS * GW)
    src = (src // GW) * HEAD_DIM + src % HEAD_DIM
    rep = (jnp.arange(KV_W)[:, None] == src[None, :]).astype(BF16)
    VW = 2 * HEAD_DIM
    vsrc = jnp.arange(ATT_KV_HEADS * VW)
    vsrc = jnp.where(vsrc % VW < HEAD_DIM, (vsrc // VW) * HEAD_DIM + vsrc % VW, -1)
    vrep = (jnp.arange(KV_W)[:, None] == vsrc[None, :]).astype(BF16)
    return pl.pallas_call(
        _attn_prep_body,
        grid=(B, ns),
        in_specs=[pl.BlockSpec((tm, ATT_W), lambda b, s: (b * ns + s, 0)),
                  pl.BlockSpec((tm, KV_W), lambda b, s: (b * ns + s, 0)),
                  pl.BlockSpec((tm, KV_W), lambda b, s: (b * ns + s, 0)),
                  pl.BlockSpec((tm, ATT_W), lambda b, s: (s, 0)),
                  pl.BlockSpec((tm, ATT_W), lambda b, s: (s, 0)),
                  pl.BlockSpec((1, ATT_W), lambda b, s: (0, 0)),
                  pl.BlockSpec((1, KV_W), lambda b, s: (0, 0)),
                  pl.BlockSpec((KV_W, ATT_KV_HEADS * VW), lambda b, s: (0, 0)),
                  pl.BlockSpec((ATT_KV_HEADS * GW, KV_W), lambda b, s: (0, 0))],
        out_specs=[pl.BlockSpec((tm, ATT_W), lambda b, s: (b * ns + s, 0)),
                   pl.BlockSpec((None, ATT_KV_HEADS * GW, tm), lambda b, s: (b, 0, s)),
                   pl.BlockSpec((tm, ATT_KV_HEADS * VW), lambda b, s: (b * ns + s, 0))],
        out_shape=[jax.ShapeDtypeStruct((T, ATT_W), BF16),
                   jax.ShapeDtypeStruct((B, ATT_KV_HEADS * GW, S), BF16),
                   jax.ShapeDtypeStruct((T, ATT_KV_HEADS * VW), BF16)],
        compiler_params=_params("parallel", "parallel"),
        name="attn_prep",
    )(aq, ak, av, cos, sin, jnp.tile(qn_w, ATT_HEADS).reshape(1, ATT_W),
      jnp.tile(kn_w, ATT_KV_HEADS).reshape(1, KV_W), vrep, rep.T)


def _attn_body(q_ref, kt_ref, v_ref, o_ref, *, tq, tk, nk, n_part):
    GW = ATT_GROUP * HEAD_DIM
    lane = lax.broadcasted_iota(jnp.int32, (1, GW), 1)
    masks = [(lane // HEAD_DIM) == g for g in range(ATT_GROUP)]
    q = q_ref[...]
    zero = jnp.zeros_like(q)
    qs = jnp.concatenate([jnp.where(m, q, zero) for m in masks], axis=0)
    rows = ATT_GROUP * tq
    VW = 2 * HEAD_DIM
    part = rows // n_part
    m_run = [jnp.full((part, 1), NEG_INF, F32) for _ in range(n_part)]
    acc = [jnp.zeros((part, VW), F32) for _ in range(n_part)]
    for c in range(nk):
        for r in range(n_part):
            s = jnp.dot(qs[r * part:(r + 1) * part], kt_ref[:, c * tk:(c + 1) * tk],
                        preferred_element_type=F32)
            m_new = jnp.maximum(m_run[r], jnp.max(s, axis=-1, keepdims=True))
            alpha = jnp.exp2(m_run[r] - m_new)
            p = jnp.exp2(s - m_new)
            acc[r] = alpha * acc[r] + jnp.dot(p.astype(BF16), v_ref[c * tk:(c + 1) * tk, :],
                                              preferred_element_type=F32)
            m_run[r] = m_new
    acc = jnp.concatenate(acc, axis=0)
    on = acc * pltpu.roll(1.0 / acc, HEAD_DIM, 1)
    low = lax.broadcasted_iota(jnp.int32, (1, VW), 1) < HEAD_DIM
    heads = [on[g * tq:(g + 1) * tq] for g in range(ATT_GROUP)]
    for pair in range(ATT_GROUP // 2):
        o_ref[:, pair * VW:(pair + 1) * VW] = jnp.where(
            low, heads[2 * pair], pltpu.roll(heads[2 * pair + 1], HEAD_DIM, 1))


def _attention(qp, kt, vp, B, S, tq, tk):
    T = B * S
    nq = S // tq
    GW = ATT_GROUP * HEAD_DIM
    return pl.pallas_call(
        functools.partial(_attn_body, tq=tq, tk=tk, nk=S // tk, n_part=2),
        grid=(B, ATT_KV_HEADS, nq),
        in_specs=[pl.BlockSpec((tq, GW), lambda b, h, i: (b * nq + i, h)),
                  pl.BlockSpec((None, GW, S), lambda b, h, i: (b, h, 0)),
                  pl.BlockSpec((S, 2 * HEAD_DIM), lambda b, h, i: (b, h))],
        out_specs=pl.BlockSpec((tq, GW), lambda b, h, i: (b * nq + i, h)),
        out_shape=jax.ShapeDtypeStruct((T, ATT_W), F32),
        compiler_params=_params("parallel", "parallel", "parallel"),
        name="attention",
    )(qp, kt, vp)


def _outproj_body(x_ref, oret_ref, olru_ref, oatt_ref, w_ref, ln2_ref, wq_ref,
                  xo_ref, xn_ref, q_ref):
    y = jnp.dot(oret_ref[...].astype(BF16), w_ref[0:RET_W, :], preferred_element_type=F32)
    y = y + jnp.dot(olru_ref[...].astype(BF16), w_ref[RET_W:RET_W + LRU_W, :],
                    preferred_element_type=F32)
    y = y + jnp.dot(oatt_ref[...].astype(BF16), w_ref[RET_W + LRU_W:, :],
                    preferred_element_type=F32)
    x = x_ref[...] + y
    xo_ref[...] = x
    xn = _rms(x, ln2_ref[...]).astype(BF16)
    xn_ref[...] = xn.astype(F32).T.astype(BF16)
    q_ref[...] = jnp.dot(xn, wq_ref[...], preferred_element_type=F32)


def _outproj(x2, o_ret, o_lru, o_att, w_out_bf, ln2, wq_bf, tm):
    T, D = x2.shape
    QW = wq_bf.shape[1]
    MW = w_out_bf.shape[0]
    tok = lambda w: pl.BlockSpec((tm, w), lambda i: (i, 0))
    return pl.pallas_call(
        _outproj_body,
        grid=(T // tm,),
        in_specs=[tok(D), tok(RET_W), tok(LRU_W), tok(ATT_W),
                  pl.BlockSpec((MW, D), lambda i: (0, 0)),
                  pl.BlockSpec((1, D), lambda i: (0, 0)),
                  pl.BlockSpec((D, QW), lambda i: (0, 0))],
        out_specs=[tok(D), pl.BlockSpec((D, tm), lambda i: (0, i)), tok(QW)],
        out_shape=[jax.ShapeDtypeStruct((T, D), F32), jax.ShapeDtypeStruct((D, T), BF16),
                   jax.ShapeDtypeStruct((T, QW), F32)],
        compiler_params=_params("parallel"),
        name="outproj",
    )(x2, o_ret, o_lru, o_att, w_out_bf, ln2, wq_bf)


def _sorting_network(n):
    pairs = []
    p = 1
    while p < n:
        k = p
        while k >= 1:
            for j in range(k % p, n - k, 2 * k):
                for i in range(min(k, n - j - k)):
                    if (i + j) // (2 * p) == (i + j + k) // (2 * p):
                        pairs.append((i + j, i + j + k))
            k //= 2
        p *= 2
    return pairs


def _pair_counts(n):
    return [n // (i + 1) for i in range(n)]


def _peer_route_body(q_ref, keys_ref, cnt_ref, w0_ref, rk_ref, w1_ref, a_s, b_s, f_s,
                     *, half, topk):
    s0 = _nt_dot(keys_ref[0], q_ref[:, :half])
    s1 = _nt_dot(keys_ref[1], q_ref[:, half:])

    def top_sorted(tiles, n, out_ref):
        rows = list(tiles) + [None] * (pl.next_power_of_2(len(tiles)) - len(tiles))
        for lo_i, hi_i in _sorting_network(len(rows)):
            x, y = rows[lo_i], rows[hi_i]
            if y is None:
                continue
            if x is None:
                rows[lo_i], rows[hi_i] = y, None
            else:
                rows[lo_i], rows[hi_i] = jnp.maximum(x, y), jnp.minimum(x, y)
        rows = [t for t in rows if t is not None]
        for r in range(n):
            m = jnp.max(rows[0], axis=0, keepdims=True)
            out_ref[r:r + 1, :] = m
            hit = rows[0] == m
            for v in range(min(len(rows), n - r - 1)):
                nxt = rows[v + 1] if v + 1 < len(rows) else NEG_INF
                rows[v] = jnp.where(hit, nxt, rows[v])

    def sublane_tiles(x):
        sl = F32_SUBLANES
        return [x[sl * v:sl * (v + 1), :] for v in range(x.shape[0] // sl)]

    n = topk + 1
    top_sorted(sublane_tiles(s0), n, a_s)
    top_sorted(sublane_tiles(s1), n, b_s)
    a = a_s[0:topk, :]
    b = b_s[0:topk, :]
    counts = _pair_counts(n)
    sl = F32_SUBLANES
    row8 = lax.broadcasted_iota(jnp.int32, (sl, 1), 0)
    parts = [a[0:1, :] + b[0:sl, :], a[0:1, :] + b[sl:2 * sl, :]]
    i = 1
    while counts[i] > 1:
        parts.append(jnp.where(row8 < counts[i], a[i:i + 1, :] + b[0:sl, :], NEG_INF))
        i += 1
    parts.append(a[i:, :] + b[0:1, :])
    last = jnp.where(row8 == 0, a[0:1, :] + b_s[topk:n, :],
                     jnp.where(row8 == 1, a_s[topk:n, :] + b[0:1, :], NEG_INF))
    parts.append(last)
    top_sorted(parts, n, f_s)
    f = f_s[0:topk, :]
    z = jnp.sum(jnp.exp(f - f[0:1, :]), axis=0, keepdims=True)
    tau = 0.5 * (f[topk - 1:topk, :] + f_s[topk:n, :])
    thr = tau - s0
    cnt = jnp.zeros_like(thr)
    for r in range(topk):
        cnt = jnp.where(b[r:r + 1, :] >= thr, float(r + 1), cnt)
    cnt_ref[...] = cnt
    rank1 = jnp.zeros_like(s1)
    for r in range(topk):
        rank1 = jnp.where(b[r:r + 1, :] > s1, float(r + 1), rank1)
    w0_ref[...] = jnp.exp(s0 - a[0:1, :]) * (0.5 / z)
    rk_ref[...] = rank1.astype(BF16)
    w1_ref[...] = jnp.exp(s1 - b[0:1, :]).astype(BF16)


def _peer_route_heads(q_ref, keys_ref, cnt_ref, w0_ref, rk_ref, w1_ref, a_s, b_s, f_s,
                      *, half, topk, hp):
    for k in range(hp):
        _peer_route_body(q_ref.at[:, k * 2 * half:(k + 1) * 2 * half], keys_ref.at[k],
                         cnt_ref.at[k], w0_ref.at[k], rk_ref.at[k], w1_ref.at[k],
                         a_s.at[k], b_s.at[k], f_s.at[k], half=half, topk=topk)


def _peer_route(q, keys, tm, hp):
    T = q.shape[0]
    H, _, NK, half = keys.shape
    sl = F32_SUBLANES
    counts = _pair_counts(PEER_TOPK + 1)
    assert PEER_TOPK == 2 * sl and counts[1] <= sl and counts[sl - 1] > 1 >= counts[sl]
    assert NK > PEER_TOPK and NK % sl == 0
    per_key = jax.ShapeDtypeStruct((H, NK, T), F32)
    per_col = jax.ShapeDtypeStruct((H, NK, T), BF16)
    ospec = pl.BlockSpec((hp, NK, tm), lambda i, h: (h, 0, i))
    return pl.pallas_call(
        functools.partial(_peer_route_heads, half=half, topk=PEER_TOPK, hp=hp),
        grid=(T // tm, H // hp),
        in_specs=[pl.BlockSpec((tm, hp * 2 * half), lambda i, h: (i, h)),
                  pl.BlockSpec((hp, 2, NK, half), lambda i, h: (h, 0, 0, 0))],
        out_specs=[ospec, ospec, ospec, ospec],
        out_shape=[per_key, per_key, per_col, per_col],
        scratch_shapes=[pltpu.VMEM((hp, PEER_TOPK + sl, tm), F32)] * 3,
        compiler_params=_params("parallel", "parallel"),
        name="peer_route",
    )(q, keys)


def _peer_mix_body(xnt_ref, xres_ref, u_ref, vt_ref, cnt_ref, w0_ref, rk_ref, w1_ref, o_ref,
                   h_s, gh_s, acc_s, *, tl):
    H, ti, tm = cnt_ref.shape
    NK = rk_ref.shape[1]
    j = pl.program_id(1)

    last = pl.num_programs(1) - 1
    cur = j % 2

    def fold():
        acc_s[...] += jnp.dot(vt_ref[...], gh_s[1 - cur], preferred_element_type=F32)

    def row_tile(row):
        tile = jnp.broadcast_to(row, (BF16_SUBLANES, row.shape[1])).astype(BF16)
        return pltpu.repeat(tile, NK // BF16_SUBLANES, axis=0)

    def build():
        for ii in range(ti):
            rs = slice(ii * NK, (ii + 1) * NK)
            for lc in range(tm // tl):
                ls = slice(lc * tl, (lc + 1) * tl)
                g = None
                for h in range(H):
                    c = row_tile(cnt_ref[h, ii:ii + 1, ls])
                    a = row_tile(w0_ref[h, ii:ii + 1, ls])
                    term = a * jnp.where(rk_ref[h, :, ls] < c, w1_ref[h, :, ls], 0)
                    g = term if g is None else g + term
                gh_s[cur, rs, ls] = g
        h_s[...] = jnp.dot(u_ref[...], xnt_ref[...], preferred_element_type=F32)
        for ii in range(ti):
            rs = slice(ii * NK, (ii + 1) * NK)
            gh_s[cur, rs, :] = gh_s[cur, rs, :] * _gelu_tanh_x2(h_s[rs, :].astype(BF16))

    @pl.when(j == 0)
    def _():
        acc_s[...] = jnp.zeros_like(acc_s)
        build()

    @pl.when(jnp.logical_and(j > 0, j < last))
    def _():
        fold()
        build()

    @pl.when(j == last)
    def _():
        fold()
        o_ref[...] = xres_ref[...] + acc_s[...].T


def _peer_mix(xnt, xres, u_bf, vt_bf, cnt, w0, rk, w1, tm, ti, tl):
    D, T = xnt.shape
    H, NK, _ = cnt.shape
    E = u_bf.shape[0]
    te = ti * NK
    n_e = E // te
    build = lambda j: jnp.minimum(j, n_e - 1)
    fold = lambda j: jnp.maximum(j - 1, 0)
    keys = pl.BlockSpec((H, ti, tm), lambda i, j: (0, build(j), i))
    full = pl.BlockSpec((H, NK, tm), lambda i, j: (0, 0, i))
    return pl.pallas_call(
        functools.partial(_peer_mix_body, tl=tl),
        grid=(T // tm, n_e + 1),
        in_specs=[pl.BlockSpec((D, tm), lambda i, j: (0, i)),
                  pl.BlockSpec((tm, D), lambda i, j: (i, 0)),
                  pl.BlockSpec((te, D), lambda i, j: (build(j), 0)),
                  pl.BlockSpec((D, te), lambda i, j: (0, fold(j))),
                  keys, keys, full, full],
        out_specs=pl.BlockSpec((tm, D), lambda i, j: (i, 0)),
        out_shape=jax.ShapeDtypeStruct((T, D), F32),
        scratch_shapes=[pltpu.VMEM((te, tm), F32), pltpu.VMEM((2, te, tm), BF16),
                        pltpu.VMEM((D, tm), F32)],
        compiler_params=_params("parallel", "arbitrary"),
        name="peer_mix",
    )(xnt, xres, u_bf, vt_bf, cnt, w0, rk, w1)


def _final_norm_body(x_ref, w_ref, o_ref):
    o_ref[...] = _rms(x_ref[...], w_ref[...])


def _final_norm(x2, w, tm):
    T, D = x2.shape
    return pl.pallas_call(
        _final_norm_body,
        grid=(T // tm,),
        in_specs=[pl.BlockSpec((tm, D), lambda i: (i, 0)), pl.BlockSpec((1, D), lambda i: (0, 0))],
        out_specs=pl.BlockSpec((tm, D), lambda i: (i, 0)),
        out_shape=jax.ShapeDtypeStruct((T, D), F32),
        compiler_params=_params("parallel"),
        name="final_norm",
    )(x2, w)


def _rope_table(pos, inv_freq):
    ang = pos[:, None] * inv_freq[None, :]
    ang = jnp.concatenate([ang, ang], axis=-1)
    return jnp.cos(ang), jnp.sin(ang)


def _position_tables(S):
    t = jnp.arange(S, dtype=F32)
    n_rows = S // GRID_W
    rows = jnp.repeat(jnp.arange(n_rows, dtype=F32), GRID_W)
    cols = jnp.tile(jnp.arange(GRID_W, dtype=F32), n_rows)
    ret_inv = 1.0 / (10000.0 ** jnp.linspace(0.0, 1.0, HEAD_DIM // 2, dtype=F32))
    ret_cos, ret_sin = _rope_table(t, ret_inv)
    ax_n = HEAD_DIM // 4
    ax_inv = ROPE_BASE ** (-jnp.arange(ax_n, dtype=F32) / ax_n)
    cr, sr = _rope_table(rows, ax_inv)
    cc, sc = _rope_table(cols, ax_inv)
    ax_cos = jnp.concatenate([cr, cc], axis=-1)
    ax_sin = jnp.concatenate([sr, sc], axis=-1)
    return (jnp.tile(ret_cos, (1, RET_HEADS)), jnp.tile(ret_sin, (1, RET_HEADS)),
            jnp.tile(ax_cos, (1, ATT_HEADS)), jnp.tile(ax_sin, (1, ATT_HEADS)))


def _tiles(B, S, n_keys):
    T = B * S
    return dict(
        tm=min(512, T),
        tb=min(256, S),
        tp=min(512, S),
        tq=min(256, S),
        tk=min(1024, S),
        tr=min(256, T),
        hp=2,
        tmix=min(512, T),
        ti=min(8, n_keys),
        tl=min(256, T),
    )


def kernel(x, ln1_w, w_in, ret_log_decay, ret_gn_w, lru_conv_w, lru_conv_b, lru_gate_w, lru_gate_b,
           lru_lambda, attn_q_norm, attn_k_norm, w_out, ln2_w, peer_wq, peer_keys, peer_u, peer_v,
           lnf_w):
    B, S, D = x.shape
    T = B * S
    depth = w_in.shape[0]
    n_keys = peer_keys.shape[3]
    tl = _tiles(B, S, n_keys)
    widths = (RET_W,) * 4 + (LRU_W,) * 2 + (ATT_W, KV_W, KV_W)
    ret_cos, ret_sin, ax_cos, ax_sin = _position_tables(S)
    x2 = x.reshape(T, D)
    for l in range(depth):
        rq, rk, rv, rg, lx, lgt, aq, ak, av = _inproj(
            x2, ln1_w[l].reshape(1, D), w_in[l].astype(BF16), widths, tl["tm"])
        o_ret = _retention(rq, rk, rv, rg, ret_cos, ret_sin, ret_log_decay[l],
                           ret_gn_w[l].reshape(1, RET_W), B, S)
        o_lru = _lru(lx, lgt, lru_conv_w[l], lru_conv_b[l], lru_gate_w[l], lru_gate_b[l],
                     lru_lambda[l], B, S, tl["tb"])
        qp, kt, vp = _attn_prep(aq, ak, av, ax_cos, ax_sin, attn_q_norm[l], attn_k_norm[l],
                                B, S, tl["tp"])
        o_att = _attention(qp, kt, vp, B, S, tl["tq"], tl["tk"])
        x2, xn, q = _outproj(x2, o_ret, o_lru, o_att, w_out[l].astype(BF16),
                             ln2_w[l].reshape(1, D), peer_wq[l].astype(BF16), tl["tm"])
        cnt, w0, rk, w1 = _peer_route(q, peer_keys[l], tl["tr"], tl["hp"])
        x2 = _peer_mix(xn, x2, peer_u[l].astype(BF16), peer_v[l].T.astype(BF16),
                       cnt, w0, rk, w1, tl["tmix"], tl["ti"], tl["tl"])
    return _final_norm(x2, lnf_w.reshape(1, D), tl["tm"]).reshape(B, S, D)
```

```python
import functools

import jax
import jax.numpy as jnp
from jax import lax
from jax.experimental import pallas as pl
from jax.experimental.pallas import tpu as pltpu

F32 = jnp.float32
BF16 = jnp.bfloat16
EPS = 1e-6
LOG2_E = 1.4426950408889634

GRID_W = 64
CHUNK = 128
HEAD_DIM = 64
RET_HEADS = 4
RET_W = RET_HEADS * HEAD_DIM
LRU_W = 256
LRU_CONV = 4
LRU_C = 8.0
ATT_HEADS = 8
ATT_KV_HEADS = 2
ATT_GROUP = ATT_HEADS // ATT_KV_HEADS
ATT_W = ATT_HEADS * HEAD_DIM
KV_W = ATT_KV_HEADS * HEAD_DIM
ROPE_BASE = 10000.0
PEER_TOPK = 16
F32_SUBLANES = 8
BF16_SUBLANES = 16

VMEM_LIMIT_BYTES = 56 * 1024 * 1024
NEG_INF = float("-inf")


def _params(*sem):
    return pltpu.CompilerParams(dimension_semantics=sem, vmem_limit_bytes=VMEM_LIMIT_BYTES)


def _rms(x, w):
    return x * lax.rsqrt(jnp.mean(x * x, axis=-1, keepdims=True) + EPS) * w


def _gelu_tanh(x):
    return 0.5 * x * (1.0 + jnp.tanh(0.7978845608028654 * (x + 0.044715 * (x * x * x))))


def _gelu_tanh_x2(x):
    return x + x * jnp.tanh(x * (0.7978845608028654 + (0.7978845608028654 * 0.044715) * (x * x)))


def _sigmoid(x):
    return 1.0 / (1.0 + jnp.exp(-x))


def _nt_dot(a, b):
    return lax.dot_general(a, b, (((1,), (1,)), ((), ())), preferred_element_type=F32)


def _group_sum(x, member):
    hi = x.astype(BF16)
    lo = (x - hi.astype(F32)).astype(BF16)
    return (jnp.dot(hi, member, preferred_element_type=F32)
            + jnp.dot(lo, member, preferred_element_type=F32))


def _tn_dot(a, b):
    return lax.dot_general(a, b, (((0,), (0,)), ((), ())), preferred_element_type=F32)


def _inproj_body(x_ref, lnw_ref, w_ref, *out_refs, widths):
    hb = _rms(x_ref[...], lnw_ref[...]).astype(BF16)
    off = 0
    for o_ref, wd in zip(out_refs, widths):
        o_ref[...] = jnp.dot(hb, w_ref[:, off:off + wd],
                             preferred_element_type=F32).astype(o_ref.dtype)
        off += wd


def _inproj(x2, lnw, w_bf, widths, tm):
    T, D = x2.shape
    n_in = w_bf.shape[1]
    return pl.pallas_call(
        functools.partial(_inproj_body, widths=widths),
        grid=(T // tm,),
        in_specs=[pl.BlockSpec((tm, D), lambda i: (i, 0)),
                  pl.BlockSpec((1, D), lambda i: (0, 0)),
                  pl.BlockSpec((D, n_in), lambda i: (0, 0))],
        out_specs=[pl.BlockSpec((tm, wd), lambda i: (i, 0)) for wd in widths],
        out_shape=[jax.ShapeDtypeStruct((T, wd), F32) for wd in widths],
        compiler_params=_params("parallel"),
        name="inproj",
    )(x2, lnw, w_bf)


def _ret_body(ld_ref, q_ref, k_ref, v_ref, g_ref, cos_ref, sin_ref, lgl_ref, gnw_ref, o_ref,
              qr_s, kr_s, st_s, d_s, *, n_chunks):
    C, W = CHUNK, RET_W
    lane = lax.broadcasted_iota(jnp.int32, (1, W), 1)
    first_half = (lane % HEAD_DIM) < (HEAD_DIM // 2)
    head_masks = [(lane // HEAD_DIM) == h for h in range(RET_HEADS)]
    same_head = (lax.broadcasted_iota(jnp.int32, (W, W), 0) // HEAD_DIM
                 == lax.broadcasted_iota(jnp.int32, (W, W), 1) // HEAD_DIM)
    bd = same_head.astype(F32)
    lgf = lgl_ref[0:1, :]
    lgb = lgl_ref[1:2, :]
    idx = lax.broadcasted_iota(jnp.int32, (C, 1), 0).astype(F32)
    decf = jnp.exp(lgf * float(C))
    decb = jnp.exp(lgb * float(C))

    diff = (lax.broadcasted_iota(jnp.int32, (C, C), 0)
            - lax.broadcasted_iota(jnp.int32, (C, C), 1)).astype(F32)
    for h in range(RET_HEADS):
        d_s[h] = jnp.where(diff >= 0.0,
                           jnp.exp(ld_ref[0, h] * jnp.maximum(diff, 0.0)),
                           jnp.exp(ld_ref[1, h] * jnp.maximum(-diff, 0.0)))

    def rot_half(x):
        return jnp.where(first_half, -pltpu.roll(x, W - HEAD_DIM // 2, 1),
                         pltpu.roll(x, HEAD_DIM // 2, 1))

    def prep(c, carry):
        r0 = pl.multiple_of(c * C, C)
        cs = cos_ref[pl.ds(r0, C), :]
        sn = sin_ref[pl.ds(r0, C), :]
        q = q_ref[pl.ds(r0, C), :]
        k = k_ref[pl.ds(r0, C), :]
        qr_s[pl.ds(r0, C), :] = q * cs + rot_half(q) * sn
        kr_s[pl.ds(r0, C), :] = (k * cs + rot_half(k) * sn) * (HEAD_DIM ** -0.5)
        return carry

    lax.fori_loop(0, n_chunks, prep, 0, unroll=4)

    st_s[...] = jnp.zeros((W, W), F32)

    def fwd(c, carry):
        r0 = pl.multiple_of(c * C, C)
        q = qr_s[pl.ds(r0, C), :]
        k = kr_s[pl.ds(r0, C), :]
        v = v_ref[pl.ds(r0, C), :]
        st = st_s[...]
        qw = q * jnp.exp(lgf * (idx + 1.0))
        o_ref[pl.ds(r0, C), :] = jnp.dot(qw.astype(BF16), (st * bd).astype(BF16),
                                         preferred_element_type=F32)
        kw = k * jnp.exp(lgf * (float(C) - 1.0 - idx))
        st_s[...] = st * decf + _tn_dot(kw.astype(BF16), v.astype(BF16))
        return carry

    lax.fori_loop(0, n_chunks, fwd, 0, unroll=4)

    st_s[...] = jnp.zeros((W, W), F32)

    def bwd(t, carry):
        c = n_chunks - 1 - t
        r0 = pl.multiple_of(c * C, C)
        q = qr_s[pl.ds(r0, C), :]
        k = kr_s[pl.ds(r0, C), :]
        v = v_ref[pl.ds(r0, C), :]
        st = st_s[...]
        qw = q * jnp.exp(lgb * (float(C) - idx))
        vb = v.astype(BF16)
        kb = k.astype(BF16)
        o = o_ref[pl.ds(r0, C), :] + jnp.dot(qw.astype(BF16), (st * bd).astype(BF16),
                                              preferred_element_type=F32)
        kw = k * jnp.exp(lgb * idx)
        st_s[...] = st * decb + _tn_dot(kw.astype(BF16), vb)
        for h in range(RET_HEADS):
            qm = jnp.where(head_masks[h], q, 0.0).astype(BF16)
            s = _nt_dot(qm, kb) * d_s[h]
            oh = jnp.dot(s.astype(BF16), vb, preferred_element_type=F32)
            o = o + jnp.where(head_masks[h], oh, 0.0)
        ms = _group_sum(o * o, bd.astype(BF16)) * (1.0 / HEAD_DIM)
        on = o * lax.rsqrt(ms + EPS) * gnw_ref[...]
        g = g_ref[pl.ds(r0, C), :]
        o_ref[pl.ds(r0, C), :] = (g * _sigmoid(g)) * on
        return carry

    lax.fori_loop(0, n_chunks, bwd, 0, unroll=4)


def _retention(rq, rk, rv, rg, cos, sin, log_decay, gn_w, B, S):
    W = RET_W
    lgl = jnp.repeat(log_decay, HEAD_DIM, axis=1)
    seq = pl.BlockSpec((S, W), lambda b: (b, 0), pipeline_mode=pl.Buffered(1))
    tab = pl.BlockSpec((S, W), lambda b: (0, 0), pipeline_mode=pl.Buffered(1))
    return pl.pallas_call(
        functools.partial(_ret_body, n_chunks=S // CHUNK),
        grid=(B,),
        in_specs=[pl.BlockSpec(memory_space=pltpu.SMEM), seq, seq, seq, seq, tab, tab,
                  pl.BlockSpec((2, W), lambda b: (0, 0)),
                  pl.BlockSpec((1, W), lambda b: (0, 0))],
        out_specs=pl.BlockSpec((S, W), lambda b: (b, 0)),
        out_shape=jax.ShapeDtypeStruct((B * S, W), F32),
        scratch_shapes=[pltpu.VMEM((S, W), F32), pltpu.VMEM((S, W), F32),
                        pltpu.VMEM((W, W), F32), pltpu.VMEM((RET_HEADS, CHUNK, CHUNK), F32)],
        compiler_params=_params("parallel"),
        name="retention",
    )(log_decay, rq, rk, rv, rg, cos, sin, lgl, gn_w)


def _lru_body(x_ref, gt_ref, cw_ref, cb_ref, gw_ref, gb_ref, lam_ref, o_ref, xc_s, h_s,
              *, n_blocks, tb):
    W = LRU_W
    S = n_blocks * tb
    z = -lam_ref[...]
    softplus = jnp.maximum(z, 0.0) + jnp.log(1.0 + jnp.exp(-jnp.abs(z)))
    cl = -LRU_C * softplus
    row = lax.broadcasted_iota(jnp.int32, (tb, 1), 0)
    left = LRU_CONV // 2

    def conv_block(kb):
        r0 = pl.multiple_of(kb * tb, tb)
        halo = F32_SUBLANES
        prev = x_ref[pl.ds(pl.multiple_of(jnp.maximum(r0 - halo, 0), halo), halo), :]
        nxt = x_ref[pl.ds(pl.multiple_of(jnp.minimum(r0 + tb, S - halo), halo), halo), :]
        ext = jnp.concatenate([prev, x_ref[pl.ds(r0, tb), :], nxt], axis=0)
        t = row + r0
        acc = cb_ref[...] + jnp.zeros((tb, W), F32)
        for j in range(LRU_CONV):
            off = j - left
            if off == 0:
                xs = ext[halo:halo + tb]
            else:
                xs = pltpu.roll(ext, (-off) % (tb + 2 * halo), 0)[halo:halo + tb]
                xs = jnp.where((t + off >= 0) & (t + off < S), xs, 0.0)
            acc = acc + xs * cw_ref[j:j + 1, :]
        return acc

    def gates(xc, d):
        g = jnp.dot(xc.astype(BF16), gw_ref[:, d * 2 * W:(d + 1) * 2 * W],
                    preferred_element_type=F32) + gb_ref[:, d * 2 * W:(d + 1) * 2 * W]
        r = _sigmoid(g[:, :W])
        i = _sigmoid(g[:, W:])
        log_a = cl[d:d + 1, :] * r
        a = jnp.exp(log_a)
        b = jnp.sqrt(-jnp.tanh(log_a) * (a * a + 1.0)) * (i * xc)
        return a, b

    def scan_block(a, b, reverse):
        d = 1
        while d < tb:
            if reverse:
                keep = row < tb - d
                a_sh = jnp.where(keep, pltpu.roll(a, tb - d, 0), 1.0)
                b_sh = jnp.where(keep, pltpu.roll(b, tb - d, 0), 0.0)
            else:
                keep = row >= d
                a_sh = jnp.where(keep, pltpu.roll(a, d, 0), 1.0)
                b_sh = jnp.where(keep, pltpu.roll(b, d, 0), 0.0)
            b = a * b_sh + b
            a = a * a_sh
            d *= 2
        return a, b

    def fwd(kb, carry):
        r0 = pl.multiple_of(kb * tb, tb)
        xc = conv_block(kb)
        xc_s[pl.ds(r0, tb), :] = xc
        a, b = gates(xc, 0)
        a, b = scan_block(a, b, False)
        h = a * carry + b
        h_s[pl.ds(r0, tb), :] = h
        return h[tb - 1:tb, :]

    lax.fori_loop(0, n_blocks, fwd, jnp.zeros((1, W), F32))

    def bwd(t, carry):
        kb = n_blocks - 1 - t
        r0 = pl.multiple_of(kb * tb, tb)
        xc = xc_s[pl.ds(r0, tb), :]
        a, b = gates(xc, 1)
        a, b = scan_block(a, b, True)
        h = a * carry + b
        o_ref[pl.ds(r0, tb), :] = (h_s[pl.ds(r0, tb), :] + h) * _gelu_tanh(gt_ref[pl.ds(r0, tb), :])
        return h[0:1, :]

    lax.fori_loop(0, n_blocks, bwd, jnp.zeros((1, W), F32))


def _lru(lx, lgt, conv_w, conv_b, gate_w, gate_b, lam, B, S, tb):
    W = LRU_W
    nb = W // HEAD_DIM
    eye = jnp.eye(nb, dtype=F32)
    gw = jnp.einsum('dgnkm,nj->nkdgjm', gate_w, eye).reshape(W, 4 * W).astype(BF16)
    gb = gate_b.reshape(1, 4 * W)
    seq = pl.BlockSpec((S, W), lambda b: (b, 0))
    const = lambda shape: pl.BlockSpec(shape, lambda b: (0,) * len(shape))
    return pl.pallas_call(
        functools.partial(_lru_body, n_blocks=S // tb, tb=tb),
        grid=(B,),
        in_specs=[seq, seq, const((LRU_CONV, W)), const((1, W)), const((W, 4 * W)),
                  const((1, 4 * W)), const((2, W))],
        out_specs=seq,
        out_shape=jax.ShapeDtypeStruct((B * S, W), F32),
        scratch_shapes=[pltpu.VMEM((S, W), F32), pltpu.VMEM((S, W), F32)],
        compiler_params=_params("parallel"),
        name="rglru",
    )(lx, lgt, conv_w, conv_b.reshape(1, W), gw, gb, lam)


def _attn_prep_body(q_ref, k_ref, v_ref, cos_ref, sin_ref, qnw_ref, knw_ref, vrep_ref, rept_ref,
                    qo_ref, kt_ref, vo_ref):
    def head_norm_rot(x, w, cs, sn):
        W = x.shape[-1]
        lane = lax.broadcasted_iota(jnp.int32, (1, W), 1)
        same_head = (lax.broadcasted_iota(jnp.int32, (W, W), 0) // HEAD_DIM
                     == lax.broadcasted_iota(jnp.int32, (W, W), 1) // HEAD_DIM)
        ms = _group_sum(x * x, same_head.astype(BF16)) * (1.0 / HEAD_DIM)
        xn = x * lax.rsqrt(ms + EPS) * w
        quarter = HEAD_DIM // 4
        rot = jnp.where((lane % (2 * quarter)) < quarter, -pltpu.roll(xn, W - quarter, 1),
                        pltpu.roll(xn, quarter, 1))
        return xn * cs + rot * sn

    cs = cos_ref[...]
    sn = sin_ref[...]
    q = head_norm_rot(q_ref[...], qnw_ref[...], cs, sn)
    qo_ref[...] = (q * (HEAD_DIM ** -0.5 * LOG2_E)).astype(BF16)
    k = head_norm_rot(k_ref[...], knw_ref[...], cs[:, :KV_W], sn[:, :KV_W]).astype(BF16)
    kt_ref[...] = _nt_dot(rept_ref[...], k).astype(BF16)
    VW = 2 * HEAD_DIM
    vlane = lax.broadcasted_iota(jnp.int32, (1, ATT_KV_HEADS * VW), 1)
    ones = jnp.where(vlane % VW >= HEAD_DIM, 1.0, 0.0)
    vo_ref[...] = (jnp.dot(v_ref[...].astype(BF16), vrep_ref[...], preferred_element_type=F32)
                   + ones).astype(BF16)


def _attn_prep(aq, ak, av, cos, sin, qn_w, kn_w, B, S, tm):
    T = B * S
    ns = S // tm
    GW = ATT_GROUP * HEAD_DIM
    src = jnp.arange(ATT_KV_HEADS * GW)
    src = (src // GW) * HEAD_DIM + src % HEAD_DIM
    rep = (jnp.arange(KV_W)[:, None] == src[None, :]).astype(BF16)
    VW = 2 * HEAD_DIM
    vsrc = jnp.arange(ATT_KV_HEADS * VW)
    vsrc = jnp.where(vsrc % VW < HEAD_DIM, (vsrc // VW) * HEAD_DIM + vsrc % VW, -1)
    vrep = (jnp.arange(KV_W)[:, None] == vsrc[None, :]).astype(BF16)
    return pl.pallas_call(
        _attn_prep_body,
        grid=(B, ns),
        in_specs=[pl.BlockSpec((tm, ATT_W), lambda b, s: (b * ns + s, 0)),
                  pl.BlockSpec((tm, KV_W), lambda b, s: (b * ns + s, 0)),
                  pl.BlockSpec((tm, KV_W), lambda b, s: (b * ns + s, 0)),
                  pl.BlockSpec((tm, ATT_W), lambda b, s: (s, 0)),
                  pl.BlockSpec((tm, ATT_W), lambda b, s: (s, 0)),
                  pl.BlockSpec((1, ATT_W), lambda b, s: (0, 0)),
                  pl.BlockSpec((1, KV_W), lambda b, s: (0, 0)),
                  pl.BlockSpec((KV_W, ATT_KV_HEADS * VW), lambda b, s: (0, 0)),
                  pl.BlockSpec((ATT_KV_HEADS * GW, KV_W), lambda b, s: (0, 0))],
        out_specs=[pl.BlockSpec((tm, ATT_W), lambda b, s: (b * ns + s, 0)),
                   pl.BlockSpec((None, ATT_KV_HEADS * GW, tm), lambda b, s: (b, 0, s)),
                   pl.BlockSpec((tm, ATT_KV_HEADS * VW), lambda b, s: (b * ns + s, 0))],
        out_shape=[jax.ShapeDtypeStruct((T, ATT_W), BF16),
                   jax.ShapeDtypeStruct((B, ATT_KV_HEADS * GW, S), BF16),
                   jax.ShapeDtypeStruct((T, ATT_KV_HEADS * VW), BF16)],
        compiler_params=_params("parallel", "parallel"),
        name="attn_prep",
    )(aq, ak, av, cos, sin, jnp.tile(qn_w, ATT_HEADS).reshape(1, ATT_W),
      jnp.tile(kn_w, ATT_KV_HEADS).reshape(1, KV_W), vrep, rep.T)


def _attn_body(q_ref, kt_ref, v_ref, o_ref, *, tq, tk, nk):
    GW = ATT_GROUP * HEAD_DIM
    lane = lax.broadcasted_iota(jnp.int32, (1, GW), 1)
    masks = [(lane // HEAD_DIM) == g for g in range(ATT_GROUP)]
    q = q_ref[...]
    zero = jnp.zeros_like(q)
    qs = jnp.concatenate([jnp.where(m, q, zero) for m in masks], axis=0)
    rows = ATT_GROUP * tq
    VW = 2 * HEAD_DIM
    m_run = jnp.full((rows, 1), NEG_INF, F32)
    acc = jnp.zeros((rows, VW), F32)
    for c in range(nk):
        s = jnp.dot(qs, kt_ref[:, c * tk:(c + 1) * tk], preferred_element_type=F32)
        m_new = jnp.maximum(m_run, jnp.max(s, axis=-1, keepdims=True))
        alpha = jnp.exp2(m_run - m_new)
        p = jnp.exp2(s - m_new)
        acc = alpha * acc + jnp.dot(p.astype(BF16), v_ref[c * tk:(c + 1) * tk, :],
                                    preferred_element_type=F32)
        m_run = m_new
    on = acc * pltpu.roll(1.0 / acc, HEAD_DIM, 1)
    low = lax.broadcasted_iota(jnp.int32, (1, VW), 1) < HEAD_DIM
    heads = [on[g * tq:(g + 1) * tq] for g in range(ATT_GROUP)]
    for pair in range(ATT_GROUP // 2):
        o_ref[:, pair * VW:(pair + 1) * VW] = jnp.where(
            low, heads[2 * pair], pltpu.roll(heads[2 * pair + 1], HEAD_DIM, 1))


def _attention(qp, kt, vp, B, S, tq, tk):
    T = B * S
    nq = S // tq
    GW = ATT_GROUP * HEAD_DIM
    return pl.pallas_call(
        functools.partial(_attn_body, tq=tq, tk=tk, nk=S // tk),
        grid=(B, ATT_KV_HEADS, nq),
        in_specs=[pl.BlockSpec((tq, GW), lambda b, h, i: (b * nq + i, h)),
                  pl.BlockSpec((None, GW, S), lambda b, h, i: (b, h, 0)),
                  pl.BlockSpec((S, 2 * HEAD_DIM), lambda b, h, i: (b, h))],
        out_specs=pl.BlockSpec((tq, GW), lambda b, h, i: (b * nq + i, h)),
        out_shape=jax.ShapeDtypeStruct((T, ATT_W), F32),
        compiler_params=_params("parallel", "parallel", "parallel"),
        name="attention",
    )(qp, kt, vp)


def _outproj_body(x_ref, oret_ref, olru_ref, oatt_ref, w_ref, ln2_ref, wq_ref,
                  xo_ref, xn_ref, q_ref):
    y = jnp.dot(oret_ref[...].astype(BF16), w_ref[0:RET_W, :], preferred_element_type=F32)
    y = y + jnp.dot(olru_ref[...].astype(BF16), w_ref[RET_W:RET_W + LRU_W, :],
                    preferred_element_type=F32)
    y = y + jnp.dot(oatt_ref[...].astype(BF16), w_ref[RET_W + LRU_W:, :],
                    preferred_element_type=F32)
    x = x_ref[...] + y
    xo_ref[...] = x
    xn = _rms(x, ln2_ref[...]).astype(BF16)
    xn_ref[...] = xn.astype(F32).T.astype(BF16)
    q_ref[...] = jnp.dot(xn, wq_ref[...], preferred_element_type=F32)


def _outproj(x2, o_ret, o_lru, o_att, w_out_bf, ln2, wq_bf, tm):
    T, D = x2.shape
    QW = wq_bf.shape[1]
    MW = w_out_bf.shape[0]
    tok = lambda w: pl.BlockSpec((tm, w), lambda i: (i, 0))
    return pl.pallas_call(
        _outproj_body,
        grid=(T // tm,),
        in_specs=[tok(D), tok(RET_W), tok(LRU_W), tok(ATT_W),
                  pl.BlockSpec((MW, D), lambda i: (0, 0)),
                  pl.BlockSpec((1, D), lambda i: (0, 0)),
                  pl.BlockSpec((D, QW), lambda i: (0, 0))],
        out_specs=[tok(D), pl.BlockSpec((D, tm), lambda i: (0, i)), tok(QW)],
        out_shape=[jax.ShapeDtypeStruct((T, D), F32), jax.ShapeDtypeStruct((D, T), BF16),
                   jax.ShapeDtypeStruct((T, QW), F32)],
        compiler_params=_params("parallel"),
        name="outproj",
    )(x2, o_ret, o_lru, o_att, w_out_bf, ln2, wq_bf)


def _sorting_network(n):
    pairs = []
    p = 1
    while p < n:
        k = p
        while k >= 1:
            for j in range(k % p, n - k, 2 * k):
                for i in range(min(k, n - j - k)):
                    if (i + j) // (2 * p) == (i + j + k) // (2 * p):
                        pairs.append((i + j, i + j + k))
            k //= 2
        p *= 2
    return pairs


def _pair_counts(n):
    return [n // (i + 1) for i in range(n)]


def _peer_route_body(q_ref, keys_ref, cnt_ref, w0_ref, rk_ref, w1_ref, a_s, b_s, f_s,
                     *, half, topk):
    s0 = _nt_dot(keys_ref[0], q_ref[:, :half])
    s1 = _nt_dot(keys_ref[1], q_ref[:, half:])

    def top_sorted(tiles, n, out_ref):
        rows = list(tiles) + [None] * (pl.next_power_of_2(len(tiles)) - len(tiles))
        for lo_i, hi_i in _sorting_network(len(rows)):
            x, y = rows[lo_i], rows[hi_i]
            if y is None:
                continue
            if x is None:
                rows[lo_i], rows[hi_i] = y, None
            else:
                rows[lo_i], rows[hi_i] = jnp.maximum(x, y), jnp.minimum(x, y)
        rows = [t for t in rows if t is not None]
        for r in range(n):
            m = jnp.max(rows[0], axis=0, keepdims=True)
            out_ref[r:r + 1, :] = m
            hit = rows[0] == m
            for v in range(min(len(rows), n - r - 1)):
                nxt = rows[v + 1] if v + 1 < len(rows) else NEG_INF
                rows[v] = jnp.where(hit, nxt, rows[v])

    def sublane_tiles(x):
        sl = F32_SUBLANES
        return [x[sl * v:sl * (v + 1), :] for v in range(x.shape[0] // sl)]

    n = topk + 1
    top_sorted(sublane_tiles(s0), n, a_s)
    top_sorted(sublane_tiles(s1), n, b_s)
    a = a_s[0:topk, :]
    b = b_s[0:topk, :]
    counts = _pair_counts(n)
    sl = F32_SUBLANES
    row8 = lax.broadcasted_iota(jnp.int32, (sl, 1), 0)
    parts = [a[0:1, :] + b[0:sl, :], a[0:1, :] + b[sl:2 * sl, :]]
    i = 1
    while counts[i] > 1:
        parts.append(jnp.where(row8 < counts[i], a[i:i + 1, :] + b[0:sl, :], NEG_INF))
        i += 1
    parts.append(a[i:, :] + b[0:1, :])
    last = jnp.where(row8 == 0, a[0:1, :] + b_s[topk:n, :],
                     jnp.where(row8 == 1, a_s[topk:n, :] + b[0:1, :], NEG_INF))
    parts.append(last)
    top_sorted(parts, n, f_s)
    f = f_s[0:topk, :]
    z = jnp.sum(jnp.exp(f - f[0:1, :]), axis=0, keepdims=True)
    tau = 0.5 * (f[topk - 1:topk, :] + f_s[topk:n, :])
    thr = tau - s0
    cnt = jnp.zeros_like(thr)
    for r in range(topk):
        cnt = jnp.where(b[r:r + 1, :] >= thr, float(r + 1), cnt)
    cnt_ref[...] = cnt
    rank1 = jnp.zeros_like(s1)
    for r in range(topk):
        rank1 = jnp.where(b[r:r + 1, :] > s1, float(r + 1), rank1)
    w0_ref[...] = jnp.exp(s0 - a[0:1, :]) * (0.5 / z)
    rk_ref[...] = rank1.astype(BF16)
    w1_ref[...] = jnp.exp(s1 - b[0:1, :]).astype(BF16)


def _peer_route_heads(q_ref, keys_ref, cnt_ref, w0_ref, rk_ref, w1_ref, a_s, b_s, f_s,
                      *, half, topk, hp):
    for k in range(hp):
        _peer_route_body(q_ref.at[:, k * 2 * half:(k + 1) * 2 * half], keys_ref.at[k],
                         cnt_ref.at[k], w0_ref.at[k], rk_ref.at[k], w1_ref.at[k],
                         a_s.at[k], b_s.at[k], f_s.at[k], half=half, topk=topk)


def _peer_route(q, keys, tm, hp):
    T = q.shape[0]
    H, _, NK, half = keys.shape
    sl = F32_SUBLANES
    counts = _pair_counts(PEER_TOPK + 1)
    assert PEER_TOPK == 2 * sl and counts[1] <= sl and counts[sl - 1] > 1 >= counts[sl]
    assert NK > PEER_TOPK and NK % sl == 0
    per_key = jax.ShapeDtypeStruct((H, NK, T), F32)
    per_col = jax.ShapeDtypeStruct((H, NK, T), BF16)
    ospec = pl.BlockSpec((hp, NK, tm), lambda i, h: (h, 0, i))
    return pl.pallas_call(
        functools.partial(_peer_route_heads, half=half, topk=PEER_TOPK, hp=hp),
        grid=(T // tm, H // hp),
        in_specs=[pl.BlockSpec((tm, hp * 2 * half), lambda i, h: (i, h)),
                  pl.BlockSpec((hp, 2, NK, half), lambda i, h: (h, 0, 0, 0))],
        out_specs=[ospec, ospec, ospec, ospec],
        out_shape=[per_key, per_key, per_col, per_col],
        scratch_shapes=[pltpu.VMEM((hp, PEER_TOPK + sl, tm), F32)] * 3,
        compiler_params=_params("parallel", "parallel"),
        name="peer_route",
    )(q, keys)


def _peer_mix_body(xnt_ref, xres_ref, u_ref, vt_ref, cnt_ref, w0_ref, rk_ref, w1_ref, o_ref,
                   h_s, gh_s, acc_s, *, tl):
    H, ti, tm = cnt_ref.shape
    NK = rk_ref.shape[1]
    j = pl.program_id(1)

    last = pl.num_programs(1) - 1
    cur = j % 2

    def fold():
        acc_s[...] += jnp.dot(vt_ref[...], gh_s[1 - cur], preferred_element_type=F32)

    def row_tile(row):
        tile = jnp.broadcast_to(row, (BF16_SUBLANES, row.shape[1])).astype(BF16)
        return pltpu.repeat(tile, NK // BF16_SUBLANES, axis=0)

    def build():
        for ii in range(ti):
            rs = slice(ii * NK, (ii + 1) * NK)
            for lc in range(tm // tl):
                ls = slice(lc * tl, (lc + 1) * tl)
                g = None
                for h in range(H):
                    c = row_tile(cnt_ref[h, ii:ii + 1, ls])
                    a = row_tile(w0_ref[h, ii:ii + 1, ls])
                    term = a * jnp.where(rk_ref[h, :, ls] < c, w1_ref[h, :, ls], 0)
                    g = term if g is None else g + term
                gh_s[cur, rs, ls] = g
        h_s[...] = jnp.dot(u_ref[...], xnt_ref[...], preferred_element_type=F32)
        for ii in range(ti):
            rs = slice(ii * NK, (ii + 1) * NK)
            gh_s[cur, rs, :] = gh_s[cur, rs, :] * _gelu_tanh_x2(h_s[rs, :].astype(BF16))

    @pl.when(j == 0)
    def _():
        acc_s[...] = jnp.zeros_like(acc_s)
        build()

    @pl.when(jnp.logical_and(j > 0, j < last))
    def _():
        fold()
        build()

    @pl.when(j == last)
    def _():
        fold()
        o_ref[...] = xres_ref[...] + acc_s[...].T


def _peer_mix(xnt, xres, u_bf, vt_bf, cnt, w0, rk, w1, tm, ti, tl):
    D, T = xnt.shape
    H, NK, _ = cnt.shape
    E = u_bf.shape[0]
    te = ti * NK
    n_e = E // te
    build = lambda j: jnp.minimum(j, n_e - 1)
    fold = lambda j: jnp.maximum(j - 1, 0)
    keys = pl.BlockSpec((H, ti, tm), lambda i, j: (0, build(j), i))
    full = pl.BlockSpec((H, NK, tm), lambda i, j: (0, 0, i))
    return pl.pallas_call(
        functools.partial(_peer_mix_body, tl=tl),
        grid=(T // tm, n_e + 1),
        in_specs=[pl.BlockSpec((D, tm), lambda i, j: (0, i)),
                  pl.BlockSpec((tm, D), lambda i, j: (i, 0)),
                  pl.BlockSpec((te, D), lambda i, j: (build(j), 0)),
                  pl.BlockSpec((D, te), lambda i, j: (0, fold(j))),
                  keys, keys, full, full],
        out_specs=pl.BlockSpec((tm, D), lambda i, j: (i, 0)),
        out_shape=jax.ShapeDtypeStruct((T, D), F32),
        scratch_shapes=[pltpu.VMEM((te, tm), F32), pltpu.VMEM((2, te, tm), BF16),
                        pltpu.VMEM((D, tm), F32)],
        compiler_params=_params("parallel", "arbitrary"),
        name="peer_mix",
    )(xnt, xres, u_bf, vt_bf, cnt, w0, rk, w1)


def _final_norm_body(x_ref, w_ref, o_ref):
    o_ref[...] = _rms(x_ref[...], w_ref[...])


def _final_norm(x2, w, tm):
    T, D = x2.shape
    return pl.pallas_call(
        _final_norm_body,
        grid=(T // tm,),
        in_specs=[pl.BlockSpec((tm, D), lambda i: (i, 0)), pl.BlockSpec((1, D), lambda i: (0, 0))],
        out_specs=pl.BlockSpec((tm, D), lambda i: (i, 0)),
        out_shape=jax.ShapeDtypeStruct((T, D), F32),
        compiler_params=_params("parallel"),
        name="final_norm",
    )(x2, w)


def _rope_table(pos, inv_freq):
    ang = pos[:, None] * inv_freq[None, :]
    ang = jnp.concatenate([ang, ang], axis=-1)
    return jnp.cos(ang), jnp.sin(ang)


def _position_tables(S):
    t = jnp.arange(S, dtype=F32)
    n_rows = S // GRID_W
    rows = jnp.repeat(jnp.arange(n_rows, dtype=F32), GRID_W)
    cols = jnp.tile(jnp.arange(GRID_W, dtype=F32), n_rows)
    ret_inv = 1.0 / (10000.0 ** jnp.linspace(0.0, 1.0, HEAD_DIM // 2, dtype=F32))
    ret_cos, ret_sin = _rope_table(t, ret_inv)
    ax_n = HEAD_DIM // 4
    ax_inv = ROPE_BASE ** (-jnp.arange(ax_n, dtype=F32) / ax_n)
    cr, sr = _rope_table(rows, ax_inv)
    cc, sc = _rope_table(cols, ax_inv)
    ax_cos = jnp.concatenate([cr, cc], axis=-1)
    ax_sin = jnp.concatenate([sr, sc], axis=-1)
    return (jnp.tile(ret_cos, (1, RET_HEADS)), jnp.tile(ret_sin, (1, RET_HEADS)),
            jnp.tile(ax_cos, (1, ATT_HEADS)), jnp.tile(ax_sin, (1, ATT_HEADS)))


def _tiles(B, S, n_keys):
    T = B * S
    return dict(
        tm=min(512, T),
        tb=min(256, S),
        tp=min(512, S),
        tq=min(256, S),
        tk=min(1024, S),
        tr=min(256, T),
        hp=4,
        tmix=min(512, T),
        ti=min(8, n_keys),
        tl=min(256, T),
    )


def kernel(x, ln1_w, w_in, ret_log_decay, ret_gn_w, lru_conv_w, lru_conv_b, lru_gate_w, lru_gate_b,
           lru_lambda, attn_q_norm, attn_k_norm, w_out, ln2_w, peer_wq, peer_keys, peer_u, peer_v,
           lnf_w):
    B, S, D = x.shape
    T = B * S
    depth = w_in.shape[0]
    n_keys = peer_keys.shape[3]
    tl = _tiles(B, S, n_keys)
    widths = (RET_W,) * 4 + (LRU_W,) * 2 + (ATT_W, KV_W, KV_W)
    ret_cos, ret_sin, ax_cos, ax_sin = _position_tables(S)
    x2 = x.reshape(T, D)
    for l in range(depth):
        rq, rk, rv, rg, lx, lgt, aq, ak, av = _inproj(
            x2, ln1_w[l].reshape(1, D), w_in[l].astype(BF16), widths, tl["tm"])
        o_ret = _retention(rq, rk, rv, rg, ret_cos, ret_sin, ret_log_decay[l],
                           ret_gn_w[l].reshape(1, RET_W), B, S)
        o_lru = _lru(lx, lgt, lru_conv_w[l], lru_conv_b[l], lru_gate_w[l], lru_gate_b[l],
                     lru_lambda[l], B, S, tl["tb"])
        qp, kt, vp = _attn_prep(aq, ak, av, ax_cos, ax_sin, attn_q_norm[l], attn_k_norm[l],
                                B, S, tl["tp"])
        o_att = _attention(qp, kt, vp, B, S, tl["tq"], tl["tk"])
        x2, xn, q = _outproj(x2, o_ret, o_lru, o_att, w_out[l].astype(BF16),
                             ln2_w[l].reshape(1, D), peer_wq[l].astype(BF16), tl["tm"])
        cnt, w0, rk, w1 = _peer_route(q, peer_keys[l], tl["tr"], tl["hp"])
        x2 = _peer_mix(xn, x2, peer_u[l].astype(BF16), peer_v[l].T.astype(BF16),
                       cnt, w0, rk, w1, tl["tmix"], tl["ti"], tl["tl"])
    return _final_norm(x2, lnf_w.reshape(1, D), tl["tm"]).reshape(B, S, D)
```
